```python
import math
import jax, jax.numpy as jnp
from jax import lax
import numpy as np

D_MODEL = 2048
BATCH = 2
SEQ = 16384
DEPTH = 2
DEC_BATCH = 4
DEC_SEQ = 4096
PAST_LEN = 128

GRID_W = 64
PLE_DIM = 256
W_A = 512
S5_H = 16
S5_G = W_A // S5_H
S5_P = 64
W_B = 512
CONV_K = 3
N_HEADS = 8
HEAD_DIM = 64
W_C = N_HEADS * HEAD_DIM
NA_ROWS = 8
NA_COLS = 16
QCB = 16
KCB = 32
W_D = 512
FNET_GROUPS = 4
FNET_GW = W_D // FNET_GROUPS
N_BRANCH = 4
MIX_IN = W_A + 3 * W_B + 3 * W_C + W_D
IN_COLS = MIX_IN + N_BRANCH * D_MODEL
D_FF = -(-8 * D_MODEL // (3 * 256)) * 256
EPS = 1e-6
NEG_INF = -1e30

kernel_name = 'hybrid_bidir_s5_conv_natten_fnet_encoder'


def rmsnorm(x, g):
    xf = x.astype(jnp.float32)
    y = xf * lax.rsqrt(jnp.mean(xf * xf, axis=-1, keepdims=True) + EPS)
    return (y * g.astype(jnp.float32)).astype(x.dtype)


def cmul(ar, ai, br, bi):
    return ar * br - ai * bi, ar * bi + ai * br


def s5_scan(uf, lam_re, lam_im, log_dt, b_re, b_im, c_re, c_im, reverse):
    lam_re = lam_re.astype(jnp.float32)
    lam_im = lam_im.astype(jnp.float32)
    dt = jnp.exp(log_dt.astype(jnp.float32))[:, None]
    mag = jnp.exp(lam_re * dt)
    ar = mag * jnp.cos(lam_im * dt)
    ai = mag * jnp.sin(lam_im * dt)
    den = lam_re * lam_re + lam_im * lam_im
    fr = ((ar - 1.0) * lam_re + ai * lam_im) / den
    fi = (ai * lam_re - (ar - 1.0) * lam_im) / den
    bbr, bbi = cmul(fr[..., None], fi[..., None], b_re.astype(jnp.float32), b_im.astype(jnp.float32))
    bur = jnp.einsum('blgh,gph->blgp', uf, bbr)
    bui = jnp.einsum('blgh,gph->blgp', uf, bbi)
    a_r = jnp.broadcast_to(ar, bur.shape)
    a_i = jnp.broadcast_to(ai, bur.shape)

    def combine(e1, e2):
        a1r, a1i, b1r, b1i = e1
        a2r, a2i, b2r, b2i = e2
        nar, nai = cmul(a2r, a2i, a1r, a1i)
        tr, ti = cmul(a2r, a2i, b1r, b1i)
        return nar, nai, tr + b2r, ti + b2i

    _, _, xr, xi = lax.associative_scan(combine, (a_r, a_i, bur, bui), reverse=reverse, axis=1)
    return (jnp.einsum('blgp,ghp->blgh', xr, c_re.astype(jnp.float32))
            - jnp.einsum('blgp,ghp->blgh', xi, c_im.astype(jnp.float32)))


def s5_mixer(u, lam_re, lam_im, log_dt, b_re, b_im, c_re, c_im, d_skip, w_glu):
    bn, L, _ = u.shape
    uf = u.astype(jnp.float32).reshape(bn, L, S5_G, S5_H)
    y = d_skip.astype(jnp.float32).reshape(S5_G, S5_H) * uf
    for direction, rev in ((0, False), (1, True)):
        y = y + s5_scan(uf, lam_re[direction], lam_im[direction], log_dt[direction],
                        b_re[direction], b_im[direction], c_re[direction], c_im[direction], rev)
    y = jax.nn.gelu(y.reshape(bn, L, W_A))
    y = y * jax.nn.sigmoid(y @ w_glu.astype(jnp.float32))
    return y.astype(u.dtype)


def short_conv_mixer(bg, cg, v, conv_w):
    z = cg * v
    y = lax.conv_general_dilated(z, conv_w[:, None, :].astype(z.dtype), window_strides=(1,),
                                 padding=((CONV_K // 2, CONV_K // 2),),
                                 dimension_numbers=('NWC', 'WIO', 'NWC'),
                                 feature_group_count=W_B)
    return bg * y


def neighbourhood_attention(q, k, v, rel_bias):
    bn, L = q.shape[0], q.shape[1]
    rows = L // GRID_W
    wr = min(NA_ROWS, rows)
    n_cb = GRID_W // QCB
    r = np.arange(rows)
    rs = np.clip(r - wr // 2, 0, rows - wr)
    row_idx = rs[:, None] + np.arange(wr)[None, :]
    dr_idx = row_idx - r[:, None] + (NA_ROWS - 1)
    j = np.arange(n_cb)
    kc0 = np.clip(j * QCB - NA_COLS // 2, 0, GRID_W - KCB)
    col_idx = kc0[:, None] + np.arange(KCB)[None, :]
    qc = j[:, None] * QCB + np.arange(QCB)[None, :]
    cs = np.clip(qc - NA_COLS // 2, 0, GRID_W - NA_COLS)
    kcol = col_idx[:, None, :]
    valid = (kcol >= cs[:, :, None]) & (kcol < cs[:, :, None] + NA_COLS)
    dc_idx = np.clip(kcol - qc[:, :, None] + (NA_COLS - 1), 0, 2 * NA_COLS - 2)

    qg = q.reshape(bn, rows, n_cb, QCB, N_HEADS, HEAD_DIM)
    kg = k.reshape(bn, rows, GRID_W, N_HEADS, HEAD_DIM)
    vg = v.reshape(bn, rows, GRID_W, N_HEADS, HEAD_DIM)
    gi_r = row_idx[:, None, :, None]
    gi_c = col_idx[None, :, None, :]
    kn = kg[:, gi_r, gi_c]
    vn = vg[:, gi_r, gi_c]
    s = jnp.einsum('brjqhd,brjakhd->brjhqak', qg, kn,
                   preferred_element_type=jnp.float32) * (1.0 / math.sqrt(HEAD_DIM))
    bias = rel_bias.astype(jnp.float32)[:, dr_idx[:, None, None, :, None], dc_idx[None, :, :, None, :]]
    bias = jnp.transpose(bias, (1, 2, 0, 3, 4, 5))
    s = jnp.where(valid[None, :, None, :, None, :], s + bias, NEG_INF)
    pr = jax.nn.softmax(s.reshape(bn, rows, n_cb, N_HEADS, QCB, wr * KCB), axis=-1)
    pr = pr.reshape(bn, rows, n_cb, N_HEADS, QCB, wr, KCB).astype(v.dtype)
    o = jnp.einsum('brjhqak,brjakhd->brjqhd', pr, vn)
    return o.reshape(bn, L, W_C)


def fourier_mixer(u):
    bn, L, _ = u.shape
    ug = u.astype(jnp.float32).reshape(bn, L, FNET_GROUPS, FNET_GW)
    f = jnp.fft.fftn(ug, axes=(1, 3), norm='ortho').real
    return f.reshape(bn, L, W_D).astype(u.dtype)


def encoder_layer(x, p, g_mix, w_in, s5_lam_re, s5_lam_im, s5_log_dt, s5_b_re, s5_b_im,
                  s5_c_re, s5_c_im, s5_d, w_glu, conv_w, q_gain, k_gain, rel_bias,
                  w_br, w_o, g_ffn, w_ffn_in, w_ffn_out, g_ple, w_ple_gate, w_ple_proj):
    bn, L, _ = x.shape
    h = rmsnorm(x, g_mix)
    z = h @ w_in[:, :MIX_IN]
    offs = [W_A, W_A + W_B, W_A + 2 * W_B, W_A + 3 * W_B,
            W_A + 3 * W_B + W_C, W_A + 3 * W_B + 2 * W_C, W_A + 3 * W_B + 3 * W_C]
    u_a, b_g, c_g, v_b, q, k, v_c, u_d = jnp.split(z, offs, axis=-1)
    y_a = s5_mixer(u_a, s5_lam_re, s5_lam_im, s5_log_dt, s5_b_re, s5_b_im, s5_c_re, s5_c_im, s5_d, w_glu)
    y_b = short_conv_mixer(b_g, c_g, v_b, conv_w)
    qh = rmsnorm(q.reshape(bn, L, N_HEADS, HEAD_DIM), q_gain)
    kh = rmsnorm(k.reshape(bn, L, N_HEADS, HEAD_DIM), k_gain)
    vh = v_c.reshape(bn, L, N_HEADS, HEAD_DIM)
    y_c = neighbourhood_attention(qh, kh, vh, rel_bias)
    y_d = fourier_mixer(u_d)
    ys = (y_a, y_b, y_c, y_d)
    merged = None
    for kb in range(N_BRANCH):
        gate = jax.nn.sigmoid(h @ w_in[:, MIX_IN + kb * D_MODEL: MIX_IN + (kb + 1) * D_MODEL])
        term = gate * (ys[kb] @ w_br[kb])
        merged = term if merged is None else merged + term
    x = x + merged @ w_o
    h2 = rmsnorm(x, g_ffn)
    a, b = jnp.split(h2 @ w_ffn_in, 2, axis=-1)
    x = x + (jax.nn.silu(a) * b) @ w_ffn_out
    pg = jax.nn.sigmoid(rmsnorm(x, g_ple) @ w_ple_gate)
    x = x + pg * (p @ w_ple_proj)
    return x


def setup_inputs(seed: int = 0) -> dict:
    key = jax.random.key(seed)
    ks = iter(jax.random.split(key, 40))
    f32 = jnp.float32

    def nrm(shape, scale):
        return jax.random.normal(next(ks), shape, f32) * scale

    x_prompt = nrm((BATCH, SEQ, D_MODEL), 1.0)
    x_sample = nrm((DEC_BATCH, DEC_SEQ, D_MODEL), 1.0)
    p_prompt = nrm((DEPTH, BATCH, SEQ, PLE_DIM), 1.0)
    p_sample = nrm((DEPTH, DEC_BATCH, DEC_SEQ, PLE_DIM), 1.0)
    g_mix = 1.0 + nrm((DEPTH, D_MODEL), 0.02)
    w_in = nrm((DEPTH, D_MODEL, IN_COLS), D_MODEL ** -0.5)
    n_idx = jnp.arange(S5_P, dtype=f32)
    s5_lam_re = -0.5 + nrm((DEPTH, 2, S5_G, S5_P), 0.01)
    s5_lam_im = math.pi * n_idx + nrm((DEPTH, 2, S5_G, S5_P), 0.01)
    s5_log_dt = jax.random.uniform(next(ks), (DEPTH, 2, S5_G), f32, math.log(1e-3), math.log(1e-1))
    s5_b_re = nrm((DEPTH, 2, S5_G, S5_P, S5_H), (2 * S5_H) ** -0.5)
    s5_b_im = nrm((DEPTH, 2, S5_G, S5_P, S5_H), (2 * S5_H) ** -0.5)
    s5_c_re = nrm((DEPTH, 2, S5_G, S5_H, S5_P), (2 * S5_P) ** -0.5)
    s5_c_im = nrm((DEPTH, 2, S5_G, S5_H, S5_P), (2 * S5_P) ** -0.5)
    s5_d = nrm((DEPTH, W_A), 1.0)
    w_glu = nrm((DEPTH, W_A, W_A), W_A ** -0.5)
    conv_w = nrm((DEPTH, CONV_K, W_B), CONV_K ** -0.5)
    q_gain = 1.0 + nrm((DEPTH, HEAD_DIM), 0.02)
    k_gain = 1.0 + nrm((DEPTH, HEAD_DIM), 0.02)
    rel_bias = nrm((DEPTH, N_HEADS, 2 * NA_ROWS - 1, 2 * NA_COLS - 1), 0.02)
    w_br = nrm((DEPTH, N_BRANCH, W_A, D_MODEL), W_A ** -0.5)
    w_o = nrm((DEPTH, D_MODEL, D_MODEL), D_MODEL ** -0.5)
    g_ffn = 1.0 + nrm((DEPTH, D_MODEL), 0.02)
    w_ffn_in = nrm((DEPTH, D_MODEL, 2 * D_FF), D_MODEL ** -0.5)
    w_ffn_out = nrm((DEPTH, D_FF, D_MODEL), D_FF ** -0.5)
    g_ple = 1.0 + nrm((DEPTH, D_MODEL), 0.02)
    w_ple_gate = nrm((DEPTH, D_MODEL, D_MODEL), D_MODEL ** -0.5)
    w_ple_proj = nrm((DEPTH, PLE_DIM, D_MODEL), PLE_DIM ** -0.5)
    return {'x_prompt': x_prompt, 'x_sample': x_sample, 'p_prompt': p_prompt, 'p_sample': p_sample,
            'g_mix': g_mix, 'w_in': w_in, 's5_lam_re': s5_lam_re, 's5_lam_im': s5_lam_im,
            's5_log_dt': s5_log_dt, 's5_b_re': s5_b_re, 's5_b_im': s5_b_im, 's5_c_re': s5_c_re,
            's5_c_im': s5_c_im, 's5_d': s5_d, 'w_glu': w_glu, 'conv_w': conv_w, 'q_gain': q_gain,
            'k_gain': k_gain, 'rel_bias': rel_bias, 'w_br': w_br, 'w_o': w_o, 'g_ffn': g_ffn,
            'w_ffn_in': w_ffn_in, 'w_ffn_out': w_ffn_out, 'g_ple': g_ple, 'w_ple_gate': w_ple_gate,
            'w_ple_proj': w_ple_proj}


def reference(x_prompt, x_sample, p_prompt, p_sample, g_mix, w_in, s5_lam_re, s5_lam_im,
              s5_log_dt, s5_b_re, s5_b_im, s5_c_re, s5_c_im, s5_d, w_glu, conv_w, q_gain,
              k_gain, rel_bias, w_br, w_o, g_ffn, w_ffn_in, w_ffn_out, g_ple, w_ple_gate,
              w_ple_proj):
    def trunk(x, p):
        for i in range(DEPTH):
            x = encoder_layer(x, p[i], g_mix[i], w_in[i], s5_lam_re[i], s5_lam_im[i], s5_log_dt[i],
                              s5_b_re[i], s5_b_im[i], s5_c_re[i], s5_c_im[i], s5_d[i], w_glu[i],
                              conv_w[i], q_gain[i], k_gain[i], rel_bias[i], w_br[i], w_o[i],
                              g_ffn[i], w_ffn_in[i], w_ffn_out[i], g_ple[i], w_ple_gate[i],
                              w_ple_proj[i])
        return x

    y_prompt = trunk(x_prompt, p_prompt)
    y_sample = trunk(x_sample, p_sample)
    return (y_prompt, y_sample)
```

```python
import functools
import math

import numpy as np
import jax
import jax.numpy as jnp
from jax import lax
from jax.experimental import pallas as pl
from jax.experimental.pallas import tpu as pltpu

F32 = jnp.float32
BF16 = jnp.bfloat16

D_MODEL = 2048
BW = 512
MIX_IN = 8 * BW
N_BRANCH = 4
S5_H = 16
S5_G = BW // S5_H
S5_P = 64
S5_CHUNK = 16
LANE = 128
S5_JB = BW // LANE
S5_GB = LANE // S5_H
S5_CW = S5_CHUNK * LANE
N_HEADS = 8
HEAD_DIM = 64
GRID_W = 64
NA_ROWS = 8
NA_COLS = 16
FNET_GW = 128
D_FF = 5632
PLE_DIM = 256
EPS = 1e-6
NEG_INF = -1e30
VMEM_LIMIT = 56 * 1024 * 1024


def _cparams(sem):
    return pltpu.CompilerParams(dimension_semantics=sem, vmem_limit_bytes=VMEM_LIMIT)


def _rms(x, g):
    ms = jnp.mean(x * x, axis=-1, keepdims=True)
    return x * lax.rsqrt(ms + EPS) * g


def _dot(a, b):
    return jnp.dot(a, b, preferred_element_type=F32)


def _inproj_body(x_ref, g_ref, w_ref, z_ref, h_scr):
    @pl.when(pl.program_id(1) == 0)
    def _():
        h_scr[...] = _rms(x_ref[...], g_ref[...]).astype(BF16)

    z_ref[...] = _dot(h_scr[...], w_ref[...])


def _inproj(x2d, g, w_bf, tm, tn):
    t = x2d.shape[0]
    return pl.pallas_call(
        _inproj_body,
        grid=(t // tm, MIX_IN // tn),
        in_specs=[
            pl.BlockSpec((tm, D_MODEL), lambda i, j: (i, 0)),
            pl.BlockSpec((1, D_MODEL), lambda i, j: (0, 0)),
            pl.BlockSpec((D_MODEL, tn), lambda i, j: (0, j)),
        ],
        out_specs=pl.BlockSpec((tm, tn), lambda i, j: (i, j)),
        out_shape=jax.ShapeDtypeStruct((t, MIX_IN), F32),
        scratch_shapes=[pltpu.VMEM((tm, D_MODEL), BF16)],
        compiler_params=_cparams(("parallel", "arbitrary")),
        name="inproj",
    )(x2d, g, w_bf)


def _s5_tables(lam_re, lam_im, log_dt, b_re, b_im, c_re, c_im, d_skip):
    hp = lax.Precision.HIGHEST
    f = lambda a: a.astype(F32)
    lam_re, lam_im, b_re, b_im, c_re, c_im = map(f, (lam_re, lam_im, b_re, b_im, c_re, c_im))
    dt = jnp.exp(f(log_dt))[..., None]
    lr, li = lam_re * dt, lam_im * dt
    mag = jnp.exp(lr)
    ar, ai = mag * jnp.cos(li), mag * jnp.sin(li)
    den = lam_re * lam_re + lam_im * lam_im
    fr = ((ar - 1.0) * lam_re + ai * lam_im) / den
    fi = (ai * lam_re - (ar - 1.0) * lam_im) / den
    bbr = fr[..., None] * b_re - fi[..., None] * b_im
    bbi = fr[..., None] * b_im + fi[..., None] * b_re
    n = jnp.arange(S5_CHUNK + 1, dtype=F32)[:, None, None, None]
    pr = jnp.exp(n * lr) * jnp.cos(n * li)
    pi = jnp.exp(n * lr) * jnp.sin(n * li)
    abr = pr[..., None] * bbr - pi[..., None] * bbi
    abi = pr[..., None] * bbi + pi[..., None] * bbr
    kk = (jnp.einsum('dghp,ndgpk->ndghk', c_re, abr, precision=hp)
          - jnp.einsum('dghp,ndgpk->ndghk', c_im, abi, precision=hp))
    tt = np.arange(S5_CHUNK)
    lag = tt[None, :] - tt[:, None]
    kf = kk[np.clip(lag, 0, S5_CHUNK), 0]
    kb = kk[np.clip(-lag, 0, S5_CHUNK), 1]
    mfull = (jnp.where((lag >= 0)[:, :, None, None, None], kf, 0.0)
             + jnp.where((lag <= 0)[:, :, None, None, None], kb, 0.0))
    eye_t = jnp.asarray(np.eye(S5_CHUNK, dtype=np.float32))
    eye_h = jnp.asarray(np.eye(S5_H, dtype=np.float32))
    dsk = f(d_skip).reshape(S5_G, S5_H)
    mfull = mfull + eye_t[:, :, None, None, None] * (dsk[:, :, None] * eye_h[None])[None, None]
    eye_g = jnp.asarray(np.eye(S5_GB, dtype=np.float32))
    m6 = mfull.reshape(S5_CHUNK, S5_CHUNK, S5_JB, S5_GB, S5_H, S5_H)
    m = jnp.einsum('stjghk,gG->jsgktGh', m6, eye_g)
    m = m.reshape(S5_JB, S5_CW, S5_CW)

    nq = S5_GB // 2
    eye_q = jnp.asarray(np.eye(nq, dtype=np.float32))
    eye_2 = jnp.asarray(np.eye(2, dtype=np.float32))
    sidx_f = (S5_CHUNK - 1) - tt
    sidx_b = tt
    sg = jnp.stack([
        jnp.stack([abr[sidx_f, 0], abi[sidx_f, 0]], axis=0),
        jnp.stack([abr[sidx_b, 1], abi[sidx_b, 1]], axis=0),
    ], axis=0)
    sg = sg.reshape(2, 2, S5_CHUNK, S5_JB, nq, 2, S5_P, S5_H)
    s = jnp.einsum('drsjqgph,qQ,gG->jsqghdQrGp', sg, eye_q, eye_2)
    s = s.reshape(S5_JB, S5_CW, 2 * S5_GB * 2 * S5_P)

    def ca(d, e):
        car = c_re[d][None] * pr[e, d][:, :, None, :] - c_im[d][None] * pi[e, d][:, :, None, :]
        cai = c_re[d][None] * pi[e, d][:, :, None, :] + c_im[d][None] * pr[e, d][:, :, None, :]
        return jnp.stack([car, -cai], axis=0)
    og = jnp.stack([ca(0, tt + 1), ca(1, S5_CHUNK - tt)], axis=0)
    og = og.reshape(2, 2, S5_CHUNK, S5_JB, nq, 2, S5_H, S5_P)
    o = jnp.einsum('drtjqghp,qQ,gG->djqrgptQGh', og, eye_q, eye_2)
    o = o.reshape(2, S5_JB, S5_GB * 2 * S5_P, S5_CW)

    a16r = pr[S5_CHUNK].reshape(2, S5_JB, nq, 1, 2 * S5_P)
    a16i = pi[S5_CHUNK].reshape(2, S5_JB, nq, 1, 2 * S5_P)
    lay = lambda a: jnp.broadcast_to(a, (2, S5_JB, nq, 2, 2 * S5_P)).transpose(1, 0, 2, 3, 4).reshape(S5_JB, -1)
    return (m.astype(BF16), s.astype(BF16), o[0].astype(BF16), o[1].astype(BF16), lay(a16r), lay(a16i))


def _chunk_rows(u_ref):
    return jnp.concatenate([u_ref[0, :, t, :] for t in range(S5_CHUNK)], axis=-1).astype(BF16)


def _s5_state_body(u_ref, s_ref, o_ref):
    o_ref[...] = _dot(_chunk_rows(u_ref), s_ref[0])


def _s5_scan_body(sf_ref, sb_ref, are_ref, aim_ref, xf_ref, xb_ref, carry):
    @pl.when(pl.program_id(1) == 0)
    def _():
        carry[...] = jnp.zeros_like(carry)

    rbs = sf_ref.shape[0]
    half = sf_ref.shape[2]
    are, aim = are_ref[...], aim_ref[...]

    def advance(x, s, base):
        parts = []
        for q in range(half // (2 * LANE)):
            lo = q * 2 * LANE
            xr, xi = x[:, lo:lo + LANE], x[:, lo + LANE:lo + 2 * LANE]
            a_r = are[:, base + lo:base + lo + LANE]
            a_i = aim[:, base + lo:base + lo + LANE]
            parts.append(a_r * xr - a_i * xi + s[:, lo:lo + LANE])
            parts.append(a_r * xi + a_i * xr + s[:, lo + LANE:lo + 2 * LANE])
        return jnp.concatenate(parts, axis=-1)

    def step(c, xs):
        xf, xb = xs
        cb = rbs - 1 - c
        xf_ref[c] = xf
        xb_ref[cb] = xb
        return advance(xf, sf_ref[c], 0), advance(xb, sb_ref[cb], half)

    xf, xb = lax.fori_loop(0, rbs, step, (carry[0], carry[1]))
    carry[0] = xf
    carry[1] = xb


def _s5_out_body(u_ref, xf_ref, xb_ref, m_ref, of_ref, ob_ref, y_ref):
    y = (_dot(_chunk_rows(u_ref), m_ref[0])
         + _dot(xf_ref[...].astype(BF16), of_ref[0])
         + _dot(xb_ref[...].astype(BF16), ob_ref[0]))
    y = jax.nn.gelu(y)
    for t in range(S5_CHUNK):
        y_ref[0, :, t, :] = y[:, t * LANE:(t + 1) * LANE]


def _s5_mixer(z, bn, seq, tabs):
    m, s, of, ob, a_re, a_im = tabs
    nc = seq // S5_CHUNK
    rbs = min(nc, 256)
    nrb = nc // rbs
    sw = s.shape[-1]
    half = sw // 2
    z5 = z.reshape(bn, nc, S5_CHUNK, MIX_IN)
    u_spec = pl.BlockSpec((1, rbs, S5_CHUNK, LANE), lambda j, b, r: (b, r, 0, j))
    st = pl.pallas_call(
        _s5_state_body,
        grid=(S5_JB, bn, nrb),
        in_specs=[u_spec, pl.BlockSpec((1, S5_CW, sw), lambda j, b, r: (j, 0, 0))],
        out_specs=pl.BlockSpec((rbs, sw), lambda j, b, r: (r, b * S5_JB + j)),
        out_shape=jax.ShapeDtypeStruct((nc, bn * S5_JB * sw), F32),
        compiler_params=_cparams(("parallel", "parallel", "parallel")),
        name="s5_state",
    )(z5, s)

    nseq = bn * S5_JB
    st3 = st.reshape(nc, nseq, sw)
    a_re_t = jnp.tile(a_re, (bn, 1))
    a_im_t = jnp.tile(a_im, (bn, 1))
    sbs = min(nc, 64)
    nsb = nc // sbs
    xf, xb = pl.pallas_call(
        _s5_scan_body,
        grid=(nseq // 8, nsb),
        in_specs=[
            pl.BlockSpec((sbs, 8, half), lambda q, i: (i, q, 0)),
            pl.BlockSpec((sbs, 8, half), lambda q, i: (nsb - 1 - i, q, 1)),
            pl.BlockSpec((8, sw), lambda q, i: (q, 0)),
            pl.BlockSpec((8, sw), lambda q, i: (q, 0)),
        ],
        out_specs=[
            pl.BlockSpec((sbs, 8, half), lambda q, i: (i, q, 0)),
            pl.BlockSpec((sbs, 8, half), lambda q, i: (nsb - 1 - i, q, 0)),
        ],
        out_shape=[jax.ShapeDtypeStruct((nc, nseq, half), F32)] * 2,
        scratch_shapes=[pltpu.VMEM((2, 8, half), F32)],
        compiler_params=_cparams(("parallel", "arbitrary")),
        name="s5_scan",
    )(st3, st3, a_re_t, a_im_t)

    x_spec = pl.BlockSpec((rbs, half), lambda j, b, r: (r, b * S5_JB + j))
    y = pl.pallas_call(
        _s5_out_body,
        grid=(S5_JB, bn, nrb),
        in_specs=[
            u_spec, x_spec, x_spec,
            pl.BlockSpec((1, S5_CW, S5_CW), lambda j, b, r: (j, 0, 0)),
            pl.BlockSpec((1, half, S5_CW), lambda j, b, r: (j, 0, 0)),
            pl.BlockSpec((1, half, S5_CW), lambda j, b, r: (j, 0, 0)),
        ],
        out_specs=pl.BlockSpec((1, rbs, S5_CHUNK, LANE), lambda j, b, r: (b, r, 0, j)),
        out_shape=jax.ShapeDtypeStruct((bn, nc, S5_CHUNK, BW), F32),
        compiler_params=_cparams(("parallel", "parallel", "parallel")),
        name="s5_out",
    )(z5, xf.reshape(nc, nseq * half), xb.reshape(nc, nseq * half), m, of, ob)
    return y.reshape(bn * seq, BW)


def _conv_body(b_ref, c_ref, v_ref, cp_ref, vp_ref, cn_ref, vn_ref, w_ref, o_ref):
    i = pl.program_id(1)
    tc = c_ref.shape[1]
    z = c_ref[0] * v_ref[0]
    zp = cp_ref[0][7:8, :] * vp_ref[0][7:8, :]
    zn = cn_ref[0][0:1, :] * vn_ref[0][0:1, :]
    zp = jnp.where(i == 0, 0.0, zp)
    zn = jnp.where(i == pl.num_programs(1) - 1, 0.0, zn)
    row = lax.broadcasted_iota(jnp.int32, z.shape, 0)
    up = jnp.where(row == 0, zp, pltpu.roll(z, 1, axis=0))
    dn = jnp.where(row == tc - 1, zn, pltpu.roll(z, tc - 1, axis=0))
    w = w_ref[...]
    y = w[0:1, :] * up + w[1:2, :] * z + w[2:3, :] * dn
    o_ref[0] = (b_ref[0] * y).astype(o_ref.dtype)


def _conv_mixer(z, bn, seq, conv_w, tc):
    z3 = z.reshape(bn, seq, MIX_IN)
    nb8 = tc // 8
    last8 = seq // 8 - 1
    main = lambda col: pl.BlockSpec((1, tc, BW), lambda b, i: (b, i, col))
    prev = lambda col: pl.BlockSpec((1, 8, BW), lambda b, i: (b, jnp.maximum(i * nb8 - 1, 0), col))
    nxt = lambda col: pl.BlockSpec((1, 8, BW), lambda b, i: (b, jnp.minimum((i + 1) * nb8, last8), col))
    y = pl.pallas_call(
        _conv_body,
        grid=(bn, seq // tc),
        in_specs=[main(1), main(2), main(3), prev(2), prev(3), nxt(2), nxt(3),
                  pl.BlockSpec((3, BW), lambda b, i: (0, 0))],
        out_specs=pl.BlockSpec((1, tc, BW), lambda b, i: (b, i, 0)),
        out_shape=jax.ShapeDtypeStruct((bn, seq, BW), BF16),
        compiler_params=_cparams(("parallel", "parallel")),
        name="short_conv",
    )(z3, z3, z3, z3, z3, z3, z3, conv_w.astype(F32))
    return y.reshape(bn * seq, BW)


def _head_mean_sq(x, ones_ref):
    x2 = x * x
    hi = x2.astype(BF16)
    lo = (x2 - hi.astype(F32)).astype(BF16)
    return _dot(hi, ones_ref[...]) + _dot(lo, ones_ref[...])


def _qkv_body(q_ref, k_ref, v_ref, qg_ref, kg_ref, ones_ref, qo_ref, ko_ref, vo_ref):
    q = q_ref[...]
    k = k_ref[...]
    qn = q * lax.rsqrt(_head_mean_sq(q, ones_ref) + EPS) * qg_ref[...]
    kn = k * lax.rsqrt(_head_mean_sq(k, ones_ref) + EPS) * kg_ref[...]
    qo_ref[...] = (qn * (1.0 / math.sqrt(HEAD_DIM))).astype(BF16)
    ko_ref[...] = kn.astype(BF16)
    vo_ref[...] = v_ref[...].astype(BF16)


def _qkv_prep(z, q_gain, k_gain, tm):
    t = z.shape[0]
    ones = np.kron(np.eye(N_HEADS, dtype=np.float32), np.full((HEAD_DIM, HEAD_DIM), 1.0 / HEAD_DIM, np.float32))
    col = lambda c: pl.BlockSpec((tm, BW), lambda i: (i, c))
    vec = pl.BlockSpec((1, BW), lambda i: (0, 0))
    out = pl.BlockSpec((tm, BW), lambda i: (i, 0))
    return pl.pallas_call(
        _qkv_body,
        grid=(t // tm,),
        in_specs=[col(4), col(5), col(6), vec, vec, pl.BlockSpec((BW, BW), lambda i: (0, 0))],
        out_specs=[out, out, out],
        out_shape=[jax.ShapeDtypeStruct((t, BW), BF16)] * 3,
        compiler_params=_cparams(("parallel",)),
        name="qkv_prep",
    )(z, z, z, jnp.tile(q_gain.astype(F32), N_HEADS)[None], jnp.tile(k_gain.astype(F32), N_HEADS)[None],
      jnp.asarray(ones, BF16))


def _na_bias_table(rel_bias):
    ir = np.arange(NA_ROWS)
    off = np.stack([np.maximum(-(NA_ROWS // 2), -ir), np.full(NA_ROWS, -(NA_ROWS // 2)),
                    np.minimum(-(NA_ROWS // 2), -ir)])
    dr = off[:, :, None] + np.arange(NA_ROWS)[None, None, :] + (NA_ROWS - 1)
    qc = np.arange(GRID_W)[:, None]
    kc = np.arange(GRID_W)[None, :]
    cs = np.clip(qc - NA_COLS // 2, 0, GRID_W - NA_COLS)
    valid = (kc >= cs) & (kc < cs + NA_COLS)
    dc = np.clip(kc - qc + (NA_COLS - 1), 0, 2 * NA_COLS - 2)
    b = rel_bias.astype(F32)[:, dr[:, :, :, None, None], dc[None, None, None, :, :]]
    b = jnp.where(valid[None, None, None, None], b, NEG_INF)
    b = b.transpose(1, 2, 0, 4, 3, 5)
    return b.reshape(3, NA_ROWS, N_HEADS // 2, 2 * GRID_W, NA_ROWS * GRID_W)


def _na_body(q_ref, kp_ref, kc_ref, kn_ref, vp_ref, vc_ref, vn_ref, bias_ref, o_ref, k_scr, v_scr, *, rows):
    i = pl.program_id(1)
    blk = NA_ROWS * GRID_W
    for n, (kr, vr) in enumerate(((kp_ref, vp_ref), (kc_ref, vc_ref), (kn_ref, vn_ref))):
        k_scr[n * blk:(n + 1) * blk, :] = kr[0]
        v_scr[n * blk:(n + 1) * blk, :] = vr[0]
    first_head = lax.broadcasted_iota(jnp.int32, (GRID_W, 2 * HEAD_DIM), 1) < HEAD_DIM
    for ir in range(NA_ROWS):
        r = i * NA_ROWS + ir
        rs = jnp.clip(r - NA_ROWS // 2, 0, rows - NA_ROWS)
        off = pl.multiple_of((rs - (i - 1) * NA_ROWS) * GRID_W, GRID_W)
        for hp in range(N_HEADS // 2):
            lanes = slice(hp * 2 * HEAD_DIM, (hp + 1) * 2 * HEAD_DIM)
            q = q_ref[0, ir * GRID_W:(ir + 1) * GRID_W, lanes]
            zero = jnp.zeros_like(q)
            q2 = jnp.concatenate([jnp.where(first_head, q, zero), jnp.where(first_head, zero, q)], axis=0)
            kw = k_scr[pl.ds(off, blk), lanes]
            s = lax.dot_general(q2, kw, (((1,), (1,)), ((), ())), preferred_element_type=F32)
            s = s + bias_ref[0, ir, hp]
            e = jnp.exp(s - jnp.max(s, axis=-1, keepdims=True))
            l = jnp.sum(e, axis=-1, keepdims=True)
            vw = v_scr[pl.ds(off, blk), lanes]
            o = _dot(e.astype(BF16), vw) / l
            o_ref[0, ir * GRID_W:(ir + 1) * GRID_W, lanes] = jnp.where(
                first_head, o[:GRID_W], o[GRID_W:]).astype(o_ref.dtype)


def _na_mixer(qn, kn, vb, bn, seq, bias_tab):
    rows = seq // GRID_W
    nblk = rows // NA_ROWS
    assert rows % NA_ROWS == 0 and nblk >= 2
    blk = NA_ROWS * GRID_W
    q3, k3, v3 = (a.reshape(bn, seq, BW) for a in (qn, kn, vb))
    cur = pl.BlockSpec((1, blk, BW), lambda b, i: (b, i, 0))
    prev = pl.BlockSpec((1, blk, BW), lambda b, i: (b, jnp.maximum(i - 1, 0), 0))
    nxt = pl.BlockSpec((1, blk, BW), lambda b, i: (b, jnp.minimum(i + 1, nblk - 1), 0))
    kind = lambda i: jnp.where(i == 0, 0, jnp.where(i == nblk - 1, 2, 1))
    y = pl.pallas_call(
        functools.partial(_na_body, rows=rows),
        grid=(bn, nblk),
        in_specs=[cur, prev, cur, nxt, prev, cur, nxt,
                  pl.BlockSpec((1, NA_ROWS, N_HEADS // 2, 2 * GRID_W, blk), lambda b, i: (kind(i), 0, 0, 0, 0))],
        out_specs=pl.BlockSpec((1, blk, BW), lambda b, i: (b, i, 0)),
        out_shape=jax.ShapeDtypeStruct((bn, seq, BW), BF16),
        scratch_shapes=[pltpu.VMEM((3 * blk, BW), BF16), pltpu.VMEM((3 * blk, BW), BF16)],
        compiler_params=_cparams(("parallel", "arbitrary")),
        name="nbr_attention",
    )(q3, k3, k3, k3, v3, v3, v3, bias_tab)
    return y.reshape(bn * seq, BW)


def _fnet_factors(seq):
    n1 = 1 << ((seq.bit_length() - 1 + 1) // 2)
    assert seq == n1 * (seq // n1) and seq & (seq - 1) == 0
    return n1, seq // n1


@functools.lru_cache(maxsize=None)
def _fnet_tables(seq):
    n1, n2 = _fnet_factors(seq)

    def cs(n, scale):
        idx = np.outer(np.arange(n), np.arange(n)) % n
        ang = 2.0 * np.pi * idx / n
        return np.cos(ang) * scale, np.sin(ang) * scale

    cc, sc = cs(FNET_GW, FNET_GW ** -0.5)
    chan = np.concatenate([cc, sc], axis=1)
    c1, s1 = cs(n1, n1 ** -0.5)
    d1 = np.block([[c1, -s1], [s1, c1]])
    c2, s2 = cs(n2, n2 ** -0.5)
    d2 = np.concatenate([c2, -s2], axis=1)
    ang = 2.0 * np.pi * (np.outer(np.arange(n2), np.arange(n1)) % seq) / seq
    twr = np.broadcast_to(np.cos(ang)[:, :, None], (n2, n1, LANE))
    twi = np.broadcast_to(np.sin(ang)[:, :, None], (n2, n1, LANE))
    return (np.asarray(chan, np.float32), np.asarray(d1, np.float32), np.asarray(d2, np.float32),
            np.ascontiguousarray(twr, np.float32), np.ascontiguousarray(twi, np.float32))


def _fnet1_body(x_ref, chan_ref, d1_ref, twr_ref, twi_ref, o_ref):
    n1, tb = x_ref.shape[1], x_ref.shape[2]
    ng = BW // FNET_GW
    for t in range(tb):
        x = x_ref[0, :, t, :].astype(BF16)
        pq = [_dot(x[:, g * FNET_GW:(g + 1) * FNET_GW], chan_ref[...]) for g in range(ng)]
        p = jnp.concatenate([a[:, :FNET_GW] for a in pq], axis=1)
        q = jnp.concatenate([a[:, FNET_GW:] for a in pq], axis=1)
        a = _dot(d1_ref[...], jnp.concatenate([p, q], axis=0).astype(BF16))
        ar, ai = a[:n1], a[n1:]
        twr = jnp.concatenate([twr_ref[t]] * ng, axis=1)
        twi = jnp.concatenate([twi_ref[t]] * ng, axis=1)
        o_ref[0, 0, t] = (ar * twr - ai * twi).astype(o_ref.dtype)
        o_ref[0, 1, t] = (ar * twi + ai * twr).astype(o_ref.dtype)


def _fnet2_body(a_ref, d2_ref, o_ref):
    o_ref[0] = _dot(d2_ref[...], a_ref[0]).astype(o_ref.dtype)


def _fnet_mixer(z, bn, seq):
    n1, n2 = _fnet_factors(seq)
    chan, d1, d2, twr, twi = _fnet_tables(seq)
    tb = 8
    zf = z.reshape(bn, n1, n2, MIX_IN)
    a = pl.pallas_call(
        _fnet1_body,
        grid=(bn, n2 // tb),
        in_specs=[
            pl.BlockSpec((1, n1, tb, BW), lambda b, j: (b, 0, j, 7)),
            pl.BlockSpec((FNET_GW, 2 * FNET_GW), lambda b, j: (0, 0)),
            pl.BlockSpec((2 * n1, 2 * n1), lambda b, j: (0, 0)),
            pl.BlockSpec((tb, n1, LANE), lambda b, j: (j, 0, 0)),
            pl.BlockSpec((tb, n1, LANE), lambda b, j: (j, 0, 0)),
        ],
        out_specs=pl.BlockSpec((1, 2, tb, n1, BW), lambda b, j: (b, 0, j, 0, 0)),
        out_shape=jax.ShapeDtypeStruct((bn, 2, n2, n1, BW), BF16),
        compiler_params=_cparams(("parallel", "parallel")),
        name="fnet_stage1",
    )(zf, jnp.asarray(chan, BF16), jnp.asarray(d1, BF16), jnp.asarray(twr), jnp.asarray(twi))

    kb = 8
    y = pl.pallas_call(
        _fnet2_body,
        grid=(bn, n1 // kb),
        in_specs=[
            pl.BlockSpec((1, 2 * n2, kb * BW), lambda b, i: (b, 0, i)),
            pl.BlockSpec((n2, 2 * n2), lambda b, i: (0, 0)),
        ],
        out_specs=pl.BlockSpec((1, n2, kb * BW), lambda b, i: (b, 0, i)),
        out_shape=jax.ShapeDtypeStruct((bn, n2, n1 * BW), BF16),
        compiler_params=_cparams(("parallel", "parallel")),
        name="fnet_stage2",
    )(a.reshape(bn, 2 * n2, n1 * BW), jnp.asarray(d2, BF16))
    return y.reshape(bn * seq, BW)


def _merge_body(x_ref, g_ref, ya_ref, yb_ref, yc_ref, yd_ref, wglu_ref, wg0_ref, wg1_ref, wg2_ref, wg3_ref,
                wbr_ref, wo_ref, o_ref, h_scr, ya_scr):
    @pl.when(pl.program_id(1) == 0)
    def _():
        x = x_ref[...]
        h_scr[...] = _rms(x, g_ref[...]).astype(BF16)
        ya = ya_ref[...]
        ya_scr[...] = (ya * jax.nn.sigmoid(_dot(ya.astype(BF16), wglu_ref[...]))).astype(BF16)
        o_ref[...] = x

    h = h_scr[...]
    ys = (ya_scr[...], yb_ref[...], yc_ref[...], yd_ref[...])
    merged = None
    for kb, wg_ref in enumerate((wg0_ref, wg1_ref, wg2_ref, wg3_ref)):
        term = jax.nn.sigmoid(_dot(h, wg_ref[...])) * _dot(ys[kb], wbr_ref[kb])
        merged = term if merged is None else merged + term
    o_ref[...] += _dot(merged.astype(BF16), wo_ref[...])


def _merge(x2d, g, ya, yb, yc, yd, w_glu, w_in_bf, w_br, w_o, tm, tn):
    t = x2d.shape[0]
    row = lambda w: pl.BlockSpec((tm, w), lambda i, n: (i, 0))
    gate = lambda kb: pl.BlockSpec((D_MODEL, tn), lambda i, n: (0, (MIX_IN + kb * D_MODEL) // tn + n))
    return pl.pallas_call(
        _merge_body,
        grid=(t // tm, D_MODEL // tn),
        in_specs=[
            row(D_MODEL), pl.BlockSpec((1, D_MODEL), lambda i, n: (0, 0)),
            row(BW), row(BW), row(BW), row(BW),
            pl.BlockSpec((BW, BW), lambda i, n: (0, 0)),
            gate(0), gate(1), gate(2), gate(3),
            pl.BlockSpec((N_BRANCH, BW, tn), lambda i, n: (0, 0, n)),
            pl.BlockSpec((tn, D_MODEL), lambda i, n: (n, 0)),
        ],
        out_specs=row(D_MODEL),
        out_shape=jax.ShapeDtypeStruct((t, D_MODEL), F32),
        scratch_shapes=[pltpu.VMEM((tm, D_MODEL), BF16), pltpu.VMEM((tm, BW), BF16)],
        compiler_params=_cparams(("parallel", "arbitrary")),
        name="gated_merge",
    )(x2d, g, ya, yb, yc, yd, w_glu, w_in_bf, w_in_bf, w_in_bf, w_in_bf, w_br, w_o)


def _ffn_body(x_ref, g_ref, wa_ref, wb_ref, wo_ref, o_ref, h_scr):
    @pl.when(pl.program_id(1) == 0)
    def _():
        x = x_ref[...]
        h_scr[...] = _rms(x, g_ref[...]).astype(BF16)
        o_ref[...] = x

    h = h_scr[...]
    a = _dot(h, wa_ref[...])
    b = _dot(h, wb_ref[...])
    o_ref[...] += _dot((jax.nn.silu(a) * b).astype(BF16), wo_ref[...])


def _ffn(x2d, g, w_in, w_out, tm, tf):
    t = x2d.shape[0]
    nf = D_FF // tf
    return pl.pallas_call(
        _ffn_body,
        grid=(t // tm, nf),
        in_specs=[
            pl.BlockSpec((tm, D_MODEL), lambda i, f: (i, 0)),
            pl.BlockSpec((1, D_MODEL), lambda i, f: (0, 0)),
            pl.BlockSpec((D_MODEL, tf), lambda i, f: (0, f)),
            pl.BlockSpec((D_MODEL, tf), lambda i, f: (0, nf + f)),
            pl.BlockSpec((tf, D_MODEL), lambda i, f: (f, 0)),
        ],
        out_specs=pl.BlockSpec((tm, D_MODEL), lambda i, f: (i, 0)),
        out_shape=jax.ShapeDtypeStruct((t, D_MODEL), F32),
        scratch_shapes=[pltpu.VMEM((tm, D_MODEL), BF16)],
        compiler_params=_cparams(("parallel", "arbitrary")),
        name="swiglu_ffn",
    )(x2d, g, w_in, w_in, w_out)


def _ple_body(x_ref, xs_ref, g_ref, p_ref, wg_ref, wp_ref, o_ref, h_scr):
    @pl.when(pl.program_id(1) == 0)
    def _():
        h_scr[...] = _rms(x_ref[...], g_ref[...]).astype(BF16)

    pg = jax.nn.sigmoid(_dot(h_scr[...], wg_ref[...]))
    o_ref[...] = xs_ref[...] + pg * _dot(p_ref[...].astype(BF16), wp_ref[...])


def _ple(x2d, g, p2d, w_gate, w_proj, tm, tn):
    t = x2d.shape[0]
    return pl.pallas_call(
        _ple_body,
        grid=(t // tm, D_MODEL // tn),
        in_specs=[
            pl.BlockSpec((tm, D_MODEL), lambda i, n: (i, 0)),
            pl.BlockSpec((tm, tn), lambda i, n: (i, n)),
            pl.BlockSpec((1, D_MODEL), lambda i, n: (0, 0)),
            pl.BlockSpec((tm, PLE_DIM), lambda i, n: (i, 0)),
            pl.BlockSpec((D_MODEL, tn), lambda i, n: (0, n)),
            pl.BlockSpec((PLE_DIM, tn), lambda i, n: (0, n)),
        ],
        out_specs=pl.BlockSpec((tm, tn), lambda i, n: (i, n)),
        out_shape=jax.ShapeDtypeStruct((t, D_MODEL), F32),
        scratch_shapes=[pltpu.VMEM((tm, D_MODEL), BF16)],
        compiler_params=_cparams(("parallel", "arbitrary")),
        name="ple_gate",
    )(x2d, x2d, g, p2d, w_gate, w_proj)


def _tile(t, want):
    return min(t, want)


def _layer(x2d, p2d, bn, seq, lw):
    t = bn * seq
    z = _inproj(x2d, lw['g_mix'], lw['w_in'], _tile(t, 512), 1024)
    ya = _s5_mixer(z, bn, seq, lw['s5'])
    yb = _conv_mixer(z, bn, seq, lw['conv_w'], _tile(seq, 1024))
    qn, kn, vb = _qkv_prep(z, lw['q_gain'], lw['k_gain'], _tile(t, 1024))
    yc = _na_mixer(qn, kn, vb, bn, seq, lw['na_bias'])
    yd = _fnet_mixer(z, bn, seq)
    x2d = _merge(x2d, lw['g_mix'], ya, yb, yc, yd, lw['w_glu'], lw['w_in'], lw['w_br'], lw['w_o'],
                 _tile(t, 512), 512)
    x2d = _ffn(x2d, lw['g_ffn'], lw['w_ffn_in'], lw['w_ffn_out'], _tile(t, 512), 512)
    return _ple(x2d, lw['g_ple'], p2d, lw['w_ple_gate'], lw['w_ple_proj'], _tile(t, 1024), 512)


def kernel(x_prompt, x_sample, p_prompt, p_sample, g_mix, w_in, s5_lam_re, s5_lam_im, s5_log_dt, s5_b_re,
           s5_b_im, s5_c_re, s5_c_im, s5_d, w_glu, conv_w, q_gain, k_gain, rel_bias, w_br, w_o, g_ffn,
           w_ffn_in, w_ffn_out, g_ple, w_ple_gate, w_ple_proj):
    depth = w_in.shape[0]
    layers = []
    for i in range(depth):
        layers.append(dict(
            g_mix=g_mix[i].astype(F32)[None], w_in=w_in[i].astype(BF16),
            s5=_s5_tables(s5_lam_re[i], s5_lam_im[i], s5_log_dt[i], s5_b_re[i], s5_b_im[i], s5_c_re[i],
                          s5_c_im[i], s5_d[i]),
            w_glu=w_glu[i].astype(BF16), conv_w=conv_w[i], q_gain=q_gain[i], k_gain=k_gain[i],
            na_bias=_na_bias_table(rel_bias[i]), w_br=w_br[i].astype(BF16), w_o=w_o[i].astype(BF16),
            g_ffn=g_ffn[i].astype(F32)[None], w_ffn_in=w_ffn_in[i].astype(BF16),
            w_ffn_out=w_ffn_out[i].astype(BF16), g_ple=g_ple[i].astype(F32)[None],
            w_ple_gate=w_ple_gate[i].astype(BF16), w_ple_proj=w_ple_proj[i].astype(BF16)))

    def trunk(x, p):
        bn, seq, _ = x.shape
        x2d = x.reshape(bn * seq, D_MODEL)
        for i in range(depth):
            x2d = _layer(x2d, p[i].reshape(bn * seq, PLE_DIM), bn, seq, layers[i])
        return x2d.reshape(bn, seq, D_MODEL)

    return trunk(x_prompt, p_prompt), trunk(x_sample, p_sample)
```

```python
import functools
import math

import numpy as np
import jax
import jax.numpy as jnp
from jax import lax
from jax.experimental import pallas as pl
from jax.experimental.pallas import tpu as pltpu

F32 = jnp.float32
BF16 = jnp.bfloat16

D_MODEL = 2048
BW = 512
MIX_IN = 8 * BW
N_BRANCH = 4
S5_H = 16
S5_G = BW // S5_H
S5_P = 64
S5_CHUNK = 16
LANE = 128
S5_JB = BW // LANE
S5_GB = LANE // S5_H
S5_CW = S5_CHUNK * LANE
SCAN_ROWS = 8
SCAN_STEPS = (1, 2, 4)
SCAN_ROWPOW = len(SCAN_STEPS)
SCAN_TILEPOW = SCAN_ROWPOW + 1
N_HEADS = 8
HEAD_DIM = 64
GRID_W = 64
NA_ROWS = 8
NA_COLS = 16
FNET_GW = 128
D_FF = 5632
PLE_DIM = 256
EPS = 1e-6
NEG_INF = -1e30
VMEM_LIMIT = 56 * 1024 * 1024


def _cparams(sem):
    return pltpu.CompilerParams(dimension_semantics=sem, vmem_limit_bytes=VMEM_LIMIT)


def _rms(x, g):
    ms = jnp.mean(x * x, axis=-1, keepdims=True)
    return x * lax.rsqrt(ms + EPS) * g


def _dot(a, b):
    return jnp.dot(a, b, preferred_element_type=F32)


def _inproj_body(x_ref, g_ref, w_ref, z_ref, h_scr):
    @pl.when(pl.program_id(1) == 0)
    def _():
        h_scr[...] = _rms(x_ref[...], g_ref[...]).astype(BF16)

    z_ref[...] = _dot(h_scr[...], w_ref[...])


def _inproj(x2d, g, w_bf, tm, tn):
    t = x2d.shape[0]
    return pl.pallas_call(
        _inproj_body,
        grid=(t // tm, MIX_IN // tn),
        in_specs=[
            pl.BlockSpec((tm, D_MODEL), lambda i, j: (i, 0)),
            pl.BlockSpec((1, D_MODEL), lambda i, j: (0, 0)),
            pl.BlockSpec((D_MODEL, tn), lambda i, j: (0, j)),
        ],
        out_specs=pl.BlockSpec((tm, tn), lambda i, j: (i, j)),
        out_shape=jax.ShapeDtypeStruct((t, MIX_IN), F32),
        scratch_shapes=[pltpu.VMEM((tm, D_MODEL), BF16)],
        compiler_params=_cparams(("parallel", "arbitrary")),
        name="inproj",
    )(x2d, g, w_bf)


def _s5_tables(lam_re, lam_im, log_dt, b_re, b_im, c_re, c_im, d_skip):
    hp = lax.Precision.HIGHEST
    f = lambda a: a.astype(F32)
    lam_re, lam_im, b_re, b_im, c_re, c_im = map(f, (lam_re, lam_im, b_re, b_im, c_re, c_im))
    dt = jnp.exp(f(log_dt))[..., None]
    lr, li = lam_re * dt, lam_im * dt
    mag = jnp.exp(lr)
    ar, ai = mag * jnp.cos(li), mag * jnp.sin(li)
    den = lam_re * lam_re + lam_im * lam_im
    fr = ((ar - 1.0) * lam_re + ai * lam_im) / den
    fi = (ai * lam_re - (ar - 1.0) * lam_im) / den
    bbr = fr[..., None] * b_re - fi[..., None] * b_im
    bbi = fr[..., None] * b_im + fi[..., None] * b_re
    n = jnp.arange(S5_CHUNK + 1, dtype=F32)[:, None, None, None]
    pr = jnp.exp(n * lr) * jnp.cos(n * li)
    pi = jnp.exp(n * lr) * jnp.sin(n * li)
    abr = pr[..., None] * bbr - pi[..., None] * bbi
    abi = pr[..., None] * bbi + pi[..., None] * bbr
    kk = (jnp.einsum('dghp,ndgpk->ndghk', c_re, abr, precision=hp)
          - jnp.einsum('dghp,ndgpk->ndghk', c_im, abi, precision=hp))
    tt = np.arange(S5_CHUNK)
    lag = tt[None, :] - tt[:, None]
    kf = kk[np.clip(lag, 0, S5_CHUNK), 0]
    kb = kk[np.clip(-lag, 0, S5_CHUNK), 1]
    mfull = (jnp.where((lag >= 0)[:, :, None, None, None], kf, 0.0)
             + jnp.where((lag <= 0)[:, :, None, None, None], kb, 0.0))
    eye_t = jnp.asarray(np.eye(S5_CHUNK, dtype=np.float32))
    eye_h = jnp.asarray(np.eye(S5_H, dtype=np.float32))
    dsk = f(d_skip).reshape(S5_G, S5_H)
    mfull = mfull + eye_t[:, :, None, None, None] * (dsk[:, :, None] * eye_h[None])[None, None]
    eye_g = jnp.asarray(np.eye(S5_GB, dtype=np.float32))
    m6 = mfull.reshape(S5_CHUNK, S5_CHUNK, S5_JB, S5_GB, S5_H, S5_H)
    m = jnp.einsum('stjghk,gG->jsgktGh', m6, eye_g)
    m = m.reshape(S5_JB, S5_CW, S5_CW)

    nq = S5_GB // 2
    eye_q = jnp.asarray(np.eye(nq, dtype=np.float32))
    eye_2 = jnp.asarray(np.eye(2, dtype=np.float32))
    sidx_f = (S5_CHUNK - 1) - tt
    sidx_b = tt
    sg = jnp.stack([
        jnp.stack([abr[sidx_f, 0], abi[sidx_f, 0]], axis=0),
        jnp.stack([abr[sidx_b, 1], abi[sidx_b, 1]], axis=0),
    ], axis=0)
    sg = sg.reshape(2, 2, S5_CHUNK, S5_JB, nq, 2, S5_P, S5_H)
    s = jnp.einsum('drsjqgph,qQ,gG->jsqghdQrGp', sg, eye_q, eye_2)
    s = s.reshape(S5_JB, S5_CW, 2 * S5_GB * 2 * S5_P)

    def ca(d, e):
        car = c_re[d][None] * pr[e, d][:, :, None, :] - c_im[d][None] * pi[e, d][:, :, None, :]
        cai = c_re[d][None] * pi[e, d][:, :, None, :] + c_im[d][None] * pr[e, d][:, :, None, :]
        return jnp.stack([car, -cai], axis=0)
    og = jnp.stack([ca(0, tt + 1), ca(1, S5_CHUNK - tt)], axis=0)
    og = og.reshape(2, 2, S5_CHUNK, S5_JB, nq, 2, S5_H, S5_P)
    o = jnp.einsum('drtjqghp,qQ,gG->djqrgptQGh', og, eye_q, eye_2)
    o = o.reshape(2, S5_JB, S5_GB * 2 * S5_P, S5_CW)

    nn = jnp.arange(SCAN_ROWS + 1, dtype=F32)[:, None, None, None] * S5_CHUNK
    cw = nq * 2 * S5_P
    pw = jnp.stack([(jnp.exp(nn * lr) * jnp.cos(nn * li)).reshape(SCAN_ROWS + 1, 2, S5_JB, cw),
                    (jnp.exp(nn * lr) * jnp.sin(nn * li)).reshape(SCAN_ROWS + 1, 2, S5_JB, cw)], axis=2)
    rows = np.arange(SCAN_ROWS)
    bcast = lambda a: jnp.broadcast_to(a[:, :, None], (2, 2, SCAN_ROWS, S5_JB, cw))
    rowpow = jnp.stack([pw[rows, 0], pw[SCAN_ROWS - 1 - rows, 1]], axis=0).transpose(0, 2, 1, 3, 4)
    tab = jnp.stack([bcast(pw[k]) for k in SCAN_STEPS] + [rowpow, bcast(pw[SCAN_ROWS])], axis=0)
    tab = tab.transpose(4, 1, 0, 2, 3, 5)
    return (m.astype(BF16), s.astype(BF16), o[0].astype(BF16), o[1].astype(BF16), tab)


def _chunk_rows(u_ref):
    return jnp.concatenate([u_ref[0, :, t, :] for t in range(S5_CHUNK)], axis=-1).astype(BF16)


def _s5_state_body(u_ref, s_ref, o_ref):
    o_ref[...] = _dot(_chunk_rows(u_ref), s_ref[0])


def _s5_scan_body(sf_ref, sb_ref, tab_ref, xf_ref, xb_ref, carry):
    @pl.when(pl.program_id(1) == 0)
    def _():
        carry[...] = jnp.zeros_like(carry)

    ntile = sf_ref.shape[0] // SCAN_ROWS
    nq = sf_ref.shape[1] // (2 * LANE)
    row = lax.broadcasted_iota(jnp.int32, (SCAN_ROWS, LANE), 0)

    def shift(v, k, d):
        if d == 0:
            return jnp.where(row >= k, pltpu.roll(v, k, axis=0), 0.0)
        return jnp.where(row < SCAN_ROWS - k, pltpu.roll(v, SCAN_ROWS - k, axis=0), 0.0)

    def cmul(d, kind, q, xr, xi):
        a_r = tab_ref[0, d, kind, 0, :, q * LANE:(q + 1) * LANE]
        a_i = tab_ref[0, d, kind, 1, :, q * LANE:(q + 1) * LANE]
        return a_r * xr - a_i * xi, a_r * xi + a_i * xr

    def tile(d, s_ref, x_ref, r0, cr):
        last = SCAN_ROWS - 1 if d == 0 else 0
        new = []
        for q in range(nq):
            re = slice(q * 2 * LANE, q * 2 * LANE + LANE)
            im = slice(q * 2 * LANE + LANE, (q + 1) * 2 * LANE)
            yr, yi = s_ref[pl.ds(r0, SCAN_ROWS), re], s_ref[pl.ds(r0, SCAN_ROWS), im]
            for kind, k in enumerate(SCAN_STEPS):
                tr, ti = cmul(d, kind, q, shift(yr, k, d), shift(yi, k, d))
                yr, yi = yr + tr, yi + ti
            c_r, c_i = cr[2 * q], cr[2 * q + 1]
            er, ei = cmul(d, SCAN_ROWPOW, q, c_r, c_i)
            x_ref[pl.ds(r0, SCAN_ROWS), re] = shift(yr, 1, d) + er
            x_ref[pl.ds(r0, SCAN_ROWS), im] = shift(yi, 1, d) + ei
            nr, ni = cmul(d, SCAN_TILEPOW, q, c_r, c_i)
            new.append(jnp.broadcast_to(yr[last:last + 1], (SCAN_ROWS, LANE)) + nr)
            new.append(jnp.broadcast_to(yi[last:last + 1], (SCAN_ROWS, LANE)) + ni)
        return tuple(new)

    def step(t, cs):
        cf, cb = cs
        rf = pl.multiple_of(t * SCAN_ROWS, SCAN_ROWS)
        rb = pl.multiple_of((ntile - 1 - t) * SCAN_ROWS, SCAN_ROWS)
        return tile(0, sf_ref, xf_ref, rf, cf), tile(1, sb_ref, xb_ref, rb, cb)

    init = tuple(tuple(carry[d, :, t * LANE:(t + 1) * LANE] for t in range(2 * nq)) for d in range(2))
    cf, cb = lax.fori_loop(0, ntile, step, init)
    for d, cs in enumerate((cf, cb)):
        for t in range(2 * nq):
            carry[d, :, t * LANE:(t + 1) * LANE] = cs[t]


def _s5_out_body(u_ref, xf_ref, xb_ref, m_ref, of_ref, ob_ref, y_ref):
    y = (_dot(_chunk_rows(u_ref), m_ref[0])
         + _dot(xf_ref[...].astype(BF16), of_ref[0])
         + _dot(xb_ref[...].astype(BF16), ob_ref[0]))
    y = jax.nn.gelu(y)
    for t in range(S5_CHUNK):
        y_ref[0, :, t, :] = y[:, t * LANE:(t + 1) * LANE]


def _s5_mixer(z, bn, seq, tabs):
    m, s, of, ob, scan_tab = tabs
    nc = seq // S5_CHUNK
    rbs = min(nc, 256)
    nrb = nc // rbs
    sw = s.shape[-1]
    half = sw // 2
    z5 = z.reshape(bn, nc, S5_CHUNK, MIX_IN)
    u_spec = pl.BlockSpec((1, rbs, S5_CHUNK, LANE), lambda j, b, r: (b, r, 0, j))
    st = pl.pallas_call(
        _s5_state_body,
        grid=(S5_JB, bn, nrb),
        in_specs=[u_spec, pl.BlockSpec((1, S5_CW, sw), lambda j, b, r: (j, 0, 0))],
        out_specs=pl.BlockSpec((rbs, sw), lambda j, b, r: (r, b * S5_JB + j)),
        out_shape=jax.ShapeDtypeStruct((nc, bn * S5_JB * sw), F32),
        compiler_params=_cparams(("parallel", "parallel", "parallel")),
        name="s5_state",
    )(z5, s)

    nseq = bn * S5_JB
    sbs = min(nc, 256)
    nsb = nc // sbs
    xf, xb = pl.pallas_call(
        _s5_scan_body,
        grid=(nseq, nsb),
        in_specs=[
            pl.BlockSpec((sbs, half), lambda q, i: (i, 2 * q)),
            pl.BlockSpec((sbs, half), lambda q, i: (nsb - 1 - i, 2 * q + 1)),
            pl.BlockSpec((1,) + scan_tab.shape[1:], lambda q, i: (q % S5_JB, 0, 0, 0, 0, 0)),
        ],
        out_specs=[
            pl.BlockSpec((sbs, half), lambda q, i: (i, q)),
            pl.BlockSpec((sbs, half), lambda q, i: (nsb - 1 - i, q)),
        ],
        out_shape=[jax.ShapeDtypeStruct((nc, nseq * half), F32)] * 2,
        scratch_shapes=[pltpu.VMEM((2, SCAN_ROWS, half), F32)],
        compiler_params=_cparams(("parallel", "arbitrary")),
        name="s5_scan",
    )(st, st, scan_tab)

    x_spec = pl.BlockSpec((rbs, half), lambda j, b, r: (r, b * S5_JB + j))
    y = pl.pallas_call(
        _s5_out_body,
        grid=(S5_JB, bn, nrb),
        in_specs=[
            u_spec, x_spec, x_spec,
            pl.BlockSpec((1, S5_CW, S5_CW), lambda j, b, r: (j, 0, 0)),
            pl.BlockSpec((1, half, S5_CW), lambda j, b, r: (j, 0, 0)),
            pl.BlockSpec((1, half, S5_CW), lambda j, b, r: (j, 0, 0)),
        ],
        out_specs=pl.BlockSpec((1, rbs, S5_CHUNK, LANE), lambda j, b, r: (b, r, 0, j)),
        out_shape=jax.ShapeDtypeStruct((bn, nc, S5_CHUNK, BW), F32),
        compiler_params=_cparams(("parallel", "parallel", "parallel")),
        name="s5_out",
    )(z5, xf, xb, m, of, ob)
    return y.reshape(bn * seq, BW)


def _conv_body(b_ref, c_ref, v_ref, cp_ref, vp_ref, cn_ref, vn_ref, w_ref, o_ref):
    i = pl.program_id(1)
    tc = c_ref.shape[1]
    z = c_ref[0] * v_ref[0]
    zp = cp_ref[0][7:8, :] * vp_ref[0][7:8, :]
    zn = cn_ref[0][0:1, :] * vn_ref[0][0:1, :]
    zp = jnp.where(i == 0, 0.0, zp)
    zn = jnp.where(i == pl.num_programs(1) - 1, 0.0, zn)
    row = lax.broadcasted_iota(jnp.int32, z.shape, 0)
    up = jnp.where(row == 0, zp, pltpu.roll(z, 1, axis=0))
    dn = jnp.where(row == tc - 1, zn, pltpu.roll(z, tc - 1, axis=0))
    w = w_ref[...]
    y = w[0:1, :] * up + w[1:2, :] * z + w[2:3, :] * dn
    o_ref[0] = (b_ref[0] * y).astype(o_ref.dtype)


def _conv_mixer(z, bn, seq, conv_w, tc):
    z3 = z.reshape(bn, seq, MIX_IN)
    nb8 = tc // 8
    last8 = seq // 8 - 1
    main = lambda col: pl.BlockSpec((1, tc, BW), lambda b, i: (b, i, col))
    prev = lambda col: pl.BlockSpec((1, 8, BW), lambda b, i: (b, jnp.maximum(i * nb8 - 1, 0), col))
    nxt = lambda col: pl.BlockSpec((1, 8, BW), lambda b, i: (b, jnp.minimum((i + 1) * nb8, last8), col))
    y = pl.pallas_call(
        _conv_body,
        grid=(bn, seq // tc),
        in_specs=[main(1), main(2), main(3), prev(2), prev(3), nxt(2), nxt(3),
                  pl.BlockSpec((3, BW), lambda b, i: (0, 0))],
        out_specs=pl.BlockSpec((1, tc, BW), lambda b, i: (b, i, 0)),
        out_shape=jax.ShapeDtypeStruct((bn, seq, BW), BF16),
        compiler_params=_cparams(("parallel", "parallel")),
        name="short_conv",
    )(z3, z3, z3, z3, z3, z3, z3, conv_w.astype(F32))
    return y.reshape(bn * seq, BW)


def _head_mean_sq(x, ones_ref):
    x2 = x * x
    hi = x2.astype(BF16)
    lo = (x2 - hi.astype(F32)).astype(BF16)
    return _dot(hi, ones_ref[...]) + _dot(lo, ones_ref[...])


def _qkv_body(q_ref, k_ref, v_ref, qg_ref, kg_ref, ones_ref, qo_ref, ko_ref, vo_ref):
    q = q_ref[...]
    k = k_ref[...]
    qn = q * lax.rsqrt(_head_mean_sq(q, ones_ref) + EPS) * qg_ref[...]
    kn = k * lax.rsqrt(_head_mean_sq(k, ones_ref) + EPS) * kg_ref[...]
    qo_ref[...] = (qn * (1.0 / math.sqrt(HEAD_DIM))).astype(BF16)
    ko_ref[...] = kn.astype(BF16)
    vo_ref[...] = v_ref[...].astype(BF16)


def _qkv_prep(z, q_gain, k_gain, tm):
    t = z.shape[0]
    ones = np.kron(np.eye(N_HEADS, dtype=np.float32), np.full((HEAD_DIM, HEAD_DIM), 1.0 / HEAD_DIM, np.float32))
    col = lambda c: pl.BlockSpec((tm, BW), lambda i: (i, c))
    vec = pl.BlockSpec((1, BW), lambda i: (0, 0))
    out = pl.BlockSpec((tm, BW), lambda i: (i, 0))
    return pl.pallas_call(
        _qkv_body,
        grid=(t // tm,),
        in_specs=[col(4), col(5), col(6), vec, vec, pl.BlockSpec((BW, BW), lambda i: (0, 0))],
        out_specs=[out, out, out],
        out_shape=[jax.ShapeDtypeStruct((t, BW), BF16)] * 3,
        compiler_params=_cparams(("parallel",)),
        name="qkv_prep",
    )(z, z, z, jnp.tile(q_gain.astype(F32), N_HEADS)[None], jnp.tile(k_gain.astype(F32), N_HEADS)[None],
      jnp.asarray(ones, BF16))


def _na_bias_table(rel_bias):
    qc = np.arange(GRID_W)[:, None]
    kc = np.arange(GRID_W)[None, :]
    cs = np.clip(qc - NA_COLS // 2, 0, GRID_W - NA_COLS)
    valid = (kc >= cs) & (kc < cs + NA_COLS)
    dc = np.clip(kc - qc + (NA_COLS - 1), 0, 2 * NA_COLS - 2)
    pick = (dc[None] == np.arange(2 * NA_COLS - 1)[:, None, None]).astype(np.float32)
    b = jnp.einsum('hrd,dqk->hrqk', rel_bias.astype(F32), jnp.asarray(pick), precision=lax.Precision.HIGHEST)
    b = jnp.where(valid[None, None], b, NEG_INF)
    tab = jnp.stack([b[:, s:s + NA_ROWS] for s in range(NA_ROWS)], axis=0)
    tab = tab.transpose(0, 1, 3, 2, 4)
    return tab.reshape(NA_ROWS, N_HEADS // 2, 2 * GRID_W, NA_ROWS * GRID_W)


def _na_body(q_ref, kp_ref, kc_ref, kn_ref, vp_ref, vc_ref, vn_ref, bias_ref, o_ref, k_scr, v_scr, *, rows):
    i = pl.program_id(1)
    blk = NA_ROWS * GRID_W
    for n, (kr, vr) in enumerate(((kp_ref, vp_ref), (kc_ref, vc_ref), (kn_ref, vn_ref))):
        k_scr[n * blk:(n + 1) * blk, :] = kr[0]
        v_scr[n * blk:(n + 1) * blk, :] = vr[0]
    first_head = lax.broadcasted_iota(jnp.int32, (GRID_W, 2 * HEAD_DIM), 1) < HEAD_DIM
    for ir in range(NA_ROWS):
        r = i * NA_ROWS + ir
        rs = jnp.clip(r - NA_ROWS // 2, 0, rows - NA_ROWS)
        off = pl.multiple_of((rs - (i - 1) * NA_ROWS) * GRID_W, GRID_W)
        bias_row = rs - r + (NA_ROWS - 1)
        for hp in range(N_HEADS // 2):
            lanes = slice(hp * 2 * HEAD_DIM, (hp + 1) * 2 * HEAD_DIM)
            q = q_ref[0, ir * GRID_W:(ir + 1) * GRID_W, lanes]
            zero = jnp.zeros_like(q)
            q2 = jnp.concatenate([jnp.where(first_head, q, zero), jnp.where(first_head, zero, q)], axis=0)
            kw = k_scr[pl.ds(off, blk), lanes]
            s = lax.dot_general(q2, kw, (((1,), (1,)), ((), ())), preferred_element_type=F32)
            s = s + bias_ref[bias_row, hp]
            e = jnp.exp(s - jnp.max(s, axis=-1, keepdims=True))
            l = jnp.sum(e, axis=-1, keepdims=True)
            vw = v_scr[pl.ds(off, blk), lanes]
            o = _dot(e.astype(BF16), vw) / l
            o_ref[0, ir * GRID_W:(ir + 1) * GRID_W, lanes] = jnp.where(
                first_head, o[:GRID_W], o[GRID_W:]).astype(o_ref.dtype)


def _na_mixer(qn, kn, vb, bn, seq, bias_tab):
    rows = seq // GRID_W
    nblk = rows // NA_ROWS
    assert rows % NA_ROWS == 0 and nblk >= 2
    blk = NA_ROWS * GRID_W
    q3, k3, v3 = (a.reshape(bn, seq, BW) for a in (qn, kn, vb))
    cur = pl.BlockSpec((1, blk, BW), lambda b, i: (b, i, 0))
    prev = pl.BlockSpec((1, blk, BW), lambda b, i: (b, jnp.maximum(i - 1, 0), 0))
    nxt = pl.BlockSpec((1, blk, BW), lambda b, i: (b, jnp.minimum(i + 1, nblk - 1), 0))
    y = pl.pallas_call(
        functools.partial(_na_body, rows=rows),
        grid=(bn, nblk),
        in_specs=[cur, prev, cur, nxt, prev, cur, nxt,
                  pl.BlockSpec((NA_ROWS, N_HEADS // 2, 2 * GRID_W, blk), lambda b, i: (0, 0, 0, 0))],
        out_specs=pl.BlockSpec((1, blk, BW), lambda b, i: (b, i, 0)),
        out_shape=jax.ShapeDtypeStruct((bn, seq, BW), BF16),
        scratch_shapes=[pltpu.VMEM((3 * blk, BW), BF16), pltpu.VMEM((3 * blk, BW), BF16)],
        compiler_params=_cparams(("parallel", "arbitrary")),
        name="nbr_attention",
    )(q3, k3, k3, k3, v3, v3, v3, bias_tab)
    return y.reshape(bn * seq, BW)


def _fnet_factors(seq):
    n1 = 1 << ((seq.bit_length() - 1 + 1) // 2)
    assert seq == n1 * (seq // n1) and seq & (seq - 1) == 0
    return n1, seq // n1


@functools.lru_cache(maxsize=None)
def _fnet_tables(seq):
    n1, n2 = _fnet_factors(seq)

    def cs(n, scale):
        idx = np.outer(np.arange(n), np.arange(n)) % n
        ang = 2.0 * np.pi * idx / n
        return np.cos(ang) * scale, np.sin(ang) * scale

    cc, sc = cs(FNET_GW, FNET_GW ** -0.5)
    chan = np.concatenate([cc, sc], axis=1)
    c1, s1 = cs(n1, n1 ** -0.5)
    d1 = np.block([[c1, -s1], [s1, c1]])
    c2, s2 = cs(n2, n2 ** -0.5)
    d2 = np.concatenate([c2, -s2], axis=1)
    ang = 2.0 * np.pi * (np.outer(np.arange(n2), np.arange(n1)) % seq) / seq
    twr = np.broadcast_to(np.cos(ang)[:, :, None], (n2, n1, LANE))
    twi = np.broadcast_to(np.sin(ang)[:, :, None], (n2, n1, LANE))
    return (np.asarray(chan, np.float32), np.asarray(d1, np.float32), np.asarray(d2, np.float32),
            np.ascontiguousarray(twr, np.float32), np.ascontiguousarray(twi, np.float32))


def _fnet1_body(x_ref, chan_ref, d1_ref, twr_ref, twi_ref, o_ref):
    n1, tb = x_ref.shape[1], x_ref.shape[2]
    ng = BW // FNET_GW
    for t in range(tb):
        x = x_ref[0, :, t, :].astype(BF16)
        pq = [_dot(x[:, g * FNET_GW:(g + 1) * FNET_GW], chan_ref[...]) for g in range(ng)]
        p = jnp.concatenate([a[:, :FNET_GW] for a in pq], axis=1)
        q = jnp.concatenate([a[:, FNET_GW:] for a in pq], axis=1)
        a = _dot(d1_ref[...], jnp.concatenate([p, q], axis=0).astype(BF16))
        ar, ai = a[:n1], a[n1:]
        twr = jnp.concatenate([twr_ref[t]] * ng, axis=1)
        twi = jnp.concatenate([twi_ref[t]] * ng, axis=1)
        o_ref[0, 0, :, t, :] = ar * twr - ai * twi
        o_ref[0, 1, :, t, :] = ar * twi + ai * twr


def _fnet2_body(a_ref, d2_ref, o_ref):
    kb, n2 = a_ref.shape[2], a_ref.shape[3]
    for k in range(kb):
        a = a_ref[0, :, k].reshape(2 * n2, BW).astype(BF16)
        o_ref[0, :, k, :] = _dot(d2_ref[...], a)


def _fnet_mixer(z, bn, seq):
    n1, n2 = _fnet_factors(seq)
    chan, d1, d2, twr, twi = _fnet_tables(seq)
    tb = 8
    zf = z.reshape(bn, n1, n2, MIX_IN)
    a = pl.pallas_call(
        _fnet1_body,
        grid=(bn, n2 // tb),
        in_specs=[
            pl.BlockSpec((1, n1, tb, BW), lambda b, j: (b, 0, j, 7)),
            pl.BlockSpec((FNET_GW, 2 * FNET_GW), lambda b, j: (0, 0)),
            pl.BlockSpec((2 * n1, 2 * n1), lambda b, j: (0, 0)),
            pl.BlockSpec((tb, n1, LANE), lambda b, j: (j, 0, 0)),
            pl.BlockSpec((tb, n1, LANE), lambda b, j: (j, 0, 0)),
        ],
        out_specs=pl.BlockSpec((1, 2, n1, tb, BW), lambda b, j: (b, 0, 0, j, 0)),
        out_shape=jax.ShapeDtypeStruct((bn, 2, n1, n2, BW), F32),
        compiler_params=_cparams(("parallel", "parallel")),
        name="fnet_stage1",
    )(zf, jnp.asarray(chan, BF16), jnp.asarray(d1, BF16), jnp.asarray(twr), jnp.asarray(twi))

    kb = 8
    y = pl.pallas_call(
        _fnet2_body,
        grid=(bn, n1 // kb),
        in_specs=[
            pl.BlockSpec((1, 2, kb, n2, BW), lambda b, i: (b, 0, i, 0, 0)),
            pl.BlockSpec((n2, 2 * n2), lambda b, i: (0, 0)),
        ],
        out_specs=pl.BlockSpec((1, n2, kb, BW), lambda b, i: (b, 0, i, 0)),
        out_shape=jax.ShapeDtypeStruct((bn, n2, n1, BW), F32),
        compiler_params=_cparams(("parallel", "parallel")),
        name="fnet_stage2",
    )(a, jnp.asarray(d2, BF16))
    return y.reshape(bn * seq, BW)


def _merge_body(x_ref, g_ref, ya_ref, yb_ref, yc_ref, yd_ref, wglu_ref, wg0_ref, wg1_ref, wg2_ref, wg3_ref,
                wbr_ref, wo_ref, o_ref, h_scr, ya_scr):
    @pl.when(pl.program_id(1) == 0)
    def _():
        x = x_ref[...]
        h_scr[...] = _rms(x, g_ref[...]).astype(BF16)
        ya = ya_ref[...]
        ya_scr[...] = (ya * jax.nn.sigmoid(_dot(ya.astype(BF16), wglu_ref[...]))).astype(BF16)
        o_ref[...] = x

    h = h_scr[...]
    ys = (ya_scr[...], yb_ref[...], yc_ref[...], yd_ref[...].astype(BF16))
    merged = None
    for kb, wg_ref in enumerate((wg0_ref, wg1_ref, wg2_ref, wg3_ref)):
        term = jax.nn.sigmoid(_dot(h, wg_ref[...])) * _dot(ys[kb], wbr_ref[kb])
        merged = term if merged is None else merged + term
    o_ref[...] += _dot(merged.astype(BF16), wo_ref[...])


def _merge(x2d, g, ya, yb, yc, yd, w_glu, w_in_bf, w_br, w_o, tm, tn):
    t = x2d.shape[0]
    row = lambda w: pl.BlockSpec((tm, w), lambda i, n: (i, 0))
    gate = lambda kb: pl.BlockSpec((D_MODEL, tn), lambda i, n: (0, (MIX_IN + kb * D_MODEL) // tn + n))
    return pl.pallas_call(
        _merge_body,
        grid=(t // tm, D_MODEL // tn),
        in_specs=[
            row(D_MODEL), pl.BlockSpec((1, D_MODEL), lambda i, n: (0, 0)),
            row(BW), row(BW), row(BW), row(BW),
            pl.BlockSpec((BW, BW), lambda i, n: (0, 0)),
            gate(0), gate(1), gate(2), gate(3),
            pl.BlockSpec((N_BRANCH, BW, tn), lambda i, n: (0, 0, n)),
            pl.BlockSpec((tn, D_MODEL), lambda i, n: (n, 0)),
        ],
        out_specs=row(D_MODEL),
        out_shape=jax.ShapeDtypeStruct((t, D_MODEL), F32),
        scratch_shapes=[pltpu.VMEM((tm, D_MODEL), BF16), pltpu.VMEM((tm, BW), BF16)],
        compiler_params=_cparams(("parallel", "arbitrary")),
        name="gated_merge",
    )(x2d, g, ya, yb, yc, yd, w_glu, w_in_bf, w_in_bf, w_in_bf, w_in_bf, w_br, w_o)


def _ffn_body(x_ref, g_ref, wa_ref, wb_ref, wo_ref, o_ref, h_scr):
    @pl.when(pl.program_id(1) == 0)
    def _():
        x = x_ref[...]
        h_scr[...] = _rms(x, g_ref[...]).astype(BF16)
        o_ref[...] = x

    h = h_scr[...]
    a = _dot(h, wa_ref[...])
    b = _dot(h, wb_ref[...])
    o_ref[...] += _dot((jax.nn.silu(a) * b).astype(BF16), wo_ref[...])


def _ffn(x2d, g, w_in, w_out, tm, tf):
    t = x2d.shape[0]
    nf = D_FF // tf
    return pl.pallas_call(
        _ffn_body,
        grid=(t // tm, nf),
        in_specs=[
            pl.BlockSpec((tm, D_MODEL), lambda i, f: (i, 0)),
            pl.BlockSpec((1, D_MODEL), lambda i, f: (0, 0)),
            pl.BlockSpec((D_MODEL, tf), lambda i, f: (0, f)),
            pl.BlockSpec((D_MODEL, tf), lambda i, f: (0, nf + f)),
            pl.BlockSpec((tf, D_MODEL), lambda i, f: (f, 0)),
        ],
        out_specs=pl.BlockSpec((tm, D_MODEL), lambda i, f: (i, 0)),
        out_shape=jax.ShapeDtypeStruct((t, D_MODEL), F32),
        scratch_shapes=[pltpu.VMEM((tm, D_MODEL), BF16)],
        compiler_params=_cparams(("parallel", "arbitrary")),
        name="swiglu_ffn",
    )(x2d, g, w_in, w_in, w_out)


def _ple_body(x_ref, xs_ref, g_ref, p_ref, wg_ref, wp_ref, o_ref, h_scr):
    @pl.when(pl.program_id(1) == 0)
    def _():
        h_scr[...] = _rms(x_ref[...], g_ref[...]).astype(BF16)

    pg = jax.nn.sigmoid(_dot(h_scr[...], wg_ref[...]))
    o_ref[...] = xs_ref[...] + pg * _dot(p_ref[...].astype(BF16), wp_ref[...])


def _ple(x2d, g, p2d, w_gate, w_proj, tm, tn):
    t = x2d.shape[0]
    return pl.pallas_call(
        _ple_body,
        grid=(t // tm, D_MODEL // tn),
        in_specs=[
            pl.BlockSpec((tm, D_MODEL), lambda i, n: (i, 0)),
            pl.BlockSpec((tm, tn), lambda i, n: (i, n)),
            pl.BlockSpec((1, D_MODEL), lambda i, n: (0, 0)),
            pl.BlockSpec((tm, PLE_DIM), lambda i, n: (i, 0)),
            pl.BlockSpec((D_MODEL, tn), lambda i, n: (0, n)),
            pl.BlockSpec((PLE_DIM, tn), lambda i, n: (0, n)),
        ],
        out_specs=pl.BlockSpec((tm, tn), lambda i, n: (i, n)),
        out_shape=jax.ShapeDtypeStruct((t, D_MODEL), F32),
        scratch_shapes=[pltpu.VMEM((tm, D_MODEL), BF16)],
        compiler_params=_cparams(("parallel", "arbitrary")),
        name="ple_gate",
    )(x2d, x2d, g, p2d, w_gate, w_proj)


def _tile(t, want):
    return min(t, want)


def _layer(x2d, p2d, bn, seq, lw):
    t = bn * seq
    z = _inproj(x2d, lw['g_mix'], lw['w_in'], _tile(t, 512), 1024)
    ya = _s5_mixer(z, bn, seq, lw['s5'])
    yb = _conv_mixer(z, bn, seq, lw['conv_w'], _tile(seq, 1024))
    qn, kn, vb = _qkv_prep(z, lw['q_gain'], lw['k_gain'], _tile(t, 1024))
    yc = _na_mixer(qn, kn, vb, bn, seq, lw['na_bias'])
    yd = _fnet_mixer(z, bn, seq)
    x2d = _merge(x2d, lw['g_mix'], ya, yb, yc, yd, lw['w_glu'], lw['w_in'], lw['w_br'], lw['w_o'],
                 _tile(t, 512), 512)
    x2d = _ffn(x2d, lw['g_ffn'], lw['w_ffn_in'], lw['w_ffn_out'], _tile(t, 512), 512)
    return _ple(x2d, lw['g_ple'], p2d, lw['w_ple_gate'], lw['w_ple_proj'], _tile(t, 1024), 512)


def kernel(x_prompt, x_sample, p_prompt, p_sample, g_mix, w_in, s5_lam_re, s5_lam_im, s5_log_dt, s5_b_re,
           s5_b_im, s5_c_re, s5_c_im, s5_d, w_glu, conv_w, q_gain, k_gain, rel_bias, w_br, w_o, g_ffn,
           w_ffn_in, w_ffn_out, g_ple, w_ple_gate, w_ple_proj):
    depth = w_in.shape[0]
    layers = []
    for i in range(depth):
        layers.append(dict(
            g_mix=g_mix[i].astype(F32)[None], w_in=w_in[i].astype(BF16),
            s5=_s5_tables(s5_lam_re[i], s5_lam_im[i], s5_log_dt[i], s5_b_re[i], s5_b_im[i], s5_c_re[i],
                          s5_c_im[i], s5_d[i]),
            w_glu=w_glu[i].astype(BF16), conv_w=conv_w[i], q_gain=q_gain[i], k_gain=k_gain[i],
            na_bias=_na_bias_table(rel_bias[i]), w_br=w_br[i].astype(BF16), w_o=w_o[i].astype(BF16),
            g_ffn=g_ffn[i].astype(F32)[None], w_ffn_in=w_ffn_in[i].astype(BF16),
            w_ffn_out=w_ffn_out[i].astype(BF16), g_ple=g_ple[i].astype(F32)[None],
            w_ple_gate=w_ple_gate[i].astype(BF16), w_ple_proj=w_ple_proj[i].astype(BF16)))

    def trunk(x, p):
        bn, seq, _ = x.shape
        x2d = x.reshape(bn * seq, D_MODEL)
        for i in range(depth):
            x2d = _layer(x2d, p[i].reshape(bn * seq, PLE_DIM), bn, seq, layers[i])
        return x2d.reshape(bn, seq, D_MODEL)

    return trunk(x_prompt, p_prompt), trunk(x_sample, p_sample)
```

```python
import functools
import math

import numpy as np
import jax
import jax.numpy as jnp
from jax import lax
from jax.experimental import pallas as pl
from jax.experimental.pallas import tpu as pltpu

F32 = jnp.float32
BF16 = jnp.bfloat16

D_MODEL = 2048
BW = 512
MIX_IN = 8 * BW
N_BRANCH = 4
S5_H = 16
S5_G = BW // S5_H
S5_P = 64
S5_CHUNK = 16
LANE = 128
S5_JB = BW // LANE
S5_GB = LANE // S5_H
S5_CW = S5_CHUNK * LANE
SCAN_ROWS = 8
SCAN_STEPS = (1, 2, 4)
SCAN_ROWPOW = len(SCAN_STEPS)
SCAN_TILEPOW = SCAN_ROWPOW + 1
N_HEADS = 8
HEAD_DIM = 64
GRID_W = 64
NA_ROWS = 8
NA_COLS = 16
FNET_GW = 128
D_FF = 5632
PLE_DIM = 256
EPS = 1e-6
NEG_INF = -1e30
VMEM_LIMIT = 56 * 1024 * 1024


def _cparams(sem):
    return pltpu.CompilerParams(dimension_semantics=sem, vmem_limit_bytes=VMEM_LIMIT)


def _rms(x, g):
    ms = jnp.mean(x * x, axis=-1, keepdims=True)
    return x * lax.rsqrt(ms + EPS) * g


def _dot(a, b):
    return jnp.dot(a, b, preferred_element_type=F32)


def _inproj_body(x_ref, g_ref, w_ref, z_ref, h_scr):
    @pl.when(pl.program_id(1) == 0)
    def _():
        h_scr[...] = _rms(x_ref[...], g_ref[...]).astype(BF16)

    z_ref[...] = _dot(h_scr[...], w_ref[...])


def _inproj(x2d, g, w_bf, tm, tn):
    t = x2d.shape[0]
    return pl.pallas_call(
        _inproj_body,
        grid=(t // tm, MIX_IN // tn),
        in_specs=[
            pl.BlockSpec((tm, D_MODEL), lambda i, j: (i, 0)),
            pl.BlockSpec((1, D_MODEL), lambda i, j: (0, 0)),
            pl.BlockSpec((D_MODEL, tn), lambda i, j: (0, j)),
        ],
        out_specs=pl.BlockSpec((tm, tn), lambda i, j: (i, j)),
        out_shape=jax.ShapeDtypeStruct((t, MIX_IN), F32),
        scratch_shapes=[pltpu.VMEM((tm, D_MODEL), BF16)],
        compiler_params=_cparams(("parallel", "arbitrary")),
        name="inproj",
    )(x2d, g, w_bf)


def _s5_tables(lam_re, lam_im, log_dt, b_re, b_im, c_re, c_im, d_skip):
    hp = lax.Precision.HIGHEST
    f = lambda a: a.astype(F32)
    lam_re, lam_im, b_re, b_im, c_re, c_im = map(f, (lam_re, lam_im, b_re, b_im, c_re, c_im))
    dt = jnp.exp(f(log_dt))[..., None]
    lr, li = lam_re * dt, lam_im * dt
    mag = jnp.exp(lr)
    ar, ai = mag * jnp.cos(li), mag * jnp.sin(li)
    den = lam_re * lam_re + lam_im * lam_im
    fr = ((ar - 1.0) * lam_re + ai * lam_im) / den
    fi = (ai * lam_re - (ar - 1.0) * lam_im) / den
    bbr = fr[..., None] * b_re - fi[..., None] * b_im
    bbi = fr[..., None] * b_im + fi[..., None] * b_re
    n = jnp.arange(S5_CHUNK + 1, dtype=F32)[:, None, None, None]
    pr = jnp.exp(n * lr) * jnp.cos(n * li)
    pi = jnp.exp(n * lr) * jnp.sin(n * li)
    abr = pr[..., None] * bbr - pi[..., None] * bbi
    abi = pr[..., None] * bbi + pi[..., None] * bbr
    kk = (jnp.einsum('dghp,ndgpk->ndghk', c_re, abr, precision=hp)
          - jnp.einsum('dghp,ndgpk->ndghk', c_im, abi, precision=hp))
    tt = np.arange(S5_CHUNK)
    nq = S5_GB // 2
    dsk = f(d_skip).reshape(S5_G, S5_H)
    k0 = kk[0, 0] + kk[0, 1] + dsk[:, :, None] * jnp.asarray(np.eye(S5_H, dtype=np.float32))[None]
    kcomb = jnp.concatenate([kk[1:S5_CHUNK, 1][::-1], k0[None], kk[1:S5_CHUNK, 0]], axis=0)
    lag = tt[None, :] - tt[:, None]
    mfull = kcomb[lag + (S5_CHUNK - 1)]
    y = mfull.reshape(S5_CHUNK, S5_CHUNK, S5_JB, S5_GB, S5_H, S5_H)
    y = y.transpose(2, 0, 5, 1, 3, 4).reshape(S5_JB, S5_CHUNK, S5_H, S5_CW)
    out_group = (np.arange(S5_CW) // S5_H) % S5_GB
    row_group = np.arange(S5_GB)
    mmask = row_group[:, None, None] == out_group[None, None, :]
    m = jnp.where(mmask[None, None], y[:, :, None], 0.0).reshape(S5_JB, S5_CW, S5_CW)

    sw = 2 * S5_GB * 2 * S5_P
    ab = jnp.stack([abr, abi], axis=0)
    spow = (S5_CHUNK - 1 - tt, tt)
    sel = jnp.stack([ab[:, spow[d], d] for d in range(2)], axis=0)
    sel = sel.reshape(2, 2, S5_CHUNK, S5_JB, nq, 2, S5_P, S5_H)
    zs = sel.transpose(3, 2, 7, 0, 4, 1, 5, 6).reshape(S5_JB, S5_CHUNK, S5_H, sw)
    scol = np.arange(sw)
    state_group = ((scol // (2 * 2 * S5_P)) % nq) * 2 + (scol // S5_P) % 2
    smask = row_group[:, None, None] == state_group[None, None, :]
    s = jnp.where(smask[None, None], zs[:, :, None], 0.0).reshape(S5_JB, S5_CW, sw)

    def ca(d, e):
        car = c_re[d][None] * pr[e, d][:, :, None, :] - c_im[d][None] * pi[e, d][:, :, None, :]
        cai = c_re[d][None] * pi[e, d][:, :, None, :] + c_im[d][None] * pr[e, d][:, :, None, :]
        return jnp.stack([car, -cai], axis=0)
    og = jnp.stack([ca(0, tt + 1), ca(1, S5_CHUNK - tt)], axis=0)
    og = og.reshape(2, 2, S5_CHUNK, S5_JB, S5_GB, S5_H, S5_P)
    w = og.transpose(0, 3, 1, 6, 2, 4, 5).reshape(2, S5_JB, 2, S5_P, S5_CW)
    omask = (np.arange(nq)[:, None, None] * 2 + np.arange(2)[None, :, None]) == out_group[None, None, :]
    o = jnp.where(omask[None, None, :, None, :, None, :], w[:, :, None, :, None, :, :], 0.0)
    o = o.reshape(2, S5_JB, S5_GB * 2 * S5_P, S5_CW)

    nn = jnp.arange(SCAN_ROWS + 1, dtype=F32)[:, None, None, None] * S5_CHUNK
    cw = nq * 2 * S5_P
    pw = jnp.stack([(jnp.exp(nn * lr) * jnp.cos(nn * li)).reshape(SCAN_ROWS + 1, 2, S5_JB, cw),
                    (jnp.exp(nn * lr) * jnp.sin(nn * li)).reshape(SCAN_ROWS + 1, 2, S5_JB, cw)], axis=2)
    rows = np.arange(SCAN_ROWS)
    bcast = lambda a: jnp.broadcast_to(a[:, :, None], (2, 2, SCAN_ROWS, S5_JB, cw))
    rowpow = jnp.stack([pw[rows, 0], pw[SCAN_ROWS - 1 - rows, 1]], axis=0).transpose(0, 2, 1, 3, 4)
    tab = jnp.stack([bcast(pw[k]) for k in SCAN_STEPS] + [rowpow, bcast(pw[SCAN_ROWS])], axis=0)
    tab = tab.transpose(4, 1, 0, 2, 3, 5)
    return (m.astype(BF16), s.astype(BF16), o[0].astype(BF16), o[1].astype(BF16), tab)


def _chunk_rows(u_ref):
    return jnp.concatenate([u_ref[0, :, t, :] for t in range(S5_CHUNK)], axis=-1).astype(BF16)


def _s5_state_body(u_ref, s_ref, o_ref):
    o_ref[...] = _dot(_chunk_rows(u_ref), s_ref[0])


def _s5_scan_body(sf_ref, sb_ref, tab_ref, xf_ref, xb_ref, carry):
    @pl.when(pl.program_id(1) == 0)
    def _():
        carry[...] = jnp.zeros_like(carry)

    ntile = sf_ref.shape[0] // SCAN_ROWS
    nq = sf_ref.shape[1] // (2 * LANE)
    row = lax.broadcasted_iota(jnp.int32, (SCAN_ROWS, LANE), 0)

    def shift(v, k, d):
        if d == 0:
            return jnp.where(row >= k, pltpu.roll(v, k, axis=0), 0.0)
        return jnp.where(row < SCAN_ROWS - k, pltpu.roll(v, SCAN_ROWS - k, axis=0), 0.0)

    def cmul(d, kind, q, xr, xi):
        a_r = tab_ref[0, d, kind, 0, :, q * LANE:(q + 1) * LANE]
        a_i = tab_ref[0, d, kind, 1, :, q * LANE:(q + 1) * LANE]
        return a_r * xr - a_i * xi, a_r * xi + a_i * xr

    def tile(d, s_ref, x_ref, r0, cr):
        last = SCAN_ROWS - 1 if d == 0 else 0
        new = []
        for q in range(nq):
            re = slice(q * 2 * LANE, q * 2 * LANE + LANE)
            im = slice(q * 2 * LANE + LANE, (q + 1) * 2 * LANE)
            yr, yi = s_ref[pl.ds(r0, SCAN_ROWS), re], s_ref[pl.ds(r0, SCAN_ROWS), im]
            for kind, k in enumerate(SCAN_STEPS):
                tr, ti = cmul(d, kind, q, shift(yr, k, d), shift(yi, k, d))
                yr, yi = yr + tr, yi + ti
            c_r, c_i = cr[2 * q], cr[2 * q + 1]
            er, ei = cmul(d, SCAN_ROWPOW, q, c_r, c_i)
            x_ref[pl.ds(r0, SCAN_ROWS), re] = shift(yr, 1, d) + er
            x_ref[pl.ds(r0, SCAN_ROWS), im] = shift(yi, 1, d) + ei
            nr, ni = cmul(d, SCAN_TILEPOW, q, c_r, c_i)
            new.append(jnp.broadcast_to(yr[last:last + 1], (SCAN_ROWS, LANE)) + nr)
            new.append(jnp.broadcast_to(yi[last:last + 1], (SCAN_ROWS, LANE)) + ni)
        return tuple(new)

    def step(t, cs):
        cf, cb = cs
        rf = pl.multiple_of(t * SCAN_ROWS, SCAN_ROWS)
        rb = pl.multiple_of((ntile - 1 - t) * SCAN_ROWS, SCAN_ROWS)
        return tile(0, sf_ref, xf_ref, rf, cf), tile(1, sb_ref, xb_ref, rb, cb)

    init = tuple(tuple(carry[d, :, t * LANE:(t + 1) * LANE] for t in range(2 * nq)) for d in range(2))
    cf, cb = lax.fori_loop(0, ntile, step, init)
    for d, cs in enumerate((cf, cb)):
        for t in range(2 * nq):
            carry[d, :, t * LANE:(t + 1) * LANE] = cs[t]


def _s5_out_body(u_ref, xf_ref, xb_ref, m_ref, of_ref, ob_ref, y_ref):
    y = (_dot(_chunk_rows(u_ref), m_ref[0])
         + _dot(xf_ref[...].astype(BF16), of_ref[0])
         + _dot(xb_ref[...].astype(BF16), ob_ref[0]))
    y = jax.nn.gelu(y)
    for t in range(S5_CHUNK):
        y_ref[0, :, t, :] = y[:, t * LANE:(t + 1) * LANE]


def _s5_mixer(z, bn, seq, tabs):
    m, s, of, ob, scan_tab = tabs
    nc = seq // S5_CHUNK
    rbs = min(nc, 256)
    nrb = nc // rbs
    sw = s.shape[-1]
    half = sw // 2
    z5 = z.reshape(bn, nc, S5_CHUNK, MIX_IN)
    u_spec = pl.BlockSpec((1, rbs, S5_CHUNK, LANE), lambda j, b, r: (b, r, 0, j))
    st = pl.pallas_call(
        _s5_state_body,
        grid=(S5_JB, bn, nrb),
        in_specs=[u_spec, pl.BlockSpec((1, S5_CW, sw), lambda j, b, r: (j, 0, 0))],
        out_specs=pl.BlockSpec((rbs, sw), lambda j, b, r: (r, b * S5_JB + j)),
        out_shape=jax.ShapeDtypeStruct((nc, bn * S5_JB * sw), F32),
        compiler_params=_cparams(("parallel", "parallel", "parallel")),
        name="s5_state",
    )(z5, s)

    nseq = bn * S5_JB
    sbs = min(nc, 256)
    nsb = nc // sbs
    xf, xb = pl.pallas_call(
        _s5_scan_body,
        grid=(nseq, nsb),
        in_specs=[
            pl.BlockSpec((sbs, half), lambda q, i: (i, 2 * q)),
            pl.BlockSpec((sbs, half), lambda q, i: (nsb - 1 - i, 2 * q + 1)),
            pl.BlockSpec((1,) + scan_tab.shape[1:], lambda q, i: (q % S5_JB, 0, 0, 0, 0, 0)),
        ],
        out_specs=[
            pl.BlockSpec((sbs, half), lambda q, i: (i, q)),
            pl.BlockSpec((sbs, half), lambda q, i: (nsb - 1 - i, q)),
        ],
        out_shape=[jax.ShapeDtypeStruct((nc, nseq * half), F32)] * 2,
        scratch_shapes=[pltpu.VMEM((2, SCAN_ROWS, half), F32)],
        compiler_params=_cparams(("parallel", "arbitrary")),
        name="s5_scan",
    )(st, st, scan_tab)

    x_spec = pl.BlockSpec((rbs, half), lambda j, b, r: (r, b * S5_JB + j))
    y = pl.pallas_call(
        _s5_out_body,
        grid=(S5_JB, bn, nrb),
        in_specs=[
            u_spec, x_spec, x_spec,
            pl.BlockSpec((1, S5_CW, S5_CW), lambda j, b, r: (j, 0, 0)),
            pl.BlockSpec((1, half, S5_CW), lambda j, b, r: (j, 0, 0)),
            pl.BlockSpec((1, half, S5_CW), lambda j, b, r: (j, 0, 0)),
        ],
        out_specs=pl.BlockSpec((1, rbs, S5_CHUNK, LANE), lambda j, b, r: (b, r, 0, j)),
        out_shape=jax.ShapeDtypeStruct((bn, nc, S5_CHUNK, BW), F32),
        compiler_params=_cparams(("parallel", "parallel", "parallel")),
        name="s5_out",
    )(z5, xf, xb, m, of, ob)
    return y.reshape(bn * seq, BW)


def _conv_body(b_ref, c_ref, v_ref, cp_ref, vp_ref, cn_ref, vn_ref, w_ref, o_ref):
    i = pl.program_id(1)
    tc = c_ref.shape[1]
    z = c_ref[0] * v_ref[0]
    zp = cp_ref[0][7:8, :] * vp_ref[0][7:8, :]
    zn = cn_ref[0][0:1, :] * vn_ref[0][0:1, :]
    zp = jnp.where(i == 0, 0.0, zp)
    zn = jnp.where(i == pl.num_programs(1) - 1, 0.0, zn)
    row = lax.broadcasted_iota(jnp.int32, z.shape, 0)
    up = jnp.where(row == 0, zp, pltpu.roll(z, 1, axis=0))
    dn = jnp.where(row == tc - 1, zn, pltpu.roll(z, tc - 1, axis=0))
    w = w_ref[...]
    y = w[0:1, :] * up + w[1:2, :] * z + w[2:3, :] * dn
    o_ref[0] = (b_ref[0] * y).astype(o_ref.dtype)


def _conv_mixer(z, bn, seq, conv_w, tc):
    z3 = z.reshape(bn, seq, MIX_IN)
    nb8 = tc // 8
    last8 = seq // 8 - 1
    main = lambda col: pl.BlockSpec((1, tc, BW), lambda b, i: (b, i, col))
    prev = lambda col: pl.BlockSpec((1, 8, BW), lambda b, i: (b, jnp.maximum(i * nb8 - 1, 0), col))
    nxt = lambda col: pl.BlockSpec((1, 8, BW), lambda b, i: (b, jnp.minimum((i + 1) * nb8, last8), col))
    y = pl.pallas_call(
        _conv_body,
        grid=(bn, seq // tc),
        in_specs=[main(1), main(2), main(3), prev(2), prev(3), nxt(2), nxt(3),
                  pl.BlockSpec((3, BW), lambda b, i: (0, 0))],
        out_specs=pl.BlockSpec((1, tc, BW), lambda b, i: (b, i, 0)),
        out_shape=jax.ShapeDtypeStruct((bn, seq, BW), BF16),
        compiler_params=_cparams(("parallel", "parallel")),
        name="short_conv",
    )(z3, z3, z3, z3, z3, z3, z3, conv_w.astype(F32))
    return y.reshape(bn * seq, BW)


def _head_mean_sq(x, ones_ref):
    x2 = x * x
    hi = x2.astype(BF16)
    lo = (x2 - hi.astype(F32)).astype(BF16)
    return _dot(hi, ones_ref[...]) + _dot(lo, ones_ref[...])


def _qkv_body(q_ref, k_ref, v_ref, qg_ref, kg_ref, ones_ref, qo_ref, ko_ref, vo_ref):
    q = q_ref[...]
    k = k_ref[...]
    qn = q * lax.rsqrt(_head_mean_sq(q, ones_ref) + EPS) * qg_ref[...]
    kn = k * lax.rsqrt(_head_mean_sq(k, ones_ref) + EPS) * kg_ref[...]
    qo_ref[...] = (qn * (1.0 / math.sqrt(HEAD_DIM))).astype(BF16)
    ko_ref[...] = kn.astype(BF16)
    vo_ref[...] = v_ref[...].astype(BF16)


def _qkv_prep(z, q_gain, k_gain, tm):
    t = z.shape[0]
    ones = np.kron(np.eye(N_HEADS, dtype=np.float32), np.full((HEAD_DIM, HEAD_DIM), 1.0 / HEAD_DIM, np.float32))
    col = lambda c: pl.BlockSpec((tm, BW), lambda i: (i, c))
    vec = pl.BlockSpec((1, BW), lambda i: (0, 0))
    out = pl.BlockSpec((tm, BW), lambda i: (i, 0))
    return pl.pallas_call(
        _qkv_body,
        grid=(t // tm,),
        in_specs=[col(4), col(5), col(6), vec, vec, pl.BlockSpec((BW, BW), lambda i: (0, 0))],
        out_specs=[out, out, out],
        out_shape=[jax.ShapeDtypeStruct((t, BW), BF16)] * 3,
        compiler_params=_cparams(("parallel",)),
        name="qkv_prep",
    )(z, z, z, jnp.tile(q_gain.astype(F32), N_HEADS)[None], jnp.tile(k_gain.astype(F32), N_HEADS)[None],
      jnp.asarray(ones, BF16))


def _na_bias_table(rel_bias):
    qc = np.arange(GRID_W)[:, None]
    kc = np.arange(GRID_W)[None, :]
    cs = np.clip(qc - NA_COLS // 2, 0, GRID_W - NA_COLS)
    valid = (kc >= cs) & (kc < cs + NA_COLS)
    dc = np.clip(kc - qc + (NA_COLS - 1), 0, 2 * NA_COLS - 2)
    pick = (dc[None] == np.arange(2 * NA_COLS - 1)[:, None, None]).astype(np.float32)
    b = jnp.einsum('hrd,dqk->hrqk', rel_bias.astype(F32), jnp.asarray(pick), precision=lax.Precision.HIGHEST)
    b = jnp.where(valid[None, None], b, NEG_INF)
    ndr = 2 * NA_ROWS - 1
    bt = b.transpose(0, 2, 1, 3).reshape(N_HEADS, GRID_W, ndr * GRID_W)
    tab = jnp.stack([bt[:, :, s * GRID_W:(s + NA_ROWS) * GRID_W] for s in range(NA_ROWS)], axis=0)
    return tab.reshape(NA_ROWS, N_HEADS // 2, 2 * GRID_W, NA_ROWS * GRID_W)


def _na_body(q_ref, kp_ref, kc_ref, kn_ref, vp_ref, vc_ref, vn_ref, bias_ref, o_ref, k_scr, v_scr, *, rows):
    i = pl.program_id(1)
    blk = NA_ROWS * GRID_W
    for n, (kr, vr) in enumerate(((kp_ref, vp_ref), (kc_ref, vc_ref), (kn_ref, vn_ref))):
        k_scr[n * blk:(n + 1) * blk, :] = kr[0]
        v_scr[n * blk:(n + 1) * blk, :] = vr[0]
    first_head = lax.broadcasted_iota(jnp.int32, (GRID_W, 2 * HEAD_DIM), 1) < HEAD_DIM
    for ir in range(NA_ROWS):
        r = i * NA_ROWS + ir
        rs = jnp.clip(r - NA_ROWS // 2, 0, rows - NA_ROWS)
        off = pl.multiple_of((rs - (i - 1) * NA_ROWS) * GRID_W, GRID_W)
        bias_row = rs - r + (NA_ROWS - 1)
        for hp in range(N_HEADS // 2):
            lanes = slice(hp * 2 * HEAD_DIM, (hp + 1) * 2 * HEAD_DIM)
            q = q_ref[0, ir * GRID_W:(ir + 1) * GRID_W, lanes]
            zero = jnp.zeros_like(q)
            q2 = jnp.concatenate([jnp.where(first_head, q, zero), jnp.where(first_head, zero, q)], axis=0)
            kw = k_scr[pl.ds(off, blk), lanes]
            s = lax.dot_general(q2, kw, (((1,), (1,)), ((), ())), preferred_element_type=F32)
            s = s + bias_ref[bias_row, hp]
            e = jnp.exp(s - jnp.max(s, axis=-1, keepdims=True))
            l = jnp.sum(e, axis=-1, keepdims=True)
            vw = v_scr[pl.ds(off, blk), lanes]
            o = _dot(e.astype(BF16), vw) / l
            o_ref[0, ir * GRID_W:(ir + 1) * GRID_W, lanes] = jnp.where(
                first_head, o[:GRID_W], o[GRID_W:]).astype(o_ref.dtype)


def _na_mixer(qn, kn, vb, bn, seq, bias_tab):
    rows = seq // GRID_W
    nblk = rows // NA_ROWS
    assert rows % NA_ROWS == 0 and nblk >= 2
    blk = NA_ROWS * GRID_W
    q3, k3, v3 = (a.reshape(bn, seq, BW) for a in (qn, kn, vb))
    cur = pl.BlockSpec((1, blk, BW), lambda b, i: (b, i, 0))
    prev = pl.BlockSpec((1, blk, BW), lambda b, i: (b, jnp.maximum(i - 1, 0), 0))
    nxt = pl.BlockSpec((1, blk, BW), lambda b, i: (b, jnp.minimum(i + 1, nblk - 1), 0))
    y = pl.pallas_call(
        functools.partial(_na_body, rows=rows),
        grid=(bn, nblk),
        in_specs=[cur, prev, cur, nxt, prev, cur, nxt,
                  pl.BlockSpec((NA_ROWS, N_HEADS // 2, 2 * GRID_W, blk), lambda b, i: (0, 0, 0, 0))],
        out_specs=pl.BlockSpec((1, blk, BW), lambda b, i: (b, i, 0)),
        out_shape=jax.ShapeDtypeStruct((bn, seq, BW), BF16),
        scratch_shapes=[pltpu.VMEM((3 * blk, BW), BF16), pltpu.VMEM((3 * blk, BW), BF16)],
        compiler_params=_cparams(("parallel", "arbitrary")),
        name="nbr_attention",
    )(q3, k3, k3, k3, v3, v3, v3, bias_tab)
    return y.reshape(bn * seq, BW)


def _fnet_factors(seq):
    n1 = 1 << ((seq.bit_length() - 1 + 1) // 2)
    assert seq == n1 * (seq // n1) and seq & (seq - 1) == 0
    return n1, seq // n1


@functools.lru_cache(maxsize=None)
def _fnet_tables(seq):
    n1, n2 = _fnet_factors(seq)

    def cs(n, scale):
        idx = np.outer(np.arange(n), np.arange(n)) % n
        ang = 2.0 * np.pi * idx / n
        return np.cos(ang) * scale, np.sin(ang) * scale

    cc, sc = cs(FNET_GW, FNET_GW ** -0.5)
    chan = np.concatenate([cc, sc], axis=1)
    c1, s1 = cs(n1, n1 ** -0.5)
    d1 = np.block([[c1, -s1], [s1, c1]])
    c2, s2 = cs(n2, n2 ** -0.5)
    d2 = np.concatenate([c2, -s2], axis=1)
    ang = 2.0 * np.pi * (np.outer(np.arange(n2), np.arange(n1)) % seq) / seq
    twr = np.broadcast_to(np.cos(ang)[:, :, None], (n2, n1, LANE))
    twi = np.broadcast_to(np.sin(ang)[:, :, None], (n2, n1, LANE))
    return (np.asarray(chan, np.float32), np.asarray(d1, np.float32), np.asarray(d2, np.float32),
            np.ascontiguousarray(twr, np.float32), np.ascontiguousarray(twi, np.float32))


def _fnet1_body(x_ref, chan_ref, d1_ref, twr_ref, twi_ref, o_ref):
    n1, tb = x_ref.shape[1], x_ref.shape[2]
    ng = BW // FNET_GW
    for t in range(tb):
        x = x_ref[0, :, t, :].astype(BF16)
        pq = [_dot(x[:, g * FNET_GW:(g + 1) * FNET_GW], chan_ref[...]) for g in range(ng)]
        p = jnp.concatenate([a[:, :FNET_GW] for a in pq], axis=1)
        q = jnp.concatenate([a[:, FNET_GW:] for a in pq], axis=1)
        a = _dot(d1_ref[...], jnp.concatenate([p, q], axis=0).astype(BF16))
        ar, ai = a[:n1], a[n1:]
        twr = jnp.concatenate([twr_ref[t]] * ng, axis=1)
        twi = jnp.concatenate([twi_ref[t]] * ng, axis=1)
        o_ref[0, 0, :, t, :] = ar * twr - ai * twi
        o_ref[0, 1, :, t, :] = ar * twi + ai * twr


def _fnet2_body(a_ref, d2_ref, o_ref):
    kb, n2 = a_ref.shape[2], a_ref.shape[3]
    for k in range(kb):
        a = a_ref[0, :, k].reshape(2 * n2, BW).astype(BF16)
        o_ref[0, :, k, :] = _dot(d2_ref[...], a)


def _fnet_mixer(z, bn, seq):
    n1, n2 = _fnet_factors(seq)
    chan, d1, d2, twr, twi = _fnet_tables(seq)
    tb = 8
    zf = z.reshape(bn, n1, n2, MIX_IN)
    a = pl.pallas_call(
        _fnet1_body,
        grid=(bn, n2 // tb),
        in_specs=[
            pl.BlockSpec((1, n1, tb, BW), lambda b, j: (b, 0, j, 7)),
            pl.BlockSpec((FNET_GW, 2 * FNET_GW), lambda b, j: (0, 0)),
            pl.BlockSpec((2 * n1, 2 * n1), lambda b, j: (0, 0)),
            pl.BlockSpec((tb, n1, LANE), lambda b, j: (j, 0, 0)),
            pl.BlockSpec((tb, n1, LANE), lambda b, j: (j, 0, 0)),
        ],
        out_specs=pl.BlockSpec((1, 2, n1, tb, BW), lambda b, j: (b, 0, 0, j, 0)),
        out_shape=jax.ShapeDtypeStruct((bn, 2, n1, n2, BW), F32),
        compiler_params=_cparams(("parallel", "parallel")),
        name="fnet_stage1",
    )(zf, jnp.asarray(chan, BF16), jnp.asarray(d1, BF16), jnp.asarray(twr), jnp.asarray(twi))

    kb = 8
    y = pl.pallas_call(
        _fnet2_body,
        grid=(bn, n1 // kb),
        in_specs=[
            pl.BlockSpec((1, 2, kb, n2, BW), lambda b, i: (b, 0, i, 0, 0)),
            pl.BlockSpec((n2, 2 * n2), lambda b, i: (0, 0)),
        ],
        out_specs=pl.BlockSpec((1, n2, kb, BW), lambda b, i: (b, 0, i, 0)),
        out_shape=jax.ShapeDtypeStruct((bn, n2, n1, BW), F32),
        compiler_params=_cparams(("parallel", "parallel")),
        name="fnet_stage2",
    )(a, jnp.asarray(d2, BF16))
    return y.reshape(bn * seq, BW)


def _merge_body(x_ref, g_ref, ya_ref, yb_ref, yc_ref, yd_ref, wglu_ref, wg0_ref, wg1_ref, wg2_ref, wg3_ref,
                wbr_ref, wo_ref, o_ref, h_scr, ya_scr):
    @pl.when(pl.program_id(1) == 0)
    def _():
        x = x_ref[...]
        h_scr[...] = _rms(x, g_ref[...]).astype(BF16)
        ya = ya_ref[...]
        ya_scr[...] = (ya * jax.nn.sigmoid(_dot(ya.astype(BF16), wglu_ref[...]))).astype(BF16)
        o_ref[...] = x

    h = h_scr[...]
    ys = (ya_scr[...], yb_ref[...], yc_ref[...], yd_ref[...].astype(BF16))
    merged = None
    for kb, wg_ref in enumerate((wg0_ref, wg1_ref, wg2_ref, wg3_ref)):
        term = jax.nn.sigmoid(_dot(h, wg_ref[...])) * _dot(ys[kb], wbr_ref[kb])
        merged = term if merged is None else merged + term
    o_ref[...] += _dot(merged.astype(BF16), wo_ref[...])


def _merge(x2d, g, ya, yb, yc, yd, w_glu, w_in_bf, w_br, w_o, tm, tn):
    t = x2d.shape[0]
    row = lambda w: pl.BlockSpec((tm, w), lambda i, n: (i, 0))
    gate = lambda kb: pl.BlockSpec((D_MODEL, tn), lambda i, n: (0, (MIX_IN + kb * D_MODEL) // tn + n))
    return pl.pallas_call(
        _merge_body,
        grid=(t // tm, D_MODEL // tn),
        in_specs=[
            row(D_MODEL), pl.BlockSpec((1, D_MODEL), lambda i, n: (0, 0)),
            row(BW), row(BW), row(BW), row(BW),
            pl.BlockSpec((BW, BW), lambda i, n: (0, 0)),
            gate(0), gate(1), gate(2), gate(3),
            pl.BlockSpec((N_BRANCH, BW, tn), lambda i, n: (0, 0, n)),
            pl.BlockSpec((tn, D_MODEL), lambda i, n: (n, 0)),
        ],
        out_specs=row(D_MODEL),
        out_shape=jax.ShapeDtypeStruct((t, D_MODEL), F32),
        scratch_shapes=[pltpu.VMEM((tm, D_MODEL), BF16), pltpu.VMEM((tm, BW), BF16)],
        compiler_params=_cparams(("parallel", "arbitrary")),
        name="gated_merge",
    )(x2d, g, ya, yb, yc, yd, w_glu, w_in_bf, w_in_bf, w_in_bf, w_in_bf, w_br, w_o)


def _ffn_ple_body(x_ref, gf_ref, wa_ref, wb_ref, wo_ref, gp_ref, p_ref, wg_ref, wp_ref, o_ref, h_scr, *, nf, npn):
    f = pl.program_id(1)
    tn = wg_ref.shape[1]

    @pl.when(f == 0)
    def _():
        x = x_ref[...]
        h_scr[...] = _rms(x, gf_ref[...]).astype(BF16)
        o_ref[...] = x

    @pl.when(f < nf)
    def _():
        h = h_scr[...]
        a = _dot(h, wa_ref[...])
        b = _dot(h, wb_ref[...])
        o_ref[...] += _dot((jax.nn.silu(a) * b).astype(BF16), wo_ref[...])

    @pl.when(f == nf)
    def _():
        h_scr[...] = _rms(o_ref[...], gp_ref[...]).astype(BF16)

    for n in range(npn):
        @pl.when(f == nf + n)
        def _(n=n):
            pg = jax.nn.sigmoid(_dot(h_scr[...], wg_ref[...]))
            o_ref[:, n * tn:(n + 1) * tn] += pg * _dot(p_ref[...].astype(BF16), wp_ref[...])


def _ffn_ple(x2d, g_ffn, w_in, w_out, g_ple, p2d, w_gate, w_proj, tm, tf, tn):
    t = x2d.shape[0]
    nf = D_FF // tf
    npn = D_MODEL // tn
    ffn_step = lambda f: jnp.minimum(f, nf - 1)
    ple_step = lambda f: jnp.maximum(f - nf, 0)
    return pl.pallas_call(
        functools.partial(_ffn_ple_body, nf=nf, npn=npn),
        grid=(t // tm, nf + npn),
        in_specs=[
            pl.BlockSpec((tm, D_MODEL), lambda i, f: (i, 0)),
            pl.BlockSpec((1, D_MODEL), lambda i, f: (0, 0)),
            pl.BlockSpec((D_MODEL, tf), lambda i, f: (0, ffn_step(f))),
            pl.BlockSpec((D_MODEL, tf), lambda i, f: (0, nf + ffn_step(f))),
            pl.BlockSpec((tf, D_MODEL), lambda i, f: (ffn_step(f), 0)),
            pl.BlockSpec((1, D_MODEL), lambda i, f: (0, 0)),
            pl.BlockSpec((tm, PLE_DIM), lambda i, f: (i, 0)),
            pl.BlockSpec((D_MODEL, tn), lambda i, f: (0, ple_step(f))),
            pl.BlockSpec((PLE_DIM, tn), lambda i, f: (0, ple_step(f))),
        ],
        out_specs=pl.BlockSpec((tm, D_MODEL), lambda i, f: (i, 0)),
        out_shape=jax.ShapeDtypeStruct((t, D_MODEL), F32),
        scratch_shapes=[pltpu.VMEM((tm, D_MODEL), BF16)],
        compiler_params=_cparams(("parallel", "arbitrary")),
        name="swiglu_ffn_ple",
    )(x2d, g_ffn, w_in, w_in, w_out, g_ple, p2d, w_gate, w_proj)


def _tile(t, want):
    return min(t, want)


def _layer(x2d, p2d, bn, seq, lw):
    t = bn * seq
    z = _inproj(x2d, lw['g_mix'], lw['w_in'], _tile(t, 1024), 1024)
    ya = _s5_mixer(z, bn, seq, lw['s5'])
    yb = _conv_mixer(z, bn, seq, lw['conv_w'], _tile(seq, 1024))
    qn, kn, vb = _qkv_prep(z, lw['q_gain'], lw['k_gain'], _tile(t, 1024))
    yc = _na_mixer(qn, kn, vb, bn, seq, lw['na_bias'])
    yd = _fnet_mixer(z, bn, seq)
    x2d = _merge(x2d, lw['g_mix'], ya, yb, yc, yd, lw['w_glu'], lw['w_in'], lw['w_br'], lw['w_o'],
                 _tile(t, 512), 512)
    return _ffn_ple(x2d, lw['g_ffn'], lw['w_ffn_in'], lw['w_ffn_out'], lw['g_ple'], p2d, lw['w_ple_gate'],
                    lw['w_ple_proj'], _tile(t, 512), 512, 512)


def kernel(x_prompt, x_sample, p_prompt, p_sample, g_mix, w_in, s5_lam_re, s5_lam_im, s5_log_dt, s5_b_re,
           s5_b_im, s5_c_re, s5_c_im, s5_d, w_glu, conv_w, q_gain, k_gain, rel_bias, w_br, w_o, g_ffn,
           w_ffn_in, w_ffn_out, g_ple, w_ple_gate, w_ple_proj):
    depth = w_in.shape[0]
    layers = []
    for i in range(depth):
        layers.append(dict(
            g_mix=g_mix[i].astype(F32)[None], w_in=w_in[i].astype(BF16),
            s5=_s5_tables(s5_lam_re[i], s5_lam_im[i], s5_log_dt[i], s5_b_re[i], s5_b_im[i], s5_c_re[i],
                          s5_c_im[i], s5_d[i]),
            w_glu=w_glu[i].astype(BF16), conv_w=conv_w[i], q_gain=q_gain[i], k_gain=k_gain[i],
            na_bias=_na_bias_table(rel_bias[i]), w_br=w_br[i].astype(BF16), w_o=w_o[i].astype(BF16),
            g_ffn=g_ffn[i].astype(F32)[None], w_ffn_in=w_ffn_in[i].astype(BF16),
            w_ffn_out=w_ffn_out[i].astype(BF16), g_ple=g_ple[i].astype(F32)[None],
            w_ple_gate=w_ple_gate[i].astype(BF16), w_ple_proj=w_ple_proj[i].astype(BF16)))

    def trunk(x, p):
        bn, seq, _ = x.shape
        x2d = x.reshape(bn * seq, D_MODEL)
        for i in range(depth):
            x2d = _layer(x2d, p[i].reshape(bn * seq, PLE_DIM), bn, seq, layers[i])
        return x2d.reshape(bn, seq, D_MODEL)

    return trunk(x_prompt, p_prompt), trunk(x_sample, p_sample)
```

```python
import functools
import math

import numpy as np
import jax
import jax.numpy as jnp
from jax import lax
from jax.experimental import pallas as pl
from jax.experimental.pallas import tpu as pltpu

F32 = jnp.float32
BF16 = jnp.bfloat16

D_MODEL = 2048
BW = 512
MIX_IN = 8 * BW
N_BRANCH = 4
S5_H = 16
S5_G = BW // S5_H
S5_P = 64
S5_CHUNK = 16
LANE = 128
S5_JB = BW // LANE
S5_GB = LANE // S5_H
S5_CW = S5_CHUNK * LANE
SCAN_ROWS = 8
SCAN_STEPS = (1, 2, 4)
SCAN_ROWPOW = len(SCAN_STEPS)
SCAN_TILEPOW = SCAN_ROWPOW + 1
N_HEADS = 8
HEAD_DIM = 64
GRID_W = 64
NA_ROWS = 8
NA_COLS = 16
FNET_GW = 128
D_FF = 5632
PLE_DIM = 256
EPS = 1e-6
NEG_INF = -1e30
VMEM_LIMIT = 56 * 1024 * 1024


def _cparams(sem):
    return pltpu.CompilerParams(dimension_semantics=sem, vmem_limit_bytes=VMEM_LIMIT)


def _rms(x, g):
    ms = jnp.mean(x * x, axis=-1, keepdims=True)
    return x * lax.rsqrt(ms + EPS) * g


def _dot(a, b):
    return jnp.dot(a, b, preferred_element_type=F32)


def _inproj_body(x_ref, g_ref, w_ref, z_ref, h_scr):
    @pl.when(pl.program_id(1) == 0)
    def _():
        h_scr[...] = _rms(x_ref[...], g_ref[...]).astype(BF16)

    z_ref[...] = _dot(h_scr[...], w_ref[...])


def _inproj(x2d, g, w_bf, tm, tn):
    t = x2d.shape[0]
    return pl.pallas_call(
        _inproj_body,
        grid=(t // tm, MIX_IN // tn),
        in_specs=[
            pl.BlockSpec((tm, D_MODEL), lambda i, j: (i, 0)),
            pl.BlockSpec((1, D_MODEL), lambda i, j: (0, 0)),
            pl.BlockSpec((D_MODEL, tn), lambda i, j: (0, j)),
        ],
        out_specs=pl.BlockSpec((tm, tn), lambda i, j: (i, j)),
        out_shape=jax.ShapeDtypeStruct((t, MIX_IN), F32),
        scratch_shapes=[pltpu.VMEM((tm, D_MODEL), BF16)],
        compiler_params=_cparams(("parallel", "arbitrary")),
        name="inproj",
    )(x2d, g, w_bf)


def _s5_tables(lam_re, lam_im, log_dt, b_re, b_im, c_re, c_im, d_skip):
    hp = lax.Precision.HIGHEST
    f = lambda a: a.astype(F32)
    lam_re, lam_im, b_re, b_im, c_re, c_im = map(f, (lam_re, lam_im, b_re, b_im, c_re, c_im))
    dt = jnp.exp(f(log_dt))[..., None]
    lr, li = lam_re * dt, lam_im * dt
    mag = jnp.exp(lr)
    ar, ai = mag * jnp.cos(li), mag * jnp.sin(li)
    den = lam_re * lam_re + lam_im * lam_im
    fr = ((ar - 1.0) * lam_re + ai * lam_im) / den
    fi = (ai * lam_re - (ar - 1.0) * lam_im) / den
    bbr = fr[..., None] * b_re - fi[..., None] * b_im
    bbi = fr[..., None] * b_im + fi[..., None] * b_re
    n = jnp.arange(S5_CHUNK + 1, dtype=F32)[:, None, None, None]
    pr = jnp.exp(n * lr) * jnp.cos(n * li)
    pi = jnp.exp(n * lr) * jnp.sin(n * li)
    abr = pr[..., None] * bbr - pi[..., None] * bbi
    abi = pr[..., None] * bbi + pi[..., None] * bbr
    kk = (jnp.einsum('dghp,ndgpk->ndghk', c_re, abr, precision=hp)
          - jnp.einsum('dghp,ndgpk->ndghk', c_im, abi, precision=hp))
    tt = np.arange(S5_CHUNK)
    nq = S5_GB // 2
    dsk = f(d_skip).reshape(S5_G, S5_H)
    k0 = kk[0, 0] + kk[0, 1] + dsk[:, :, None] * jnp.asarray(np.eye(S5_H, dtype=np.float32))[None]
    kcomb = jnp.concatenate([kk[1:S5_CHUNK, 1][::-1], k0[None], kk[1:S5_CHUNK, 0]], axis=0)
    lag = tt[None, :] - tt[:, None]
    mfull = kcomb[lag + (S5_CHUNK - 1)]
    y = mfull.reshape(S5_CHUNK, S5_CHUNK, S5_JB, S5_GB, S5_H, S5_H)
    y = y.transpose(2, 0, 5, 1, 3, 4).reshape(S5_JB, S5_CHUNK, S5_H, S5_CW)
    out_group = (np.arange(S5_CW) // S5_H) % S5_GB
    row_group = np.arange(S5_GB)
    mmask = row_group[:, None, None] == out_group[None, None, :]
    m = jnp.where(mmask[None, None], y[:, :, None], 0.0).reshape(S5_JB, S5_CW, S5_CW)

    sw = 2 * S5_GB * 2 * S5_P
    ab = jnp.stack([abr, abi], axis=0)
    spow = (S5_CHUNK - 1 - tt, tt)
    sel = jnp.stack([ab[:, spow[d], d] for d in range(2)], axis=0)
    sel = sel.reshape(2, 2, S5_CHUNK, S5_JB, nq, 2, S5_P, S5_H)
    zs = sel.transpose(3, 2, 7, 0, 4, 1, 5, 6).reshape(S5_JB, S5_CHUNK, S5_H, sw)
    scol = np.arange(sw)
    state_group = ((scol // (2 * 2 * S5_P)) % nq) * 2 + (scol // S5_P) % 2
    smask = row_group[:, None, None] == state_group[None, None, :]
    s = jnp.where(smask[None, None], zs[:, :, None], 0.0).reshape(S5_JB, S5_CW, sw)

    def ca(d, e):
        car = c_re[d][None] * pr[e, d][:, :, None, :] - c_im[d][None] * pi[e, d][:, :, None, :]
        cai = c_re[d][None] * pi[e, d][:, :, None, :] + c_im[d][None] * pr[e, d][:, :, None, :]
        return jnp.stack([car, -cai], axis=0)
    og = jnp.stack([ca(0, tt + 1), ca(1, S5_CHUNK - tt)], axis=0)
    og = og.reshape(2, 2, S5_CHUNK, S5_JB, S5_GB, S5_H, S5_P)
    w = og.transpose(0, 3, 1, 6, 2, 4, 5).reshape(2, S5_JB, 2, S5_P, S5_CW)
    omask = (np.arange(nq)[:, None, None] * 2 + np.arange(2)[None, :, None]) == out_group[None, None, :]
    o = jnp.where(omask[None, None, :, None, :, None, :], w[:, :, None, :, None, :, :], 0.0)
    o = o.reshape(2, S5_JB, S5_GB * 2 * S5_P, S5_CW)

    nn = jnp.arange(SCAN_ROWS + 1, dtype=F32)[:, None, None, None] * S5_CHUNK
    cw = nq * 2 * S5_P
    pw = jnp.stack([(jnp.exp(nn * lr) * jnp.cos(nn * li)).reshape(SCAN_ROWS + 1, 2, S5_JB, cw),
                    (jnp.exp(nn * lr) * jnp.sin(nn * li)).reshape(SCAN_ROWS + 1, 2, S5_JB, cw)], axis=2)
    rows = np.arange(SCAN_ROWS)
    bcast = lambda a: jnp.broadcast_to(a[:, :, None], (2, 2, SCAN_ROWS, S5_JB, cw))
    rowpow = jnp.stack([pw[rows, 0], pw[SCAN_ROWS - 1 - rows, 1]], axis=0).transpose(0, 2, 1, 3, 4)
    tab = jnp.stack([bcast(pw[k]) for k in SCAN_STEPS] + [rowpow, bcast(pw[SCAN_ROWS])], axis=0)
    tab = tab.transpose(4, 1, 0, 2, 3, 5)
    return (m.astype(BF16), s.astype(BF16), o[0].astype(BF16), o[1].astype(BF16), tab)


def _chunk_rows(u_ref):
    return jnp.concatenate([u_ref[0, :, t, :] for t in range(S5_CHUNK)], axis=-1).astype(BF16)


def _s5_state_body(u_ref, s_ref, o_ref):
    o_ref[...] = _dot(_chunk_rows(u_ref), s_ref[0])


def _s5_scan_body(sf_ref, sb_ref, tab_ref, xf_ref, xb_ref, carry):
    @pl.when(pl.program_id(1) == 0)
    def _():
        carry[...] = jnp.zeros_like(carry)

    ntile = sf_ref.shape[0] // SCAN_ROWS
    nq = sf_ref.shape[1] // (2 * LANE)
    row = lax.broadcasted_iota(jnp.int32, (SCAN_ROWS, LANE), 0)

    def shift(v, k, d):
        if d == 0:
            return jnp.where(row >= k, pltpu.roll(v, k, axis=0), 0.0)
        return jnp.where(row < SCAN_ROWS - k, pltpu.roll(v, SCAN_ROWS - k, axis=0), 0.0)

    def cmul(d, kind, q, xr, xi):
        a_r = tab_ref[0, d, kind, 0, :, q * LANE:(q + 1) * LANE]
        a_i = tab_ref[0, d, kind, 1, :, q * LANE:(q + 1) * LANE]
        return a_r * xr - a_i * xi, a_r * xi + a_i * xr

    def tile(d, s_ref, x_ref, r0, cr):
        last = SCAN_ROWS - 1 if d == 0 else 0
        new = []
        for q in range(nq):
            re = slice(q * 2 * LANE, q * 2 * LANE + LANE)
            im = slice(q * 2 * LANE + LANE, (q + 1) * 2 * LANE)
            yr, yi = s_ref[pl.ds(r0, SCAN_ROWS), re], s_ref[pl.ds(r0, SCAN_ROWS), im]
            for kind, k in enumerate(SCAN_STEPS):
                tr, ti = cmul(d, kind, q, shift(yr, k, d), shift(yi, k, d))
                yr, yi = yr + tr, yi + ti
            c_r, c_i = cr[2 * q], cr[2 * q + 1]
            er, ei = cmul(d, SCAN_ROWPOW, q, c_r, c_i)
            x_ref[pl.ds(r0, SCAN_ROWS), re] = shift(yr, 1, d) + er
            x_ref[pl.ds(r0, SCAN_ROWS), im] = shift(yi, 1, d) + ei
            nr, ni = cmul(d, SCAN_TILEPOW, q, c_r, c_i)
            new.append(jnp.broadcast_to(yr[last:last + 1], (SCAN_ROWS, LANE)) + nr)
            new.append(jnp.broadcast_to(yi[last:last + 1], (SCAN_ROWS, LANE)) + ni)
        return tuple(new)

    def step(t, cs):
        cf, cb = cs
        rf = pl.multiple_of(t * SCAN_ROWS, SCAN_ROWS)
        rb = pl.multiple_of((ntile - 1 - t) * SCAN_ROWS, SCAN_ROWS)
        return tile(0, sf_ref, xf_ref, rf, cf), tile(1, sb_ref, xb_ref, rb, cb)

    init = tuple(tuple(carry[d, :, t * LANE:(t + 1) * LANE] for t in range(2 * nq)) for d in range(2))
    cf, cb = lax.fori_loop(0, ntile, step, init)
    for d, cs in enumerate((cf, cb)):
        for t in range(2 * nq):
            carry[d, :, t * LANE:(t + 1) * LANE] = cs[t]


def _s5_out_body(u_ref, xf_ref, xb_ref, m_ref, of_ref, ob_ref, y_ref):
    y = (_dot(_chunk_rows(u_ref), m_ref[0])
         + _dot(xf_ref[...].astype(BF16), of_ref[0])
         + _dot(xb_ref[...].astype(BF16), ob_ref[0]))
    y = jax.nn.gelu(y)
    for t in range(S5_CHUNK):
        y_ref[0, :, t, :] = y[:, t * LANE:(t + 1) * LANE]


def _s5_mixer(z, bn, seq, tabs):
    m, s, of, ob, scan_tab = tabs
    nc = seq // S5_CHUNK
    rbs = min(nc, 256)
    nrb = nc // rbs
    sw = s.shape[-1]
    half = sw // 2
    z5 = z.reshape(bn, nc, S5_CHUNK, MIX_IN)
    u_spec = pl.BlockSpec((1, rbs, S5_CHUNK, LANE), lambda j, b, r: (b, r, 0, j))
    st = pl.pallas_call(
        _s5_state_body,
        grid=(S5_JB, bn, nrb),
        in_specs=[u_spec, pl.BlockSpec((1, S5_CW, sw), lambda j, b, r: (j, 0, 0))],
        out_specs=pl.BlockSpec((rbs, sw), lambda j, b, r: (r, b * S5_JB + j)),
        out_shape=jax.ShapeDtypeStruct((nc, bn * S5_JB * sw), F32),
        compiler_params=_cparams(("parallel", "parallel", "parallel")),
        name="s5_state",
    )(z5, s)

    nseq = bn * S5_JB
    sbs = min(nc, 256)
    nsb = nc // sbs
    xf, xb = pl.pallas_call(
        _s5_scan_body,
        grid=(nseq, nsb),
        in_specs=[
            pl.BlockSpec((sbs, half), lambda q, i: (i, 2 * q)),
            pl.BlockSpec((sbs, half), lambda q, i: (nsb - 1 - i, 2 * q + 1)),
            pl.BlockSpec((1,) + scan_tab.shape[1:], lambda q, i: (q % S5_JB, 0, 0, 0, 0, 0)),
        ],
        out_specs=[
            pl.BlockSpec((sbs, half), lambda q, i: (i, q)),
            pl.BlockSpec((sbs, half), lambda q, i: (nsb - 1 - i, q)),
        ],
        out_shape=[jax.ShapeDtypeStruct((nc, nseq * half), F32)] * 2,
        scratch_shapes=[pltpu.VMEM((2, SCAN_ROWS, half), F32)],
        compiler_params=_cparams(("parallel", "arbitrary")),
        name="s5_scan",
    )(st, st, scan_tab)

    x_spec = pl.BlockSpec((rbs, half), lambda j, b, r: (r, b * S5_JB + j))
    y = pl.pallas_call(
        _s5_out_body,
        grid=(S5_JB, bn, nrb),
        in_specs=[
            u_spec, x_spec, x_spec,
            pl.BlockSpec((1, S5_CW, S5_CW), lambda j, b, r: (j, 0, 0)),
            pl.BlockSpec((1, half, S5_CW), lambda j, b, r: (j, 0, 0)),
            pl.BlockSpec((1, half, S5_CW), lambda j, b, r: (j, 0, 0)),
        ],
        out_specs=pl.BlockSpec((1, rbs, S5_CHUNK, LANE), lambda j, b, r: (b, r, 0, j)),
        out_shape=jax.ShapeDtypeStruct((bn, nc, S5_CHUNK, BW), F32),
        compiler_params=_cparams(("parallel", "parallel", "parallel")),
        name="s5_out",
    )(z5, xf, xb, m, of, ob)
    return y.reshape(bn * seq, BW)


def _conv_body(b_ref, c_ref, v_ref, cp_ref, vp_ref, cn_ref, vn_ref, w_ref, o_ref):
    i = pl.program_id(1)
    tc = c_ref.shape[1]
    z = c_ref[0] * v_ref[0]
    zp = cp_ref[0][7:8, :] * vp_ref[0][7:8, :]
    zn = cn_ref[0][0:1, :] * vn_ref[0][0:1, :]
    zp = jnp.where(i == 0, 0.0, zp)
    zn = jnp.where(i == pl.num_programs(1) - 1, 0.0, zn)
    row = lax.broadcasted_iota(jnp.int32, z.shape, 0)
    up = jnp.where(row == 0, zp, pltpu.roll(z, 1, axis=0))
    dn = jnp.where(row == tc - 1, zn, pltpu.roll(z, tc - 1, axis=0))
    w = w_ref[...]
    y = w[0:1, :] * up + w[1:2, :] * z + w[2:3, :] * dn
    o_ref[0] = (b_ref[0] * y).astype(o_ref.dtype)


def _conv_mixer(z, bn, seq, conv_w, tc):
    z3 = z.reshape(bn, seq, MIX_IN)
    nb8 = tc // 8
    last8 = seq // 8 - 1
    main = lambda col: pl.BlockSpec((1, tc, BW), lambda b, i: (b, i, col))
    prev = lambda col: pl.BlockSpec((1, 8, BW), lambda b, i: (b, jnp.maximum(i * nb8 - 1, 0), col))
    nxt = lambda col: pl.BlockSpec((1, 8, BW), lambda b, i: (b, jnp.minimum((i + 1) * nb8, last8), col))
    y = pl.pallas_call(
        _conv_body,
        grid=(bn, seq // tc),
        in_specs=[main(1), main(2), main(3), prev(2), prev(3), nxt(2), nxt(3),
                  pl.BlockSpec((3, BW), lambda b, i: (0, 0))],
        out_specs=pl.BlockSpec((1, tc, BW), lambda b, i: (b, i, 0)),
        out_shape=jax.ShapeDtypeStruct((bn, seq, BW), BF16),
        compiler_params=_cparams(("parallel", "parallel")),
        name="short_conv",
    )(z3, z3, z3, z3, z3, z3, z3, conv_w.astype(F32))
    return y.reshape(bn * seq, BW)


def _head_mean_sq(x, ones_ref):
    x2 = x * x
    hi = x2.astype(BF16)
    lo = (x2 - hi.astype(F32)).astype(BF16)
    return _dot(hi, ones_ref[...]) + _dot(lo, ones_ref[...])


def _qkv_body(q_ref, k_ref, v_ref, qg_ref, kg_ref, ones_ref, qo_ref, ko_ref, vo_ref):
    q = q_ref[...]
    k = k_ref[...]
    qn = q * lax.rsqrt(_head_mean_sq(q, ones_ref) + EPS) * qg_ref[...]
    kn = k * lax.rsqrt(_head_mean_sq(k, ones_ref) + EPS) * kg_ref[...]
    qo_ref[...] = (qn * (1.0 / math.sqrt(HEAD_DIM))).astype(BF16)
    ko_ref[...] = kn.astype(BF16)
    vo_ref[...] = v_ref[...].astype(BF16)


def _qkv_prep(z, q_gain, k_gain, tm):
    t = z.shape[0]
    ones = np.kron(np.eye(N_HEADS, dtype=np.float32), np.full((HEAD_DIM, HEAD_DIM), 1.0 / HEAD_DIM, np.float32))
    col = lambda c: pl.BlockSpec((tm, BW), lambda i: (i, c))
    vec = pl.BlockSpec((1, BW), lambda i: (0, 0))
    out = pl.BlockSpec((tm, BW), lambda i: (i, 0))
    return pl.pallas_call(
        _qkv_body,
        grid=(t // tm,),
        in_specs=[col(4), col(5), col(6), vec, vec, pl.BlockSpec((BW, BW), lambda i: (0, 0))],
        out_specs=[out, out, out],
        out_shape=[jax.ShapeDtypeStruct((t, BW), BF16)] * 3,
        compiler_params=_cparams(("parallel",)),
        name="qkv_prep",
    )(z, z, z, jnp.tile(q_gain.astype(F32), N_HEADS)[None], jnp.tile(k_gain.astype(F32), N_HEADS)[None],
      jnp.asarray(ones, BF16))


def _na_bias_table(rel_bias):
    qc = np.arange(GRID_W)[:, None]
    kc = np.arange(GRID_W)[None, :]
    cs = np.clip(qc - NA_COLS // 2, 0, GRID_W - NA_COLS)
    valid = (kc >= cs) & (kc < cs + NA_COLS)
    dc = np.clip(kc - qc + (NA_COLS - 1), 0, 2 * NA_COLS - 2)
    pick = (dc[None] == np.arange(2 * NA_COLS - 1)[:, None, None]).astype(np.float32)
    b = jnp.einsum('hrd,dqk->hrqk', rel_bias.astype(F32), jnp.asarray(pick), precision=lax.Precision.HIGHEST)
    b = jnp.where(valid[None, None], b, NEG_INF)
    ndr = 2 * NA_ROWS - 1
    bt = b.transpose(0, 2, 1, 3).reshape(N_HEADS, GRID_W, ndr * GRID_W)
    tab = jnp.stack([bt[:, :, s * GRID_W:(s + NA_ROWS) * GRID_W] for s in range(NA_ROWS)], axis=0)
    return tab.reshape(NA_ROWS, N_HEADS // 2, 2 * GRID_W, NA_ROWS * GRID_W)


def _na_body(q_ref, kp_ref, kc_ref, kn_ref, vp_ref, vc_ref, vn_ref, bias_ref, o_ref, k_scr, v_scr, *, rows):
    i = pl.program_id(1)
    blk = NA_ROWS * GRID_W
    for n, (kr, vr) in enumerate(((kp_ref, vp_ref), (kc_ref, vc_ref), (kn_ref, vn_ref))):
        k_scr[n * blk:(n + 1) * blk, :] = kr[0]
        v_scr[n * blk:(n + 1) * blk, :] = vr[0]
    first_head = lax.broadcasted_iota(jnp.int32, (GRID_W, 2 * HEAD_DIM), 1) < HEAD_DIM
    for ir in range(NA_ROWS):
        r = i * NA_ROWS + ir
        rs = jnp.clip(r - NA_ROWS // 2, 0, rows - NA_ROWS)
        off = pl.multiple_of((rs - (i - 1) * NA_ROWS) * GRID_W, GRID_W)
        bias_row = rs - r + (NA_ROWS - 1)
        pair_lanes = [slice(hp * 2 * HEAD_DIM, (hp + 1) * 2 * HEAD_DIM) for hp in range(N_HEADS // 2)]
        scores = []
        for hp, lanes in enumerate(pair_lanes):
            q = q_ref[0, ir * GRID_W:(ir + 1) * GRID_W, lanes]
            zero = jnp.zeros_like(q)
            q2 = jnp.concatenate([jnp.where(first_head, q, zero), jnp.where(first_head, zero, q)], axis=0)
            kw = k_scr[pl.ds(off, blk), lanes]
            s = lax.dot_general(q2, kw, (((1,), (1,)), ((), ())), preferred_element_type=F32)
            scores.append(s + bias_ref[bias_row, hp])
        probs = []
        for s in scores:
            e = jnp.exp(s - jnp.max(s, axis=-1, keepdims=True))
            probs.append((e.astype(BF16), jnp.sum(e, axis=-1, keepdims=True)))
        for lanes, (e, l) in zip(pair_lanes, probs):
            o = _dot(e, v_scr[pl.ds(off, blk), lanes]) / l
            o_ref[0, ir * GRID_W:(ir + 1) * GRID_W, lanes] = jnp.where(
                first_head, o[:GRID_W], o[GRID_W:]).astype(o_ref.dtype)


def _na_mixer(qn, kn, vb, bn, seq, bias_tab):
    rows = seq // GRID_W
    nblk = rows // NA_ROWS
    assert rows % NA_ROWS == 0 and nblk >= 2
    blk = NA_ROWS * GRID_W
    q3, k3, v3 = (a.reshape(bn, seq, BW) for a in (qn, kn, vb))
    cur = pl.BlockSpec((1, blk, BW), lambda b, i: (b, i, 0))
    prev = pl.BlockSpec((1, blk, BW), lambda b, i: (b, jnp.maximum(i - 1, 0), 0))
    nxt = pl.BlockSpec((1, blk, BW), lambda b, i: (b, jnp.minimum(i + 1, nblk - 1), 0))
    y = pl.pallas_call(
        functools.partial(_na_body, rows=rows),
        grid=(bn, nblk),
        in_specs=[cur, prev, cur, nxt, prev, cur, nxt,
                  pl.BlockSpec((NA_ROWS, N_HEADS // 2, 2 * GRID_W, blk), lambda b, i: (0, 0, 0, 0))],
        out_specs=pl.BlockSpec((1, blk, BW), lambda b, i: (b, i, 0)),
        out_shape=jax.ShapeDtypeStruct((bn, seq, BW), BF16),
        scratch_shapes=[pltpu.VMEM((3 * blk, BW), BF16), pltpu.VMEM((3 * blk, BW), BF16)],
        compiler_params=_cparams(("parallel", "arbitrary")),
        name="nbr_attention",
    )(q3, k3, k3, k3, v3, v3, v3, bias_tab)
    return y.reshape(bn * seq, BW)


def _fnet_factors(seq):
    n1 = 1 << ((seq.bit_length() - 1 + 1) // 2)
    assert seq == n1 * (seq // n1) and seq & (seq - 1) == 0
    return n1, seq // n1


@functools.lru_cache(maxsize=None)
def _fnet_tables(seq):
    n1, n2 = _fnet_factors(seq)

    def cs(n, scale):
        idx = np.outer(np.arange(n), np.arange(n)) % n
        ang = 2.0 * np.pi * idx / n
        return np.cos(ang) * scale, np.sin(ang) * scale

    cc, sc = cs(FNET_GW, FNET_GW ** -0.5)
    chan = np.concatenate([cc, sc], axis=1)
    c1, s1 = cs(n1, n1 ** -0.5)
    d1 = np.block([[c1, -s1], [s1, c1]])
    c2, s2 = cs(n2, n2 ** -0.5)
    d2 = np.concatenate([c2, -s2], axis=1)
    ang = 2.0 * np.pi * (np.outer(np.arange(n2), np.arange(n1)) % seq) / seq
    twr = np.broadcast_to(np.cos(ang)[:, :, None], (n2, n1, LANE))
    twi = np.broadcast_to(np.sin(ang)[:, :, None], (n2, n1, LANE))
    return (np.asarray(chan, np.float32), np.asarray(d1, np.float32), np.asarray(d2, np.float32),
            np.ascontiguousarray(twr, np.float32), np.ascontiguousarray(twi, np.float32))


def _fnet1_body(x_ref, chan_ref, d1_ref, twr_ref, twi_ref, o_ref):
    n1, tb = x_ref.shape[1], x_ref.shape[2]
    ng = BW // FNET_GW
    for t in range(tb):
        x = x_ref[0, :, t, :].astype(BF16)
        pq = [_dot(x[:, g * FNET_GW:(g + 1) * FNET_GW], chan_ref[...]) for g in range(ng)]
        p = jnp.concatenate([a[:, :FNET_GW] for a in pq], axis=1)
        q = jnp.concatenate([a[:, FNET_GW:] for a in pq], axis=1)
        a = _dot(d1_ref[...], jnp.concatenate([p, q], axis=0).astype(BF16))
        ar, ai = a[:n1], a[n1:]
        twr = jnp.concatenate([twr_ref[t]] * ng, axis=1)
        twi = jnp.concatenate([twi_ref[t]] * ng, axis=1)
        o_ref[0, 0, :, t, :] = ar * twr - ai * twi
        o_ref[0, 1, :, t, :] = ar * twi + ai * twr


def _fnet2_body(a_ref, d2_ref, o_ref):
    kb, n2 = a_ref.shape[2], a_ref.shape[3]
    for k in range(kb):
        a = a_ref[0, :, k].reshape(2 * n2, BW).astype(BF16)
        o_ref[0, :, k, :] = _dot(d2_ref[...], a)


def _fnet_mixer(z, bn, seq):
    n1, n2 = _fnet_factors(seq)
    chan, d1, d2, twr, twi = _fnet_tables(seq)
    tb = 8
    zf = z.reshape(bn, n1, n2, MIX_IN)
    a = pl.pallas_call(
        _fnet1_body,
        grid=(bn, n2 // tb),
        in_specs=[
            pl.BlockSpec((1, n1, tb, BW), lambda b, j: (b, 0, j, 7)),
            pl.BlockSpec((FNET_GW, 2 * FNET_GW), lambda b, j: (0, 0)),
            pl.BlockSpec((2 * n1, 2 * n1), lambda b, j: (0, 0)),
            pl.BlockSpec((tb, n1, LANE), lambda b, j: (j, 0, 0)),
            pl.BlockSpec((tb, n1, LANE), lambda b, j: (j, 0, 0)),
        ],
        out_specs=pl.BlockSpec((1, 2, n1, tb, BW), lambda b, j: (b, 0, 0, j, 0)),
        out_shape=jax.ShapeDtypeStruct((bn, 2, n1, n2, BW), F32),
        compiler_params=_cparams(("parallel", "parallel")),
        name="fnet_stage1",
    )(zf, jnp.asarray(chan, BF16), jnp.asarray(d1, BF16), jnp.asarray(twr), jnp.asarray(twi))

    kb = 8
    y = pl.pallas_call(
        _fnet2_body,
        grid=(bn, n1 // kb),
        in_specs=[
            pl.BlockSpec((1, 2, kb, n2, BW), lambda b, i: (b, 0, i, 0, 0)),
            pl.BlockSpec((n2, 2 * n2), lambda b, i: (0, 0)),
        ],
        out_specs=pl.BlockSpec((1, n2, kb, BW), lambda b, i: (b, 0, i, 0)),
        out_shape=jax.ShapeDtypeStruct((bn, n2, n1, BW), F32),
        compiler_params=_cparams(("parallel", "parallel")),
        name="fnet_stage2",
    )(a, jnp.asarray(d2, BF16))
    return y.reshape(bn * seq, BW)


def _merge_body(x_ref, g_ref, ya_ref, yb_ref, yc_ref, yd_ref, wglu_ref, wg0_ref, wg1_ref, wg2_ref, wg3_ref,
                wbr_ref, wo_ref, o_ref, h_scr, ya_scr):
    @pl.when(pl.program_id(1) == 0)
    def _():
        x = x_ref[...]
        h_scr[...] = _rms(x, g_ref[...]).astype(BF16)
        ya = ya_ref[...]
        ya_scr[...] = (ya * jax.nn.sigmoid(_dot(ya.astype(BF16), wglu_ref[...]))).astype(BF16)
        o_ref[...] = x

    h = h_scr[...]
    ys = (ya_scr[...], yb_ref[...], yc_ref[...], yd_ref[...].astype(BF16))
    merged = None
    for kb, wg_ref in enumerate((wg0_ref, wg1_ref, wg2_ref, wg3_ref)):
        term = jax.nn.sigmoid(_dot(h, wg_ref[...])) * _dot(ys[kb], wbr_ref[kb])
        merged = term if merged is None else merged + term
    o_ref[...] += _dot(merged.astype(BF16), wo_ref[...])


def _merge(x2d, g, ya, yb, yc, yd, w_glu, w_in_bf, w_br, w_o, tm, tn):
    t = x2d.shape[0]
    row = lambda w: pl.BlockSpec((tm, w), lambda i, n: (i, 0))
    gate = lambda kb: pl.BlockSpec((D_MODEL, tn), lambda i, n: (0, (MIX_IN + kb * D_MODEL) // tn + n))
    return pl.pallas_call(
        _merge_body,
        grid=(t // tm, D_MODEL // tn),
        in_specs=[
            row(D_MODEL), pl.BlockSpec((1, D_MODEL), lambda i, n: (0, 0)),
            row(BW), row(BW), row(BW), row(BW),
            pl.BlockSpec((BW, BW), lambda i, n: (0, 0)),
            gate(0), gate(1), gate(2), gate(3),
            pl.BlockSpec((N_BRANCH, BW, tn), lambda i, n: (0, 0, n)),
            pl.BlockSpec((tn, D_MODEL), lambda i, n: (n, 0)),
        ],
        out_specs=row(D_MODEL),
        out_shape=jax.ShapeDtypeStruct((t, D_MODEL), F32),
        scratch_shapes=[pltpu.VMEM((tm, D_MODEL), BF16), pltpu.VMEM((tm, BW), BF16)],
        compiler_params=_cparams(("parallel", "arbitrary")),
        name="gated_merge",
    )(x2d, g, ya, yb, yc, yd, w_glu, w_in_bf, w_in_bf, w_in_bf, w_in_bf, w_br, w_o)


def _ffn_ple_body(x_ref, gf_ref, wa_ref, wb_ref, wo_ref, gp_ref, p_ref, wg_ref, wp_ref, o_ref, h_scr, *, nf, tn):
    f = pl.program_id(1)

    @pl.when(f == 0)
    def _():
        x = x_ref[...]
        h_scr[...] = _rms(x, gf_ref[...]).astype(BF16)
        o_ref[...] = x

    @pl.when(f < nf)
    def _():
        h = h_scr[...]
        a = _dot(h, wa_ref[...])
        b = _dot(h, wb_ref[...])
        o_ref[...] += _dot((jax.nn.silu(a) * b).astype(BF16), wo_ref[...])

    @pl.when(f == nf)
    def _():
        h_scr[...] = _rms(o_ref[...], gp_ref[...]).astype(BF16)
        p = p_ref[...].astype(BF16)
        for n in range(D_MODEL // tn):
            cols = slice(n * tn, (n + 1) * tn)
            pg = jax.nn.sigmoid(_dot(h_scr[...], wg_ref[:, cols]))
            o_ref[:, cols] += pg * _dot(p, wp_ref[:, cols])


def _ffn_ple(x2d, g_ffn, w_in, w_out, g_ple, p2d, w_gate, w_proj, tm, tf, tn):
    t = x2d.shape[0]
    nf = D_FF // tf
    ffn_step = lambda f: jnp.minimum(f, nf - 1)
    return pl.pallas_call(
        functools.partial(_ffn_ple_body, nf=nf, tn=tn),
        grid=(t // tm, nf + 1),
        in_specs=[
            pl.BlockSpec((tm, D_MODEL), lambda i, f: (i, 0)),
            pl.BlockSpec((1, D_MODEL), lambda i, f: (0, 0)),
            pl.BlockSpec((D_MODEL, tf), lambda i, f: (0, ffn_step(f))),
            pl.BlockSpec((D_MODEL, tf), lambda i, f: (0, nf + ffn_step(f))),
            pl.BlockSpec((tf, D_MODEL), lambda i, f: (ffn_step(f), 0)),
            pl.BlockSpec((1, D_MODEL), lambda i, f: (0, 0)),
            pl.BlockSpec((tm, PLE_DIM), lambda i, f: (i, 0)),
            pl.BlockSpec((D_MODEL, D_MODEL), lambda i, f: (0, 0)),
            pl.BlockSpec((PLE_DIM, D_MODEL), lambda i, f: (0, 0)),
        ],
        out_specs=pl.BlockSpec((tm, D_MODEL), lambda i, f: (i, 0)),
        out_shape=jax.ShapeDtypeStruct((t, D_MODEL), F32),
        scratch_shapes=[pltpu.VMEM((tm, D_MODEL), BF16)],
        compiler_params=_cparams(("parallel", "arbitrary")),
        name="swiglu_ffn_ple",
    )(x2d, g_ffn, w_in, w_in, w_out, g_ple, p2d, w_gate, w_proj)


def _tile(t, want):
    return min(t, want)


def _layer(x2d, p2d, bn, seq, lw):
    t = bn * seq
    z = _inproj(x2d, lw['g_mix'], lw['w_in'], _tile(t, 1024), 1024)
    ya = _s5_mixer(z, bn, seq, lw['s5'])
    yb = _conv_mixer(z, bn, seq, lw['conv_w'], _tile(seq, 1024))
    qn, kn, vb = _qkv_prep(z, lw['q_gain'], lw['k_gain'], _tile(t, 1024))
    yc = _na_mixer(qn, kn, vb, bn, seq, lw['na_bias'])
    yd = _fnet_mixer(z, bn, seq)
    x2d = _merge(x2d, lw['g_mix'], ya, yb, yc, yd, lw['w_glu'], lw['w_in'], lw['w_br'], lw['w_o'],
                 _tile(t, 512), 512)
    return _ffn_ple(x2d, lw['g_ffn'], lw['w_ffn_in'], lw['w_ffn_out'], lw['g_ple'], p2d, lw['w_ple_gate'],
                    lw['w_ple_proj'], _tile(t, 512), 512, 512)


def kernel(x_prompt, x_sample, p_prompt, p_sample, g_mix, w_in, s5_lam_re, s5_lam_im, s5_log_dt, s5_b_re,
           s5_b_im, s5_c_re, s5_c_im, s5_d, w_glu, conv_w, q_gain, k_gain, rel_bias, w_br, w_o, g_ffn,
           w_ffn_in, w_ffn_out, g_ple, w_ple_gate, w_ple_proj):
    depth = w_in.shape[0]
    layers = []
    for i in range(depth):
        layers.append(dict(
            g_mix=g_mix[i].astype(F32)[None], w_in=w_in[i].astype(BF16),
            s5=_s5_tables(s5_lam_re[i], s5_lam_im[i], s5_log_dt[i], s5_b_re[i], s5_b_im[i], s5_c_re[i],
                          s5_c_im[i], s5_d[i]),
            w_glu=w_glu[i].astype(BF16), conv_w=conv_w[i], q_gain=q_gain[i], k_gain=k_gain[i],
            na_bias=_na_bias_table(rel_bias[i]), w_br=w_br[i].astype(BF16), w_o=w_o[i].astype(BF16),
            g_ffn=g_ffn[i].astype(F32)[None], w_ffn_in=w_ffn_in[i].astype(BF16),
            w_ffn_out=w_ffn_out[i].astype(BF16), g_ple=g_ple[i].astype(F32)[None],
            w_ple_gate=w_ple_gate[i].astype(BF16), w_ple_proj=w_ple_proj[i].astype(BF16)))

    def trunk(x, p):
        bn, seq, _ = x.shape
        x2d = x.reshape(bn * seq, D_MODEL)
        for i in range(depth):
            x2d = _layer(x2d, p[i].reshape(bn * seq, PLE_DIM), bn, seq, layers[i])
        return x2d.reshape(bn, seq, D_MODEL)

    return trunk(x_prompt, p_prompt), trunk(x_sample, p_sample)
```

```python
import functools
import math

import numpy as np
import jax
import jax.numpy as jnp
from jax import lax
from jax.experimental import pallas as pl
from jax.experimental.pallas import tpu as pltpu

F32 = jnp.float32
BF16 = jnp.bfloat16

D_MODEL = 2048
BW = 512
MIX_IN = 8 * BW
N_BRANCH = 4
S5_H = 16
S5_G = BW // S5_H
S5_P = 64
S5_CHUNK = 16
LANE = 128
S5_JB = BW // LANE
S5_GB = LANE // S5_H
S5_CW = S5_CHUNK * LANE
SCAN_ROWS = 8
SCAN_STEPS = (1, 2, 4)
SCAN_ROWPOW = len(SCAN_STEPS)
SCAN_TILEPOW = SCAN_ROWPOW + 1
N_HEADS = 8
HEAD_DIM = 64
GRID_W = 64
NA_ROWS = 8
NA_COLS = 16
FNET_GW = 128
D_FF = 5632
PLE_DIM = 256
EPS = 1e-6
NEG_INF = -1e30
VMEM_LIMIT = 56 * 1024 * 1024


def _cparams(sem):
    return pltpu.CompilerParams(dimension_semantics=sem, vmem_limit_bytes=VMEM_LIMIT)


def _rms(x, g):
    ms = jnp.mean(x * x, axis=-1, keepdims=True)
    return x * lax.rsqrt(ms + EPS) * g


def _dot(a, b):
    return jnp.dot(a, b, preferred_element_type=F32)


INPROJ_TN = 2 * BW
QK_STEP = 2
V_STEP = 3


def _head_mean_sq(x, ones_ref):
    x2 = x * x
    hi = x2.astype(BF16)
    lo = (x2 - hi.astype(F32)).astype(BF16)
    return _dot(hi, ones_ref[...]) + _dot(lo, ones_ref[...])


def _inproj_body(x_ref, g_ref, w_ref, qg_ref, kg_ref, ones_ref, z_ref, q_ref, k_ref, v_ref, h_scr):
    j = pl.program_id(1)

    @pl.when(j == 0)
    def _():
        h_scr[...] = _rms(x_ref[...], g_ref[...]).astype(BF16)

    acc = _dot(h_scr[...], w_ref[...])

    @pl.when(j != QK_STEP)
    def _():
        z_ref[...] = acc

    @pl.when(j == QK_STEP)
    def _():
        q, k = acc[:, :BW], acc[:, BW:]
        qn = q * lax.rsqrt(_head_mean_sq(q, ones_ref) + EPS) * qg_ref[...]
        kn = k * lax.rsqrt(_head_mean_sq(k, ones_ref) + EPS) * kg_ref[...]
        q_ref[...] = (qn * (1.0 / math.sqrt(HEAD_DIM))).astype(BF16)
        k_ref[...] = kn.astype(BF16)

    @pl.when(j == V_STEP)
    def _():
        v_ref[...] = acc[:, :BW].astype(BF16)


def _inproj(x2d, g, w_bf, q_gain, k_gain, tm):
    t = x2d.shape[0]
    ones = np.kron(np.eye(N_HEADS, dtype=np.float32), np.full((HEAD_DIM, HEAD_DIM), 1.0 / HEAD_DIM, np.float32))
    vec = pl.BlockSpec((1, BW), lambda i, j: (0, 0))
    head = pl.BlockSpec((tm, BW), lambda i, j: (i, 0))
    return pl.pallas_call(
        _inproj_body,
        grid=(t // tm, MIX_IN // INPROJ_TN),
        in_specs=[
            pl.BlockSpec((tm, D_MODEL), lambda i, j: (i, 0)),
            pl.BlockSpec((1, D_MODEL), lambda i, j: (0, 0)),
            pl.BlockSpec((D_MODEL, INPROJ_TN), lambda i, j: (0, j)),
            vec, vec, pl.BlockSpec((BW, BW), lambda i, j: (0, 0)),
        ],
        out_specs=[pl.BlockSpec((tm, INPROJ_TN), lambda i, j: (i, jnp.where(j == QK_STEP, QK_STEP - 1, j))),
                   head, head, head],
        out_shape=[jax.ShapeDtypeStruct((t, MIX_IN), F32)] + [jax.ShapeDtypeStruct((t, BW), BF16)] * 3,
        scratch_shapes=[pltpu.VMEM((tm, D_MODEL), BF16)],
        compiler_params=_cparams(("parallel", "arbitrary")),
        name="inproj",
    )(x2d, g, w_bf, jnp.tile(q_gain.astype(F32), N_HEADS)[None], jnp.tile(k_gain.astype(F32), N_HEADS)[None],
      jnp.asarray(ones, BF16))


def _s5_tables(lam_re, lam_im, log_dt, b_re, b_im, c_re, c_im, d_skip):
    hp = lax.Precision.HIGHEST
    f = lambda a: a.astype(F32)
    lam_re, lam_im, b_re, b_im, c_re, c_im = map(f, (lam_re, lam_im, b_re, b_im, c_re, c_im))
    dt = jnp.exp(f(log_dt))[..., None]
    lr, li = lam_re * dt, lam_im * dt
    mag = jnp.exp(lr)
    ar, ai = mag * jnp.cos(li), mag * jnp.sin(li)
    den = lam_re * lam_re + lam_im * lam_im
    fr = ((ar - 1.0) * lam_re + ai * lam_im) / den
    fi = (ai * lam_re - (ar - 1.0) * lam_im) / den
    bbr = (fr[..., None] * b_re - fi[..., None] * b_im).transpose(0, 1, 3, 2)
    bbi = (fr[..., None] * b_im + fi[..., None] * b_re).transpose(0, 1, 3, 2)
    n = jnp.arange(S5_CHUNK + 1, dtype=F32)[:, None, None, None]
    pr = jnp.exp(n * lr) * jnp.cos(n * li)
    pi = jnp.exp(n * lr) * jnp.sin(n * li)
    abr = pr[:, :, :, None, :] * bbr - pi[:, :, :, None, :] * bbi
    abi = pr[:, :, :, None, :] * bbi + pi[:, :, :, None, :] * bbr
    kk = (jnp.einsum('dghp,ndgkp->ndghk', c_re, abr, precision=hp)
          - jnp.einsum('dghp,ndgkp->ndghk', c_im, abi, precision=hp))
    tt = np.arange(S5_CHUNK)
    nq = S5_GB // 2
    dsk = f(d_skip).reshape(S5_G, S5_H)
    k0 = kk[0, 0] + kk[0, 1] + dsk[:, :, None] * jnp.asarray(np.eye(S5_H, dtype=np.float32))[None]
    kcomb = jnp.concatenate([kk[1:S5_CHUNK, 1][::-1], k0[None], kk[1:S5_CHUNK, 0]], axis=0)
    kcomb = kcomb.reshape(2 * S5_CHUNK - 1, S5_JB, S5_GB, S5_H, S5_H).transpose(1, 0, 4, 2, 3)
    kcomb = kcomb.reshape(S5_JB, 2 * S5_CHUNK - 1, S5_H, LANE)
    lag = tt[None, :] - tt[:, None]
    y = kcomb[:, lag + (S5_CHUNK - 1)]
    y = y.transpose(0, 1, 3, 2, 4).reshape(S5_JB, S5_CHUNK, S5_H, S5_CW)
    out_group = (np.arange(S5_CW) // S5_H) % S5_GB
    row_group = np.arange(S5_GB)
    mmask = row_group[:, None, None] == out_group[None, None, :]
    m = jnp.where(mmask[None, None], y[:, :, None], 0.0).reshape(S5_JB, S5_CW, S5_CW)

    sw = 2 * S5_GB * 2 * S5_P
    ab = jnp.stack([abr, abi], axis=0)
    spow = (S5_CHUNK - 1 - tt, tt)
    sel = jnp.stack([ab[:, spow[d], d] for d in range(2)], axis=0)
    sel = sel.reshape(2, 2, S5_CHUNK, S5_JB, nq, 2, S5_H, S5_P)
    zs = sel.transpose(3, 2, 6, 0, 4, 1, 5, 7).reshape(S5_JB, S5_CHUNK, S5_H, sw)
    scol = np.arange(sw)
    state_group = ((scol // (2 * 2 * S5_P)) % nq) * 2 + (scol // S5_P) % 2
    smask = row_group[:, None, None] == state_group[None, None, :]
    s = jnp.where(smask[None, None], zs[:, :, None], 0.0).reshape(S5_JB, S5_CW, sw)

    def lanes_gh(c):
        return c.reshape(2, S5_JB, S5_GB, S5_H, S5_P).transpose(0, 1, 4, 2, 3).reshape(2, S5_JB, S5_P, 1, LANE)

    def pow_gh(a):
        e = jnp.stack([a[tt + 1, 0], a[S5_CHUNK - tt, 1]], axis=0)
        e = e.reshape(2, S5_CHUNK, S5_JB, S5_GB, S5_P).transpose(0, 2, 4, 1, 3)
        return jnp.broadcast_to(e[..., None], e.shape + (S5_H,)).reshape(2, S5_JB, S5_P, S5_CHUNK, LANE)

    cr, ci, er, ei = lanes_gh(c_re), lanes_gh(c_im), pow_gh(pr), pow_gh(pi)
    w = jnp.stack([cr * er - ci * ei, -(cr * ei + ci * er)], axis=2)
    w = w.reshape(2, S5_JB, 2, S5_P, S5_CW)
    omask = (np.arange(nq)[:, None, None] * 2 + np.arange(2)[None, :, None]) == out_group[None, None, :]
    o = jnp.where(omask[None, None, :, None, :, None, :], w[:, :, None, :, None, :, :], 0.0)
    o = o.reshape(2, S5_JB, S5_GB * 2 * S5_P, S5_CW)

    nn = jnp.arange(SCAN_ROWS + 1, dtype=F32)[:, None, None, None] * S5_CHUNK
    cw = nq * 2 * S5_P
    pw = jnp.stack([(jnp.exp(nn * lr) * jnp.cos(nn * li)).reshape(SCAN_ROWS + 1, 2, S5_JB, cw),
                    (jnp.exp(nn * lr) * jnp.sin(nn * li)).reshape(SCAN_ROWS + 1, 2, S5_JB, cw)], axis=2)
    rows = np.arange(SCAN_ROWS)
    bcast = lambda a: jnp.broadcast_to(a[:, :, None], (2, 2, SCAN_ROWS, S5_JB, cw))
    rowpow = jnp.stack([pw[rows, 0], pw[SCAN_ROWS - 1 - rows, 1]], axis=0).transpose(0, 2, 1, 3, 4)
    tab = jnp.stack([bcast(pw[k]) for k in SCAN_STEPS] + [rowpow, bcast(pw[SCAN_ROWS])], axis=0)
    tab = tab.transpose(4, 1, 0, 2, 3, 5)
    return (m.astype(BF16), s.astype(BF16), o[0].astype(BF16), o[1].astype(BF16), tab)


def _chunk_rows(u_ref):
    return jnp.concatenate([u_ref[0, :, t, :] for t in range(S5_CHUNK)], axis=-1).astype(BF16)


def _s5_state_body(u_ref, s_ref, o_ref):
    o_ref[...] = _dot(_chunk_rows(u_ref), s_ref[0])


def _s5_scan_body(sf_ref, sb_ref, tab_ref, xf_ref, xb_ref, carry):
    @pl.when(pl.program_id(1) == 0)
    def _():
        carry[...] = jnp.zeros_like(carry)

    ntile = sf_ref.shape[0] // SCAN_ROWS
    nq = sf_ref.shape[1] // (2 * LANE)
    row = lax.broadcasted_iota(jnp.int32, (SCAN_ROWS, LANE), 0)

    def shift(v, k, d):
        if d == 0:
            return jnp.where(row >= k, pltpu.roll(v, k, axis=0), 0.0)
        return jnp.where(row < SCAN_ROWS - k, pltpu.roll(v, SCAN_ROWS - k, axis=0), 0.0)

    def cmul(d, kind, q, xr, xi):
        a_r = tab_ref[0, d, kind, 0, :, q * LANE:(q + 1) * LANE]
        a_i = tab_ref[0, d, kind, 1, :, q * LANE:(q + 1) * LANE]
        return a_r * xr - a_i * xi, a_r * xi + a_i * xr

    def tile(d, s_ref, x_ref, r0, cr):
        last = SCAN_ROWS - 1 if d == 0 else 0
        new = []
        for q in range(nq):
            re = slice(q * 2 * LANE, q * 2 * LANE + LANE)
            im = slice(q * 2 * LANE + LANE, (q + 1) * 2 * LANE)
            yr, yi = s_ref[pl.ds(r0, SCAN_ROWS), re], s_ref[pl.ds(r0, SCAN_ROWS), im]
            for kind, k in enumerate(SCAN_STEPS):
                tr, ti = cmul(d, kind, q, shift(yr, k, d), shift(yi, k, d))
                yr, yi = yr + tr, yi + ti
            c_r, c_i = cr[2 * q], cr[2 * q + 1]
            er, ei = cmul(d, SCAN_ROWPOW, q, c_r, c_i)
            x_ref[pl.ds(r0, SCAN_ROWS), re] = shift(yr, 1, d) + er
            x_ref[pl.ds(r0, SCAN_ROWS), im] = shift(yi, 1, d) + ei
            nr, ni = cmul(d, SCAN_TILEPOW, q, c_r, c_i)
            new.append(jnp.broadcast_to(yr[last:last + 1], (SCAN_ROWS, LANE)) + nr)
            new.append(jnp.broadcast_to(yi[last:last + 1], (SCAN_ROWS, LANE)) + ni)
        return tuple(new)

    def step(t, cs):
        cf, cb = cs
        rf = pl.multiple_of(t * SCAN_ROWS, SCAN_ROWS)
        rb = pl.multiple_of((ntile - 1 - t) * SCAN_ROWS, SCAN_ROWS)
        return tile(0, sf_ref, xf_ref, rf, cf), tile(1, sb_ref, xb_ref, rb, cb)

    init = tuple(tuple(carry[d, :, t * LANE:(t + 1) * LANE] for t in range(2 * nq)) for d in range(2))
    cf, cb = lax.fori_loop(0, ntile, step, init)
    for d, cs in enumerate((cf, cb)):
        for t in range(2 * nq):
            carry[d, :, t * LANE:(t + 1) * LANE] = cs[t]


def _s5_out_body(u_ref, xf_ref, xb_ref, m_ref, of_ref, ob_ref, y_ref):
    y = (_dot(_chunk_rows(u_ref), m_ref[0])
         + _dot(xf_ref[...].astype(BF16), of_ref[0])
         + _dot(xb_ref[...].astype(BF16), ob_ref[0]))
    y = jax.nn.gelu(y)
    for t in range(S5_CHUNK):
        y_ref[0, :, t, :] = y[:, t * LANE:(t + 1) * LANE]


def _s5_mixer(z, bn, seq, tabs):
    m, s, of, ob, scan_tab = tabs
    nc = seq // S5_CHUNK
    rbs = min(nc, 256)
    nrb = nc // rbs
    sw = s.shape[-1]
    half = sw // 2
    z5 = z.reshape(bn, nc, S5_CHUNK, MIX_IN)
    u_spec = pl.BlockSpec((1, rbs, S5_CHUNK, LANE), lambda j, b, r: (b, r, 0, j))
    st = pl.pallas_call(
        _s5_state_body,
        grid=(S5_JB, bn, nrb),
        in_specs=[u_spec, pl.BlockSpec((1, S5_CW, sw), lambda j, b, r: (j, 0, 0))],
        out_specs=pl.BlockSpec((rbs, sw), lambda j, b, r: (r, b * S5_JB + j)),
        out_shape=jax.ShapeDtypeStruct((nc, bn * S5_JB * sw), F32),
        compiler_params=_cparams(("parallel", "parallel", "parallel")),
        name="s5_state",
    )(z5, s)

    nseq = bn * S5_JB
    sbs = min(nc, 256)
    nsb = nc // sbs
    xf, xb = pl.pallas_call(
        _s5_scan_body,
        grid=(nseq, nsb),
        in_specs=[
            pl.BlockSpec((sbs, half), lambda q, i: (i, 2 * q)),
            pl.BlockSpec((sbs, half), lambda q, i: (nsb - 1 - i, 2 * q + 1)),
            pl.BlockSpec((1,) + scan_tab.shape[1:], lambda q, i: (q % S5_JB, 0, 0, 0, 0, 0)),
        ],
        out_specs=[
            pl.BlockSpec((sbs, half), lambda q, i: (i, q)),
            pl.BlockSpec((sbs, half), lambda q, i: (nsb - 1 - i, q)),
        ],
        out_shape=[jax.ShapeDtypeStruct((nc, nseq * half), F32)] * 2,
        scratch_shapes=[pltpu.VMEM((2, SCAN_ROWS, half), F32)],
        compiler_params=_cparams(("parallel", "arbitrary")),
        name="s5_scan",
    )(st, st, scan_tab)

    x_spec = pl.BlockSpec((rbs, half), lambda j, b, r: (r, b * S5_JB + j))
    y = pl.pallas_call(
        _s5_out_body,
        grid=(S5_JB, bn, nrb),
        in_specs=[
            u_spec, x_spec, x_spec,
            pl.BlockSpec((1, S5_CW, S5_CW), lambda j, b, r: (j, 0, 0)),
            pl.BlockSpec((1, half, S5_CW), lambda j, b, r: (j, 0, 0)),
            pl.BlockSpec((1, half, S5_CW), lambda j, b, r: (j, 0, 0)),
        ],
        out_specs=pl.BlockSpec((1, rbs, S5_CHUNK, LANE), lambda j, b, r: (b, r, 0, j)),
        out_shape=jax.ShapeDtypeStruct((bn, nc, S5_CHUNK, BW), F32),
        compiler_params=_cparams(("parallel", "parallel", "parallel")),
        name="s5_out",
    )(z5, xf, xb, m, of, ob)
    return y.reshape(bn * seq, BW)


def _conv_body(b_ref, c_ref, v_ref, cp_ref, vp_ref, cn_ref, vn_ref, w_ref, o_ref):
    i = pl.program_id(1)
    tc = c_ref.shape[1]
    z = c_ref[0] * v_ref[0]
    zp = cp_ref[0][7:8, :] * vp_ref[0][7:8, :]
    zn = cn_ref[0][0:1, :] * vn_ref[0][0:1, :]
    zp = jnp.where(i == 0, 0.0, zp)
    zn = jnp.where(i == pl.num_programs(1) - 1, 0.0, zn)
    row = lax.broadcasted_iota(jnp.int32, z.shape, 0)
    up = jnp.where(row == 0, zp, pltpu.roll(z, 1, axis=0))
    dn = jnp.where(row == tc - 1, zn, pltpu.roll(z, tc - 1, axis=0))
    w = w_ref[...]
    y = w[0:1, :] * up + w[1:2, :] * z + w[2:3, :] * dn
    o_ref[0] = (b_ref[0] * y).astype(o_ref.dtype)


def _conv_mixer(z, bn, seq, conv_w, tc):
    z3 = z.reshape(bn, seq, MIX_IN)
    nb8 = tc // 8
    last8 = seq // 8 - 1
    main = lambda col: pl.BlockSpec((1, tc, BW), lambda b, i: (b, i, col))
    prev = lambda col: pl.BlockSpec((1, 8, BW), lambda b, i: (b, jnp.maximum(i * nb8 - 1, 0), col))
    nxt = lambda col: pl.BlockSpec((1, 8, BW), lambda b, i: (b, jnp.minimum((i + 1) * nb8, last8), col))
    y = pl.pallas_call(
        _conv_body,
        grid=(bn, seq // tc),
        in_specs=[main(1), main(2), main(3), prev(2), prev(3), nxt(2), nxt(3),
                  pl.BlockSpec((3, BW), lambda b, i: (0, 0))],
        out_specs=pl.BlockSpec((1, tc, BW), lambda b, i: (b, i, 0)),
        out_shape=jax.ShapeDtypeStruct((bn, seq, BW), BF16),
        compiler_params=_cparams(("parallel", "parallel")),
        name="short_conv",
    )(z3, z3, z3, z3, z3, z3, z3, conv_w.astype(F32))
    return y.reshape(bn * seq, BW)


def _na_bias_table(rel_bias):
    qc = np.arange(GRID_W)[:, None]
    kc = np.arange(GRID_W)[None, :]
    cs = np.clip(qc - NA_COLS // 2, 0, GRID_W - NA_COLS)
    valid = (kc >= cs) & (kc < cs + NA_COLS)
    dc = np.clip(kc - qc + (NA_COLS - 1), 0, 2 * NA_COLS - 2)
    pick = (dc[None] == np.arange(2 * NA_COLS - 1)[:, None, None]).astype(np.float32)
    b = jnp.einsum('hrd,dqk->hrqk', rel_bias.astype(F32), jnp.asarray(pick), precision=lax.Precision.HIGHEST)
    b = jnp.where(valid[None, None], b, NEG_INF)
    ndr = 2 * NA_ROWS - 1
    bt = b.transpose(0, 2, 1, 3).reshape(N_HEADS, GRID_W, ndr * GRID_W)
    tab = jnp.stack([bt[:, :, s * GRID_W:(s + NA_ROWS) * GRID_W] for s in range(NA_ROWS)], axis=0)
    return tab.reshape(NA_ROWS, N_HEADS // 2, 2 * GRID_W, NA_ROWS * GRID_W)


def _na_body(q_ref, kp_ref, kc_ref, kn_ref, vp_ref, vc_ref, vn_ref, bias_ref, o_ref, k_scr, v_scr, *, rows):
    i = pl.program_id(1)
    blk = NA_ROWS * GRID_W
    for n, (kr, vr) in enumerate(((kp_ref, vp_ref), (kc_ref, vc_ref), (kn_ref, vn_ref))):
        k_scr[n * blk:(n + 1) * blk, :] = kr[0]
        v_scr[n * blk:(n + 1) * blk, :] = vr[0]
    first_head = lax.broadcasted_iota(jnp.int32, (GRID_W, 2 * HEAD_DIM), 1) < HEAD_DIM
    for ir in range(NA_ROWS):
        r = i * NA_ROWS + ir
        rs = jnp.clip(r - NA_ROWS // 2, 0, rows - NA_ROWS)
        off = pl.multiple_of((rs - (i - 1) * NA_ROWS) * GRID_W, GRID_W)
        bias_row = rs - r + (NA_ROWS - 1)
        pair_lanes = [slice(hp * 2 * HEAD_DIM, (hp + 1) * 2 * HEAD_DIM) for hp in range(N_HEADS // 2)]
        scores = []
        for hp, lanes in enumerate(pair_lanes):
            q = q_ref[0, ir * GRID_W:(ir + 1) * GRID_W, lanes]
            zero = jnp.zeros_like(q)
            q2 = jnp.concatenate([jnp.where(first_head, q, zero), jnp.where(first_head, zero, q)], axis=0)
            kw = k_scr[pl.ds(off, blk), lanes]
            s = lax.dot_general(q2, kw, (((1,), (1,)), ((), ())), preferred_element_type=F32)
            scores.append(s + bias_ref[bias_row, hp])
        probs = []
        for s in scores:
            e = jnp.exp(s - jnp.max(s, axis=-1, keepdims=True))
            probs.append((e.astype(BF16), jnp.sum(e, axis=-1, keepdims=True)))
        for lanes, (e, l) in zip(pair_lanes, probs):
            o = _dot(e, v_scr[pl.ds(off, blk), lanes]) / l
            o_ref[0, ir * GRID_W:(ir + 1) * GRID_W, lanes] = jnp.where(
                first_head, o[:GRID_W], o[GRID_W:]).astype(o_ref.dtype)


def _na_mixer(qn, kn, vb, bn, seq, bias_tab):
    rows = seq // GRID_W
    nblk = rows // NA_ROWS
    assert rows % NA_ROWS == 0 and nblk >= 2
    blk = NA_ROWS * GRID_W
    q3, k3, v3 = (a.reshape(bn, seq, BW) for a in (qn, kn, vb))
    cur = pl.BlockSpec((1, blk, BW), lambda b, i: (b, i, 0))
    prev = pl.BlockSpec((1, blk, BW), lambda b, i: (b, jnp.maximum(i - 1, 0), 0))
    nxt = pl.BlockSpec((1, blk, BW), lambda b, i: (b, jnp.minimum(i + 1, nblk - 1), 0))
    y = pl.pallas_call(
        functools.partial(_na_body, rows=rows),
        grid=(bn, nblk),
        in_specs=[cur, prev, cur, nxt, prev, cur, nxt,
                  pl.BlockSpec((NA_ROWS, N_HEADS // 2, 2 * GRID_W, blk), lambda b, i: (0, 0, 0, 0))],
        out_specs=pl.BlockSpec((1, blk, BW), lambda b, i: (b, i, 0)),
        out_shape=jax.ShapeDtypeStruct((bn, seq, BW), BF16),
        scratch_shapes=[pltpu.VMEM((3 * blk, BW), BF16), pltpu.VMEM((3 * blk, BW), BF16)],
        compiler_params=_cparams(("parallel", "arbitrary")),
        name="nbr_attention",
    )(q3, k3, k3, k3, v3, v3, v3, bias_tab)
    return y.reshape(bn * seq, BW)


def _fnet_factors(seq):
    n1 = 1 << ((seq.bit_length() - 1 + 1) // 2)
    assert seq == n1 * (seq // n1) and seq & (seq - 1) == 0
    return n1, seq // n1


@functools.lru_cache(maxsize=None)
def _fnet_tables(seq):
    n1, n2 = _fnet_factors(seq)

    def cs(n, scale):
        idx = np.outer(np.arange(n), np.arange(n)) % n
        ang = 2.0 * np.pi * idx / n
        return np.cos(ang) * scale, np.sin(ang) * scale

    cc, sc = cs(FNET_GW, FNET_GW ** -0.5)
    chan = np.concatenate([cc, sc], axis=1)
    c1, s1 = cs(n1, n1 ** -0.5)
    d1 = np.block([[c1, -s1], [s1, c1]])
    c2, s2 = cs(n2, n2 ** -0.5)
    d2 = np.concatenate([c2, -s2], axis=1)
    ang = 2.0 * np.pi * (np.outer(np.arange(n2), np.arange(n1)) % seq) / seq
    twr = np.broadcast_to(np.cos(ang)[:, :, None], (n2, n1, LANE))
    twi = np.broadcast_to(np.sin(ang)[:, :, None], (n2, n1, LANE))
    return (np.asarray(chan, np.float32), np.asarray(d1, np.float32), np.asarray(d2, np.float32),
            np.ascontiguousarray(twr, np.float32), np.ascontiguousarray(twi, np.float32))


def _fnet1_body(x_ref, chan_ref, d1_ref, twr_ref, twi_ref, o_ref):
    n1, tb = x_ref.shape[1], x_ref.shape[2]
    ng = BW // FNET_GW
    for t in range(tb):
        x = x_ref[0, :, t, :].astype(BF16)
        pq = [_dot(x[:, g * FNET_GW:(g + 1) * FNET_GW], chan_ref[...]) for g in range(ng)]
        p = jnp.concatenate([a[:, :FNET_GW] for a in pq], axis=1)
        q = jnp.concatenate([a[:, FNET_GW:] for a in pq], axis=1)
        a = _dot(d1_ref[...], jnp.concatenate([p, q], axis=0).astype(BF16))
        ar, ai = a[:n1], a[n1:]
        twr = jnp.concatenate([twr_ref[t]] * ng, axis=1)
        twi = jnp.concatenate([twi_ref[t]] * ng, axis=1)
        o_ref[0, 0, :, t, :] = ar * twr - ai * twi
        o_ref[0, 1, :, t, :] = ar * twi + ai * twr


def _fnet2_body(a_ref, d2_ref, o_ref):
    kb, n2 = a_ref.shape[2], a_ref.shape[3]
    for k in range(kb):
        a = a_ref[0, :, k].reshape(2 * n2, BW).astype(BF16)
        o_ref[0, :, k, :] = _dot(d2_ref[...], a)


def _fnet_mixer(z, bn, seq):
    n1, n2 = _fnet_factors(seq)
    chan, d1, d2, twr, twi = _fnet_tables(seq)
    tb = 8
    zf = z.reshape(bn, n1, n2, MIX_IN)
    a = pl.pallas_call(
        _fnet1_body,
        grid=(bn, n2 // tb),
        in_specs=[
            pl.BlockSpec((1, n1, tb, BW), lambda b, j: (b, 0, j, 7)),
            pl.BlockSpec((FNET_GW, 2 * FNET_GW), lambda b, j: (0, 0)),
            pl.BlockSpec((2 * n1, 2 * n1), lambda b, j: (0, 0)),
            pl.BlockSpec((tb, n1, LANE), lambda b, j: (j, 0, 0)),
            pl.BlockSpec((tb, n1, LANE), lambda b, j: (j, 0, 0)),
        ],
        out_specs=pl.BlockSpec((1, 2, n1, tb, BW), lambda b, j: (b, 0, 0, j, 0)),
        out_shape=jax.ShapeDtypeStruct((bn, 2, n1, n2, BW), F32),
        compiler_params=_cparams(("parallel", "parallel")),
        name="fnet_stage1",
    )(zf, jnp.asarray(chan, BF16), jnp.asarray(d1, BF16), jnp.asarray(twr), jnp.asarray(twi))

    kb = 8
    y = pl.pallas_call(
        _fnet2_body,
        grid=(bn, n1 // kb),
        in_specs=[
            pl.BlockSpec((1, 2, kb, n2, BW), lambda b, i: (b, 0, i, 0, 0)),
            pl.BlockSpec((n2, 2 * n2), lambda b, i: (0, 0)),
        ],
        out_specs=pl.BlockSpec((1, n2, kb, BW), lambda b, i: (b, 0, i, 0)),
        out_shape=jax.ShapeDtypeStruct((bn, n2, n1, BW), F32),
        compiler_params=_cparams(("parallel", "parallel")),
        name="fnet_stage2",
    )(a, jnp.asarray(d2, BF16))
    return y.reshape(bn * seq, BW)


def _merge_body(x_ref, g_ref, ya_ref, yb_ref, yc_ref, yd_ref, wglu_ref, wg0_ref, wg1_ref, wg2_ref, wg3_ref,
                wbr_ref, wo_ref, o_ref, h_scr, ya_scr):
    @pl.when(pl.program_id(1) == 0)
    def _():
        x = x_ref[...]
        h_scr[...] = _rms(x, g_ref[...]).astype(BF16)
        ya = ya_ref[...]
        ya_scr[...] = (ya * jax.nn.sigmoid(_dot(ya.astype(BF16), wglu_ref[...]))).astype(BF16)
        o_ref[...] = x

    h = h_scr[...]
    ys = (ya_scr[...], yb_ref[...], yc_ref[...], yd_ref[...].astype(BF16))
    merged = None
    for kb, wg_ref in enumerate((wg0_ref, wg1_ref, wg2_ref, wg3_ref)):
        term = jax.nn.sigmoid(_dot(h, wg_ref[...])) * _dot(ys[kb], wbr_ref[kb])
        merged = term if merged is None else merged + term
    o_ref[...] += _dot(merged.astype(BF16), wo_ref[...])


def _merge(x2d, g, ya, yb, yc, yd, w_glu, w_in_bf, w_br, w_o, tm, tn):
    t = x2d.shape[0]
    row = lambda w: pl.BlockSpec((tm, w), lambda i, n: (i, 0))
    gate = lambda kb: pl.BlockSpec((D_MODEL, tn), lambda i, n: (0, (MIX_IN + kb * D_MODEL) // tn + n))
    return pl.pallas_call(
        _merge_body,
        grid=(t // tm, D_MODEL // tn),
        in_specs=[
            row(D_MODEL), pl.BlockSpec((1, D_MODEL), lambda i, n: (0, 0)),
            row(BW), row(BW), row(BW), row(BW),
            pl.BlockSpec((BW, BW), lambda i, n: (0, 0)),
            gate(0), gate(1), gate(2), gate(3),
            pl.BlockSpec((N_BRANCH, BW, tn), lambda i, n: (0, 0, n)),
            pl.BlockSpec((tn, D_MODEL), lambda i, n: (n, 0)),
        ],
        out_specs=row(D_MODEL),
        out_shape=jax.ShapeDtypeStruct((t, D_MODEL), F32),
        scratch_shapes=[pltpu.VMEM((tm, D_MODEL), BF16), pltpu.VMEM((tm, BW), BF16)],
        compiler_params=_cparams(("parallel", "arbitrary")),
        name="gated_merge",
    )(x2d, g, ya, yb, yc, yd, w_glu, w_in_bf, w_in_bf, w_in_bf, w_in_bf, w_br, w_o)


def _ffn_ple_body(x_ref, gf_ref, wa_ref, wb_ref, wo_ref, gp_ref, p_ref, wg_ref, wp_ref, o_ref, h_scr, *, nf, tn):
    f = pl.program_id(1)

    @pl.when(f == 0)
    def _():
        x = x_ref[...]
        h_scr[...] = _rms(x, gf_ref[...]).astype(BF16)
        o_ref[...] = x

    @pl.when(f < nf)
    def _():
        h = h_scr[...]
        a = _dot(h, wa_ref[...])
        b = _dot(h, wb_ref[...])
        o_ref[...] += _dot((jax.nn.silu(a) * b).astype(BF16), wo_ref[...])

    @pl.when(f == nf)
    def _():
        h_scr[...] = _rms(o_ref[...], gp_ref[...]).astype(BF16)
        p = p_ref[...].astype(BF16)
        for n in range(D_MODEL // tn):
            cols = slice(n * tn, (n + 1) * tn)
            pg = jax.nn.sigmoid(_dot(h_scr[...], wg_ref[:, cols]))
            o_ref[:, cols] += pg * _dot(p, wp_ref[:, cols])


def _ffn_ple(x2d, g_ffn, w_in, w_out, g_ple, p2d, w_gate, w_proj, tm, tf, tn):
    t = x2d.shape[0]
    nf = D_FF // tf
    ffn_step = lambda f: jnp.minimum(f, nf - 1)
    return pl.pallas_call(
        functools.partial(_ffn_ple_body, nf=nf, tn=tn),
        grid=(t // tm, nf + 1),
        in_specs=[
            pl.BlockSpec((tm, D_MODEL), lambda i, f: (i, 0)),
            pl.BlockSpec((1, D_MODEL), lambda i, f: (0, 0)),
            pl.BlockSpec((D_MODEL, tf), lambda i, f: (0, ffn_step(f))),
            pl.BlockSpec((D_MODEL, tf), lambda i, f: (0, nf + ffn_step(f))),
            pl.BlockSpec((tf, D_MODEL), lambda i, f: (ffn_step(f), 0)),
            pl.BlockSpec((1, D_MODEL), lambda i, f: (0, 0)),
            pl.BlockSpec((tm, PLE_DIM), lambda i, f: (i, 0)),
            pl.BlockSpec((D_MODEL, D_MODEL), lambda i, f: (0, 0)),
            pl.BlockSpec((PLE_DIM, D_MODEL), lambda i, f: (0, 0)),
        ],
        out_specs=pl.BlockSpec((tm, D_MODEL), lambda i, f: (i, 0)),
        out_shape=jax.ShapeDtypeStruct((t, D_MODEL), F32),
        scratch_shapes=[pltpu.VMEM((tm, D_MODEL), BF16)],
        compiler_params=_cparams(("parallel", "arbitrary")),
        name="swiglu_ffn_ple",
    )(x2d, g_ffn, w_in, w_in, w_out, g_ple, p2d, w_gate, w_proj)


def _tile(t, want):
    return min(t, want)


def _layer(x2d, p2d, bn, seq, lw):
    t = bn * seq
    z, qn, kn, vb = _inproj(x2d, lw['g_mix'], lw['w_in'], lw['q_gain'], lw['k_gain'], _tile(t, 1024))
    ya = _s5_mixer(z, bn, seq, lw['s5'])
    yb = _conv_mixer(z, bn, seq, lw['conv_w'], _tile(seq, 1024))
    yc = _na_mixer(qn, kn, vb, bn, seq, lw['na_bias'])
    yd = _fnet_mixer(z, bn, seq)
    x2d = _merge(x2d, lw['g_mix'], ya, yb, yc, yd, lw['w_glu'], lw['w_in'], lw['w_br'], lw['w_o'],
                 _tile(t, 512), 512)
    return _ffn_ple(x2d, lw['g_ffn'], lw['w_ffn_in'], lw['w_ffn_out'], lw['g_ple'], p2d, lw['w_ple_gate'],
                    lw['w_ple_proj'], _tile(t, 512), 512, 512)


def kernel(x_prompt, x_sample, p_prompt, p_sample, g_mix, w_in, s5_lam_re, s5_lam_im, s5_log_dt, s5_b_re,
           s5_b_im, s5_c_re, s5_c_im, s5_d, w_glu, conv_w, q_gain, k_gain, rel_bias, w_br, w_o, g_ffn,
           w_ffn_in, w_ffn_out, g_ple, w_ple_gate, w_ple_proj):
    depth = w_in.shape[0]
    layers = []
    for i in range(depth):
        layers.append(dict(
            g_mix=g_mix[i].astype(F32)[None], w_in=w_in[i].astype(BF16),
            s5=_s5_tables(s5_lam_re[i], s5_lam_im[i], s5_log_dt[i], s5_b_re[i], s5_b_im[i], s5_c_re[i],
                          s5_c_im[i], s5_d[i]),
            w_glu=w_glu[i].astype(BF16), conv_w=conv_w[i], q_gain=q_gain[i], k_gain=k_gain[i],
            na_bias=_na_bias_table(rel_bias[i]), w_br=w_br[i].astype(BF16), w_o=w_o[i].astype(BF16),
            g_ffn=g_ffn[i].astype(F32)[None], w_ffn_in=w_ffn_in[i].astype(BF16),
            w_ffn_out=w_ffn_out[i].astype(BF16), g_ple=g_ple[i].astype(F32)[None],
            w_ple_gate=w_ple_gate[i].astype(BF16), w_ple_proj=w_ple_proj[i].astype(BF16)))

    def trunk(x, p):
        bn, seq, _ = x.shape
        x2d = x.reshape(bn * seq, D_MODEL)
        for i in range(depth):
            x2d = _layer(x2d, p[i].reshape(bn * seq, PLE_DIM), bn, seq, layers[i])
        return x2d.reshape(bn, seq, D_MODEL)

    return trunk(x_prompt, p_prompt), trunk(x_sample, p_sample)
```

```python
import functools
import math

import numpy as np
import jax
import jax.numpy as jnp
from jax import lax
from jax.experimental import pallas as pl
from jax.experimental.pallas import tpu as pltpu

F32 = jnp.float32
BF16 = jnp.bfloat16

D_MODEL = 2048
BW = 512
MIX_IN = 8 * BW
N_BRANCH = 4
S5_H = 16
S5_G = BW // S5_H
S5_P = 64
S5_CHUNK = 16
LANE = 128
S5_JB = BW // LANE
S5_GB = LANE // S5_H
S5_CW = S5_CHUNK * LANE
SCAN_ROWS = 8
SCAN_STEPS = (1, 2, 4)
SCAN_ROWPOW = len(SCAN_STEPS)
SCAN_TILEPOW = SCAN_ROWPOW + 1
N_HEADS = 8
HEAD_DIM = 64
GRID_W = 64
NA_ROWS = 8
NA_COLS = 16
FNET_GW = 128
D_FF = 5632
PLE_DIM = 256
EPS = 1e-6
NEG_INF = -1e30
VMEM_LIMIT = 56 * 1024 * 1024


def _cparams(sem):
    return pltpu.CompilerParams(dimension_semantics=sem, vmem_limit_bytes=VMEM_LIMIT)


def _rms(x, g):
    ms = jnp.mean(x * x, axis=-1, keepdims=True)
    return x * lax.rsqrt(ms + EPS) * g


def _dot(a, b):
    return jnp.dot(a, b, preferred_element_type=F32)


INPROJ_TN = 2 * BW
QK_STEP = 2
V_STEP = 3


def _head_mean_sq(x, ones_ref):
    x2 = x * x
    hi = x2.astype(BF16)
    lo = (x2 - hi.astype(F32)).astype(BF16)
    return _dot(hi, ones_ref[...]) + _dot(lo, ones_ref[...])


def _inproj_body(x_ref, g_ref, w_ref, qg_ref, kg_ref, ones_ref, z_ref, q_ref, k_ref, v_ref, h_scr):
    j = pl.program_id(1)

    @pl.when(j == 0)
    def _():
        h_scr[...] = _rms(x_ref[...], g_ref[...]).astype(BF16)

    acc = _dot(h_scr[...], w_ref[...])

    @pl.when(j != QK_STEP)
    def _():
        z_ref[...] = acc

    @pl.when(j == QK_STEP)
    def _():
        q, k = acc[:, :BW], acc[:, BW:]
        qn = q * lax.rsqrt(_head_mean_sq(q, ones_ref) + EPS) * qg_ref[...]
        kn = k * lax.rsqrt(_head_mean_sq(k, ones_ref) + EPS) * kg_ref[...]
        q_ref[...] = (qn * (1.0 / math.sqrt(HEAD_DIM))).astype(BF16)
        k_ref[...] = kn.astype(BF16)

    @pl.when(j == V_STEP)
    def _():
        v_ref[...] = acc[:, :BW].astype(BF16)


def _inproj(x2d, g, w_bf, q_gain, k_gain, tm):
    t = x2d.shape[0]
    ones = np.kron(np.eye(N_HEADS, dtype=np.float32), np.full((HEAD_DIM, HEAD_DIM), 1.0 / HEAD_DIM, np.float32))
    vec = pl.BlockSpec((1, BW), lambda i, j: (0, 0))
    head = pl.BlockSpec((tm, BW), lambda i, j: (i, 0))
    return pl.pallas_call(
        _inproj_body,
        grid=(t // tm, MIX_IN // INPROJ_TN),
        in_specs=[
            pl.BlockSpec((tm, D_MODEL), lambda i, j: (i, 0)),
            pl.BlockSpec((1, D_MODEL), lambda i, j: (0, 0)),
            pl.BlockSpec((D_MODEL, INPROJ_TN), lambda i, j: (0, j)),
            vec, vec, pl.BlockSpec((BW, BW), lambda i, j: (0, 0)),
        ],
        out_specs=[pl.BlockSpec((tm, INPROJ_TN), lambda i, j: (i, jnp.where(j == QK_STEP, QK_STEP - 1, j))),
                   head, head, head],
        out_shape=[jax.ShapeDtypeStruct((t, MIX_IN), F32)] + [jax.ShapeDtypeStruct((t, BW), BF16)] * 3,
        scratch_shapes=[pltpu.VMEM((tm, D_MODEL), BF16)],
        compiler_params=_cparams(("parallel", "arbitrary")),
        name="inproj",
    )(x2d, g, w_bf, jnp.tile(q_gain.astype(F32), N_HEADS)[None], jnp.tile(k_gain.astype(F32), N_HEADS)[None],
      jnp.asarray(ones, BF16))


def _s5_tables(lam_re, lam_im, log_dt, b_re, b_im, c_re, c_im, d_skip):
    f = lambda a: a.astype(F32)
    lam_re, lam_im, b_re, b_im, c_re, c_im = map(f, (lam_re, lam_im, b_re, b_im, c_re, c_im))
    dt = jnp.exp(f(log_dt))[..., None]
    lr, li = lam_re * dt, lam_im * dt
    mag = jnp.exp(lr)
    ar, ai = mag * jnp.cos(li), mag * jnp.sin(li)
    den = lam_re * lam_re + lam_im * lam_im
    fr = ((ar - 1.0) * lam_re + ai * lam_im) / den
    fi = (ai * lam_re - (ar - 1.0) * lam_im) / den
    bbr = (fr[..., None] * b_re - fi[..., None] * b_im).transpose(0, 1, 3, 2)
    bbi = (fr[..., None] * b_im + fi[..., None] * b_re).transpose(0, 1, 3, 2)
    n = jnp.arange(S5_CHUNK + 1, dtype=F32)[:, None, None, None]
    pr = jnp.exp(n * lr) * jnp.cos(n * li)
    pi = jnp.exp(n * lr) * jnp.sin(n * li)
    abr = pr[:, :, :, None, :] * bbr - pi[:, :, :, None, :] * bbi
    abi = pr[:, :, :, None, :] * bbi + pi[:, :, :, None, :] * bbr
    kk = jnp.sum(c_re[None, :, :, :, None, :] * abr[:, :, :, None, :, :]
                 - c_im[None, :, :, :, None, :] * abi[:, :, :, None, :, :], axis=-1)
    tt = np.arange(S5_CHUNK)
    nq = S5_GB // 2
    dsk = f(d_skip).reshape(S5_G, S5_H)
    k0 = kk[0, 0] + kk[0, 1] + dsk[:, :, None] * jnp.asarray(np.eye(S5_H, dtype=np.float32))[None]
    kcomb = jnp.concatenate([kk[1:S5_CHUNK, 1][::-1], k0[None], kk[1:S5_CHUNK, 0]], axis=0)
    kcomb = kcomb.reshape(2 * S5_CHUNK - 1, S5_JB, S5_GB, S5_H, S5_H).transpose(1, 0, 4, 2, 3)
    kcomb = kcomb.reshape(S5_JB, 2 * S5_CHUNK - 1, S5_H, LANE)
    lag = tt[None, :] - tt[:, None]
    y = kcomb[:, lag + (S5_CHUNK - 1)]
    y = y.transpose(0, 1, 3, 2, 4).reshape(S5_JB, S5_CHUNK, S5_H, S5_CW)
    out_group = (np.arange(S5_CW) // S5_H) % S5_GB
    row_group = np.arange(S5_GB)
    mmask = row_group[:, None, None] == out_group[None, None, :]
    m = jnp.where(mmask[None, None], y[:, :, None], 0.0).reshape(S5_JB, S5_CW, S5_CW)

    sw = 2 * S5_GB * 2 * S5_P
    ab = jnp.stack([abr, abi], axis=0)
    spow = (S5_CHUNK - 1 - tt, tt)
    sel = jnp.stack([ab[:, spow[d], d] for d in range(2)], axis=0)
    sel = sel.reshape(2, 2, S5_CHUNK, S5_JB, nq, 2, S5_H, S5_P)
    zs = sel.transpose(3, 2, 6, 0, 4, 1, 5, 7).reshape(S5_JB, S5_CHUNK, S5_H, sw)
    scol = np.arange(sw)
    state_group = ((scol // (2 * 2 * S5_P)) % nq) * 2 + (scol // S5_P) % 2
    smask = row_group[:, None, None] == state_group[None, None, :]
    s = jnp.where(smask[None, None], zs[:, :, None], 0.0).reshape(S5_JB, S5_CW, sw)

    def lanes_gh(c):
        return c.reshape(2, S5_JB, S5_GB, S5_H, S5_P).transpose(0, 1, 4, 2, 3).reshape(2, S5_JB, S5_P, 1, LANE)

    def pow_gh(a):
        e = jnp.stack([a[tt + 1, 0], a[S5_CHUNK - tt, 1]], axis=0)
        e = e.reshape(2, S5_CHUNK, S5_JB, S5_GB, S5_P).transpose(0, 2, 4, 1, 3)
        return jnp.broadcast_to(e[..., None], e.shape + (S5_H,)).reshape(2, S5_JB, S5_P, S5_CHUNK, LANE)

    cr, ci, er, ei = lanes_gh(c_re), lanes_gh(c_im), pow_gh(pr), pow_gh(pi)
    w = jnp.stack([cr * er - ci * ei, -(cr * ei + ci * er)], axis=2)
    w = w.reshape(2, S5_JB, 2, S5_P, S5_CW)
    omask = (np.arange(nq)[:, None, None] * 2 + np.arange(2)[None, :, None]) == out_group[None, None, :]
    o = jnp.where(omask[None, None, :, None, :, None, :], w[:, :, None, :, None, :, :], 0.0)
    o = o.reshape(2, S5_JB, S5_GB * 2 * S5_P, S5_CW)

    nn = jnp.arange(SCAN_ROWS + 1, dtype=F32)[:, None, None, None] * S5_CHUNK
    cw = nq * 2 * S5_P
    pw = jnp.stack([(jnp.exp(nn * lr) * jnp.cos(nn * li)).reshape(SCAN_ROWS + 1, 2, S5_JB, cw),
                    (jnp.exp(nn * lr) * jnp.sin(nn * li)).reshape(SCAN_ROWS + 1, 2, S5_JB, cw)], axis=2)
    rows = np.arange(SCAN_ROWS)
    bcast = lambda a: jnp.broadcast_to(a[:, :, None], (2, 2, SCAN_ROWS, S5_JB, cw))
    rowpow = jnp.stack([pw[rows, 0], pw[SCAN_ROWS - 1 - rows, 1]], axis=0).transpose(0, 2, 1, 3, 4)
    tab = jnp.stack([bcast(pw[k]) for k in SCAN_STEPS] + [rowpow, bcast(pw[SCAN_ROWS])], axis=0)
    tab = tab.transpose(4, 1, 0, 2, 3, 5)
    return (m.astype(BF16), s.astype(BF16), o[0].astype(BF16), o[1].astype(BF16), tab)


def _chunk_rows(u_ref):
    rbs = u_ref.shape[0] // S5_CHUNK
    return jnp.concatenate([u_ref[pl.ds(t, rbs, stride=S5_CHUNK), :] for t in range(S5_CHUNK)], axis=-1).astype(BF16)


def _s5_state_body(u_ref, s_ref, o_ref):
    o_ref[...] = _dot(_chunk_rows(u_ref), s_ref[0])


def _s5_scan_body(sf_ref, sb_ref, tab_ref, xf_ref, xb_ref, carry):
    @pl.when(pl.program_id(1) == 0)
    def _():
        carry[...] = jnp.zeros_like(carry)

    ntile = sf_ref.shape[0] // SCAN_ROWS
    nq = sf_ref.shape[1] // (2 * LANE)
    row = lax.broadcasted_iota(jnp.int32, (SCAN_ROWS, LANE), 0)

    def shift(v, k, d):
        if d == 0:
            return jnp.where(row >= k, pltpu.roll(v, k, axis=0), 0.0)
        return jnp.where(row < SCAN_ROWS - k, pltpu.roll(v, SCAN_ROWS - k, axis=0), 0.0)

    def cmul(d, kind, q, xr, xi):
        a_r = tab_ref[0, d, kind, 0, :, q * LANE:(q + 1) * LANE]
        a_i = tab_ref[0, d, kind, 1, :, q * LANE:(q + 1) * LANE]
        return a_r * xr - a_i * xi, a_r * xi + a_i * xr

    def tile(d, s_ref, x_ref, r0, cr):
        last = SCAN_ROWS - 1 if d == 0 else 0
        new = []
        for q in range(nq):
            re = slice(q * 2 * LANE, q * 2 * LANE + LANE)
            im = slice(q * 2 * LANE + LANE, (q + 1) * 2 * LANE)
            yr, yi = s_ref[pl.ds(r0, SCAN_ROWS), re], s_ref[pl.ds(r0, SCAN_ROWS), im]
            for kind, k in enumerate(SCAN_STEPS):
                tr, ti = cmul(d, kind, q, shift(yr, k, d), shift(yi, k, d))
                yr, yi = yr + tr, yi + ti
            c_r, c_i = cr[2 * q], cr[2 * q + 1]
            er, ei = cmul(d, SCAN_ROWPOW, q, c_r, c_i)
            x_ref[pl.ds(r0, SCAN_ROWS), re] = shift(yr, 1, d) + er
            x_ref[pl.ds(r0, SCAN_ROWS), im] = shift(yi, 1, d) + ei
            nr, ni = cmul(d, SCAN_TILEPOW, q, c_r, c_i)
            new.append(jnp.broadcast_to(yr[last:last + 1], (SCAN_ROWS, LANE)) + nr)
            new.append(jnp.broadcast_to(yi[last:last + 1], (SCAN_ROWS, LANE)) + ni)
        return tuple(new)

    def step(t, cs):
        cf, cb = cs
        rf = pl.multiple_of(t * SCAN_ROWS, SCAN_ROWS)
        rb = pl.multiple_of((ntile - 1 - t) * SCAN_ROWS, SCAN_ROWS)
        return tile(0, sf_ref, xf_ref, rf, cf), tile(1, sb_ref, xb_ref, rb, cb)

    init = tuple(tuple(carry[d, :, t * LANE:(t + 1) * LANE] for t in range(2 * nq)) for d in range(2))
    cf, cb = lax.fori_loop(0, ntile, step, init)
    for d, cs in enumerate((cf, cb)):
        for t in range(2 * nq):
            carry[d, :, t * LANE:(t + 1) * LANE] = cs[t]


def _s5_out_body(u_ref, xf_ref, xb_ref, m_ref, of_ref, ob_ref, y_ref):
    y = (_dot(_chunk_rows(u_ref), m_ref[0])
         + _dot(xf_ref[...].astype(BF16), of_ref[0])
         + _dot(xb_ref[...].astype(BF16), ob_ref[0]))
    y = jax.nn.gelu(y)
    for t in range(S5_CHUNK):
        y_ref[pl.ds(t, y.shape[0], stride=S5_CHUNK), :] = y[:, t * LANE:(t + 1) * LANE]


def _s5_mixer(z, bn, seq, tabs):
    m, s, of, ob, scan_tab = tabs
    nc = seq // S5_CHUNK
    rbs = min(nc, 256)
    nrb = nc // rbs
    sw = s.shape[-1]
    half = sw // 2
    u_spec = pl.BlockSpec((rbs * S5_CHUNK, LANE), lambda j, b, r: (b * nrb + r, j))
    st = pl.pallas_call(
        _s5_state_body,
        grid=(S5_JB, bn, nrb),
        in_specs=[u_spec, pl.BlockSpec((1, S5_CW, sw), lambda j, b, r: (j, 0, 0))],
        out_specs=pl.BlockSpec((rbs, sw), lambda j, b, r: (r, b * S5_JB + j)),
        out_shape=jax.ShapeDtypeStruct((nc, bn * S5_JB * sw), F32),
        compiler_params=_cparams(("parallel", "parallel", "parallel")),
        name="s5_state",
    )(z, s)

    nseq = bn * S5_JB
    sbs = min(nc, 256)
    nsb = nc // sbs
    xf, xb = pl.pallas_call(
        _s5_scan_body,
        grid=(nseq, nsb),
        in_specs=[
            pl.BlockSpec((sbs, half), lambda q, i: (i, 2 * q)),
            pl.BlockSpec((sbs, half), lambda q, i: (nsb - 1 - i, 2 * q + 1)),
            pl.BlockSpec((1,) + scan_tab.shape[1:], lambda q, i: (q % S5_JB, 0, 0, 0, 0, 0)),
        ],
        out_specs=[
            pl.BlockSpec((sbs, half), lambda q, i: (i, q)),
            pl.BlockSpec((sbs, half), lambda q, i: (nsb - 1 - i, q)),
        ],
        out_shape=[jax.ShapeDtypeStruct((nc, nseq * half), F32)] * 2,
        scratch_shapes=[pltpu.VMEM((2, SCAN_ROWS, half), F32)],
        compiler_params=_cparams(("parallel", "arbitrary")),
        name="s5_scan",
    )(st, st, scan_tab)

    x_spec = pl.BlockSpec((rbs, half), lambda j, b, r: (r, b * S5_JB + j))
    y = pl.pallas_call(
        _s5_out_body,
        grid=(S5_JB, bn, nrb),
        in_specs=[
            u_spec, x_spec, x_spec,
            pl.BlockSpec((1, S5_CW, S5_CW), lambda j, b, r: (j, 0, 0)),
            pl.BlockSpec((1, half, S5_CW), lambda j, b, r: (j, 0, 0)),
            pl.BlockSpec((1, half, S5_CW), lambda j, b, r: (j, 0, 0)),
        ],
        out_specs=u_spec,
        out_shape=jax.ShapeDtypeStruct((bn * seq, BW), F32),
        compiler_params=_cparams(("parallel", "parallel", "parallel")),
        name="s5_out",
    )(z, xf, xb, m, of, ob)
    return y


def _conv_body(b_ref, c_ref, v_ref, cp_ref, vp_ref, cn_ref, vn_ref, w_ref, o_ref):
    i = pl.program_id(1)
    tc = c_ref.shape[1]
    z = c_ref[0] * v_ref[0]
    zp = cp_ref[0][7:8, :] * vp_ref[0][7:8, :]
    zn = cn_ref[0][0:1, :] * vn_ref[0][0:1, :]
    zp = jnp.where(i == 0, 0.0, zp)
    zn = jnp.where(i == pl.num_programs(1) - 1, 0.0, zn)
    row = lax.broadcasted_iota(jnp.int32, z.shape, 0)
    up = jnp.where(row == 0, zp, pltpu.roll(z, 1, axis=0))
    dn = jnp.where(row == tc - 1, zn, pltpu.roll(z, tc - 1, axis=0))
    w = w_ref[...]
    y = w[0:1, :] * up + w[1:2, :] * z + w[2:3, :] * dn
    o_ref[0] = (b_ref[0] * y).astype(o_ref.dtype)


def _conv_mixer(z, bn, seq, conv_w, tc):
    z3 = z.reshape(bn, seq, MIX_IN)
    nb8 = tc // 8
    last8 = seq // 8 - 1
    main = lambda col: pl.BlockSpec((1, tc, BW), lambda b, i: (b, i, col))
    prev = lambda col: pl.BlockSpec((1, 8, BW), lambda b, i: (b, jnp.maximum(i * nb8 - 1, 0), col))
    nxt = lambda col: pl.BlockSpec((1, 8, BW), lambda b, i: (b, jnp.minimum((i + 1) * nb8, last8), col))
    y = pl.pallas_call(
        _conv_body,
        grid=(bn, seq // tc),
        in_specs=[main(1), main(2), main(3), prev(2), prev(3), nxt(2), nxt(3),
                  pl.BlockSpec((3, BW), lambda b, i: (0, 0))],
        out_specs=pl.BlockSpec((1, tc, BW), lambda b, i: (b, i, 0)),
        out_shape=jax.ShapeDtypeStruct((bn, seq, BW), BF16),
        compiler_params=_cparams(("parallel", "parallel")),
        name="short_conv",
    )(z3, z3, z3, z3, z3, z3, z3, conv_w.astype(F32))
    return y.reshape(bn * seq, BW)


def _na_bias_table(rel_bias):
    qc = np.arange(GRID_W)[:, None]
    kc = np.arange(GRID_W)[None, :]
    cs = np.clip(qc - NA_COLS // 2, 0, GRID_W - NA_COLS)
    valid = (kc >= cs) & (kc < cs + NA_COLS)
    dc = np.clip(kc - qc + (NA_COLS - 1), 0, 2 * NA_COLS - 2)
    pick = (dc[None] == np.arange(2 * NA_COLS - 1)[:, None, None]).astype(np.float32)
    b = jnp.einsum('hrd,dqk->hrqk', rel_bias.astype(F32), jnp.asarray(pick), precision=lax.Precision.HIGHEST)
    b = jnp.where(valid[None, None], b, NEG_INF)
    ndr = 2 * NA_ROWS - 1
    bt = b.transpose(0, 2, 1, 3).reshape(N_HEADS, GRID_W, ndr * GRID_W)
    tab = jnp.stack([bt[:, :, s * GRID_W:(s + NA_ROWS) * GRID_W] for s in range(NA_ROWS)], axis=0)
    return tab.reshape(NA_ROWS, N_HEADS // 2, 2 * GRID_W, NA_ROWS * GRID_W)


def _na_body(q_ref, kp_ref, kc_ref, kn_ref, vp_ref, vc_ref, vn_ref, bias_ref, o_ref, k_scr, v_scr, *, rows):
    i = pl.program_id(1)
    blk = NA_ROWS * GRID_W
    for n, (kr, vr) in enumerate(((kp_ref, vp_ref), (kc_ref, vc_ref), (kn_ref, vn_ref))):
        k_scr[n * blk:(n + 1) * blk, :] = kr[0]
        v_scr[n * blk:(n + 1) * blk, :] = vr[0]
    first_head = lax.broadcasted_iota(jnp.int32, (GRID_W, 2 * HEAD_DIM), 1) < HEAD_DIM
    for ir in range(NA_ROWS):
        r = i * NA_ROWS + ir
        rs = jnp.clip(r - NA_ROWS // 2, 0, rows - NA_ROWS)
        off = pl.multiple_of((rs - (i - 1) * NA_ROWS) * GRID_W, GRID_W)
        bias_row = rs - r + (NA_ROWS - 1)
        pair_lanes = [slice(hp * 2 * HEAD_DIM, (hp + 1) * 2 * HEAD_DIM) for hp in range(N_HEADS // 2)]
        scores = []
        for hp, lanes in enumerate(pair_lanes):
            q = q_ref[0, ir * GRID_W:(ir + 1) * GRID_W, lanes]
            zero = jnp.zeros_like(q)
            q2 = jnp.concatenate([jnp.where(first_head, q, zero), jnp.where(first_head, zero, q)], axis=0)
            kw = k_scr[pl.ds(off, blk), lanes]
            s = lax.dot_general(q2, kw, (((1,), (1,)), ((), ())), preferred_element_type=F32)
            scores.append(s + bias_ref[bias_row, hp])
        probs = []
        for s in scores:
            e = jnp.exp(s - jnp.max(s, axis=-1, keepdims=True))
            probs.append((e.astype(BF16), jnp.sum(e, axis=-1, keepdims=True)))
        for lanes, (e, l) in zip(pair_lanes, probs):
            o = _dot(e, v_scr[pl.ds(off, blk), lanes]) / l
            o_ref[0, ir * GRID_W:(ir + 1) * GRID_W, lanes] = jnp.where(
                first_head, o[:GRID_W], o[GRID_W:]).astype(o_ref.dtype)


def _na_mixer(qn, kn, vb, bn, seq, bias_tab):
    rows = seq // GRID_W
    nblk = rows // NA_ROWS
    assert rows % NA_ROWS == 0 and nblk >= 2
    blk = NA_ROWS * GRID_W
    q3, k3, v3 = (a.reshape(bn, seq, BW) for a in (qn, kn, vb))
    cur = pl.BlockSpec((1, blk, BW), lambda b, i: (b, i, 0))
    prev = pl.BlockSpec((1, blk, BW), lambda b, i: (b, jnp.maximum(i - 1, 0), 0))
    nxt = pl.BlockSpec((1, blk, BW), lambda b, i: (b, jnp.minimum(i + 1, nblk - 1), 0))
    y = pl.pallas_call(
        functools.partial(_na_body, rows=rows),
        grid=(bn, nblk),
        in_specs=[cur, prev, cur, nxt, prev, cur, nxt,
                  pl.BlockSpec((NA_ROWS, N_HEADS // 2, 2 * GRID_W, blk), lambda b, i: (0, 0, 0, 0))],
        out_specs=pl.BlockSpec((1, blk, BW), lambda b, i: (b, i, 0)),
        out_shape=jax.ShapeDtypeStruct((bn, seq, BW), BF16),
        scratch_shapes=[pltpu.VMEM((3 * blk, BW), BF16), pltpu.VMEM((3 * blk, BW), BF16)],
        compiler_params=_cparams(("parallel", "arbitrary")),
        name="nbr_attention",
    )(q3, k3, k3, k3, v3, v3, v3, bias_tab)
    return y.reshape(bn * seq, BW)


def _fnet_factors(seq):
    n1 = 1 << ((seq.bit_length() - 1 + 1) // 2)
    assert seq == n1 * (seq // n1) and seq & (seq - 1) == 0
    return n1, seq // n1


@functools.lru_cache(maxsize=None)
def _fnet_tables(seq):
    n1, n2 = _fnet_factors(seq)

    def cs(n, scale):
        idx = np.outer(np.arange(n), np.arange(n)) % n
        ang = 2.0 * np.pi * idx / n
        return np.cos(ang) * scale, np.sin(ang) * scale

    cc, sc = cs(FNET_GW, FNET_GW ** -0.5)
    chan = np.concatenate([cc, sc], axis=1)
    c1, s1 = cs(n1, n1 ** -0.5)
    d1 = np.block([[c1, -s1], [s1, c1]])
    c2, s2 = cs(n2, n2 ** -0.5)
    d2 = np.concatenate([c2, -s2], axis=1)
    ang = 2.0 * np.pi * (np.outer(np.arange(n2), np.arange(n1)) % seq) / seq
    twr = np.broadcast_to(np.cos(ang)[:, :, None], (n2, n1, LANE))
    twi = np.broadcast_to(np.sin(ang)[:, :, None], (n2, n1, LANE))
    return (np.asarray(chan, np.float32), np.asarray(d1, np.float32), np.asarray(d2, np.float32),
            np.ascontiguousarray(twr, np.float32), np.ascontiguousarray(twi, np.float32))


FNET_NG = BW // FNET_GW


def _fnet1_body(x0_ref, x1_ref, x2_ref, x3_ref, chan_ref, d1_ref, twr_ref, twi_ref, o_ref):
    n1, tb = x0_ref.shape[1], x0_ref.shape[2]
    xs = [r.reshape(n1 * tb, FNET_GW) for r in (x0_ref, x1_ref, x2_ref, x3_ref)]
    o2 = o_ref.reshape(2 * FNET_NG * n1 * tb, FNET_GW)
    for t in range(tb):
        pq = [_dot(x[pl.ds(t, n1, stride=tb), :].astype(BF16), chan_ref[...]) for x in xs]
        p = jnp.concatenate([a[:, :FNET_GW] for a in pq], axis=1)
        q = jnp.concatenate([a[:, FNET_GW:] for a in pq], axis=1)
        a = _dot(d1_ref[...], jnp.concatenate([p, q], axis=0).astype(BF16))
        ar, ai = a[:n1], a[n1:]
        twr = jnp.concatenate([twr_ref[t]] * FNET_NG, axis=1)
        twi = jnp.concatenate([twi_ref[t]] * FNET_NG, axis=1)
        for ri, v in enumerate((ar * twr - ai * twi, ar * twi + ai * twr)):
            for g in range(FNET_NG):
                o2[pl.ds((ri * FNET_NG + g) * n1 * tb + t, n1, stride=tb), :] = v[:, g * FNET_GW:(g + 1) * FNET_GW]


def _fnet2_body(a_ref, d2_ref, o_ref):
    kb, n2 = a_ref.shape[3], a_ref.shape[4]
    o2 = o_ref.reshape(FNET_NG * n2 * kb, FNET_GW)
    for k in range(kb):
        a = jnp.concatenate([jnp.concatenate([a_ref[0, ri, g, k] for g in range(FNET_NG)], axis=1)
                             for ri in range(2)], axis=0).astype(BF16)
        y = _dot(d2_ref[...], a)
        for g in range(FNET_NG):
            o2[pl.ds(g * n2 * kb + k, n2, stride=kb), :] = y[:, g * FNET_GW:(g + 1) * FNET_GW]


def _fnet_mixer(z, bn, seq):
    n1, n2 = _fnet_factors(seq)
    chan, d1, d2, twr, twi = _fnet_tables(seq)
    tb = 8
    zf = z.reshape(bn, n1, n2, MIX_IN)
    first_group = (MIX_IN - BW) // FNET_GW
    a = pl.pallas_call(
        _fnet1_body,
        grid=(bn, n2 // tb),
        in_specs=[pl.BlockSpec((1, n1, tb, FNET_GW), lambda b, j, g=g: (b, 0, j, first_group + g))
                  for g in range(FNET_NG)] + [
            pl.BlockSpec((FNET_GW, 2 * FNET_GW), lambda b, j: (0, 0)),
            pl.BlockSpec((2 * n1, 2 * n1), lambda b, j: (0, 0)),
            pl.BlockSpec((tb, n1, LANE), lambda b, j: (j, 0, 0)),
            pl.BlockSpec((tb, n1, LANE), lambda b, j: (j, 0, 0)),
        ],
        out_specs=pl.BlockSpec((1, 2, FNET_NG, n1, tb, FNET_GW), lambda b, j: (b, 0, 0, 0, j, 0)),
        out_shape=jax.ShapeDtypeStruct((bn, 2, FNET_NG, n1, n2, FNET_GW), F32),
        compiler_params=_cparams(("parallel", "parallel")),
        name="fnet_stage1",
    )(*([zf] * FNET_NG), jnp.asarray(chan, BF16), jnp.asarray(d1, BF16), jnp.asarray(twr), jnp.asarray(twi))

    kb = 8
    y = pl.pallas_call(
        _fnet2_body,
        grid=(bn, n1 // kb),
        in_specs=[
            pl.BlockSpec((1, 2, FNET_NG, kb, n2, FNET_GW), lambda b, i: (b, 0, 0, i, 0, 0)),
            pl.BlockSpec((n2, 2 * n2), lambda b, i: (0, 0)),
        ],
        out_specs=pl.BlockSpec((FNET_NG, 1, n2, kb, FNET_GW), lambda b, i: (0, b, 0, i, 0)),
        out_shape=jax.ShapeDtypeStruct((FNET_NG, bn, n2, n1, FNET_GW), F32),
        compiler_params=_cparams(("parallel", "parallel")),
        name="fnet_stage2",
    )(a, jnp.asarray(d2, BF16))
    return y.reshape(FNET_NG, bn * seq, FNET_GW)


def _merge_body(x_ref, g_ref, ya_ref, yb_ref, yc_ref, yd_ref, wglu_ref, wg0_ref, wg1_ref, wg2_ref, wg3_ref,
                wbr_ref, wo_ref, o_ref, h_scr, ya_scr):
    @pl.when(pl.program_id(1) == 0)
    def _():
        x = x_ref[...]
        h_scr[...] = _rms(x, g_ref[...]).astype(BF16)
        ya = ya_ref[...]
        ya_scr[...] = (ya * jax.nn.sigmoid(_dot(ya.astype(BF16), wglu_ref[...]))).astype(BF16)
        o_ref[...] = x

    h = h_scr[...]
    yd = jnp.concatenate([yd_ref[g] for g in range(FNET_NG)], axis=1).astype(BF16)
    ys = (ya_scr[...], yb_ref[...], yc_ref[...], yd)
    merged = None
    for kb, wg_ref in enumerate((wg0_ref, wg1_ref, wg2_ref, wg3_ref)):
        term = jax.nn.sigmoid(_dot(h, wg_ref[...])) * _dot(ys[kb], wbr_ref[kb])
        merged = term if merged is None else merged + term
    o_ref[...] += _dot(merged.astype(BF16), wo_ref[...])


def _merge(x2d, g, ya, yb, yc, yd, w_glu, w_in_bf, w_br, w_o, tm, tn):
    t = x2d.shape[0]
    row = lambda w: pl.BlockSpec((tm, w), lambda i, n: (i, 0))
    gate = lambda kb: pl.BlockSpec((D_MODEL, tn), lambda i, n: (0, (MIX_IN + kb * D_MODEL) // tn + n))
    return pl.pallas_call(
        _merge_body,
        grid=(t // tm, D_MODEL // tn),
        in_specs=[
            row(D_MODEL), pl.BlockSpec((1, D_MODEL), lambda i, n: (0, 0)),
            row(BW), row(BW), row(BW), pl.BlockSpec((FNET_NG, tm, FNET_GW), lambda i, n: (0, i, 0)),
            pl.BlockSpec((BW, BW), lambda i, n: (0, 0)),
            gate(0), gate(1), gate(2), gate(3),
            pl.BlockSpec((N_BRANCH, BW, tn), lambda i, n: (0, 0, n)),
            pl.BlockSpec((tn, D_MODEL), lambda i, n: (n, 0)),
        ],
        out_specs=row(D_MODEL),
        out_shape=jax.ShapeDtypeStruct((t, D_MODEL), F32),
        scratch_shapes=[pltpu.VMEM((tm, D_MODEL), BF16), pltpu.VMEM((tm, BW), BF16)],
        compiler_params=_cparams(("parallel", "arbitrary")),
        name="gated_merge",
    )(x2d, g, ya, yb, yc, yd, w_glu, w_in_bf, w_in_bf, w_in_bf, w_in_bf, w_br, w_o)


def _ffn_ple_body(x_ref, gf_ref, wa_ref, wb_ref, wo_ref, gp_ref, p_ref, wg_ref, wp_ref, o_ref, h_scr, *, nf, tn):
    f = pl.program_id(1)

    @pl.when(f == 0)
    def _():
        x = x_ref[...]
        h_scr[...] = _rms(x, gf_ref[...]).astype(BF16)
        o_ref[...] = x

    @pl.when(f < nf)
    def _():
        h = h_scr[...]
        a = _dot(h, wa_ref[...])
        b = _dot(h, wb_ref[...])
        o_ref[...] += _dot((jax.nn.silu(a) * b).astype(BF16), wo_ref[...])

    @pl.when(f == nf)
    def _():
        h_scr[...] = _rms(o_ref[...], gp_ref[...]).astype(BF16)
        p = p_ref[...].astype(BF16)
        for n in range(D_MODEL // tn):
            cols = slice(n * tn, (n + 1) * tn)
            pg = jax.nn.sigmoid(_dot(h_scr[...], wg_ref[:, cols]))
            o_ref[:, cols] += pg * _dot(p, wp_ref[:, cols])


def _ffn_ple(x2d, g_ffn, w_in, w_out, g_ple, p2d, w_gate, w_proj, tm, tf, tn):
    t = x2d.shape[0]
    nf = D_FF // tf
    ffn_step = lambda f: jnp.minimum(f, nf - 1)
    return pl.pallas_call(
        functools.partial(_ffn_ple_body, nf=nf, tn=tn),
        grid=(t // tm, nf + 1),
        in_specs=[
            pl.BlockSpec((tm, D_MODEL), lambda i, f: (i, 0)),
            pl.BlockSpec((1, D_MODEL), lambda i, f: (0, 0)),
            pl.BlockSpec((D_MODEL, tf), lambda i, f: (0, ffn_step(f))),
            pl.BlockSpec((D_MODEL, tf), lambda i, f: (0, nf + ffn_step(f))),
            pl.BlockSpec((tf, D_MODEL), lambda i, f: (ffn_step(f), 0)),
            pl.BlockSpec((1, D_MODEL), lambda i, f: (0, 0)),
            pl.BlockSpec((tm, PLE_DIM), lambda i, f: (i, 0)),
            pl.BlockSpec((D_MODEL, D_MODEL), lambda i, f: (0, 0)),
            pl.BlockSpec((PLE_DIM, D_MODEL), lambda i, f: (0, 0)),
        ],
        out_specs=pl.BlockSpec((tm, D_MODEL), lambda i, f: (i, 0)),
        out_shape=jax.ShapeDtypeStruct((t, D_MODEL), F32),
        scratch_shapes=[pltpu.VMEM((tm, D_MODEL), BF16)],
        compiler_params=_cparams(("parallel", "arbitrary")),
        name="swiglu_ffn_ple",
    )(x2d, g_ffn, w_in, w_in, w_out, g_ple, p2d, w_gate, w_proj)


def _tile(t, want):
    return min(t, want)


def _layer(x2d, p2d, bn, seq, lw):
    t = bn * seq
    z, qn, kn, vb = _inproj(x2d, lw['g_mix'], lw['w_in'], lw['q_gain'], lw['k_gain'], _tile(t, 1024))
    ya = _s5_mixer(z, bn, seq, lw['s5'])
    yb = _conv_mixer(z, bn, seq, lw['conv_w'], _tile(seq, 1024))
    yc = _na_mixer(qn, kn, vb, bn, seq, lw['na_bias'])
    yd = _fnet_mixer(z, bn, seq)
    x2d = _merge(x2d, lw['g_mix'], ya, yb, yc, yd, lw['w_glu'], lw['w_in'], lw['w_br'], lw['w_o'],
                 _tile(t, 512), 512)
    return _ffn_ple(x2d, lw['g_ffn'], lw['w_ffn_in'], lw['w_ffn_out'], lw['g_ple'], p2d, lw['w_ple_gate'],
                    lw['w_ple_proj'], _tile(t, 512), 512, 512)


def kernel(x_prompt, x_sample, p_prompt, p_sample, g_mix, w_in, s5_lam_re, s5_lam_im, s5_log_dt, s5_b_re,
           s5_b_im, s5_c_re, s5_c_im, s5_d, w_glu, conv_w, q_gain, k_gain, rel_bias, w_br, w_o, g_ffn,
           w_ffn_in, w_ffn_out, g_ple, w_ple_gate, w_ple_proj):
    depth = w_in.shape[0]
    layers = []
    for i in range(depth):
        layers.append(dict(
            g_mix=g_mix[i].astype(F32)[None], w_in=w_in[i].astype(BF16),
            s5=_s5_tables(s5_lam_re[i], s5_lam_im[i], s5_log_dt[i], s5_b_re[i], s5_b_im[i], s5_c_re[i],
                          s5_c_im[i], s5_d[i]),
            w_glu=w_glu[i].astype(BF16), conv_w=conv_w[i], q_gain=q_gain[i], k_gain=k_gain[i],
            na_bias=_na_bias_table(rel_bias[i]), w_br=w_br[i].astype(BF16), w_o=w_o[i].astype(BF16),
            g_ffn=g_ffn[i].astype(F32)[None], w_ffn_in=w_ffn_in[i].astype(BF16),
            w_ffn_out=w_ffn_out[i].astype(BF16), g_ple=g_ple[i].astype(F32)[None],
            w_ple_gate=w_ple_gate[i].astype(BF16), w_ple_proj=w_ple_proj[i].astype(BF16)))

    def trunk(x, p):
        bn, seq, _ = x.shape
        x2d = x.reshape(bn * seq, D_MODEL)
        for i in range(depth):
            x2d = _layer(x2d, p[i].reshape(bn * seq, PLE_DIM), bn, seq, layers[i])
        return x2d.reshape(bn, seq, D_MODEL)

    return trunk(x_prompt, p_prompt), trunk(x_sample, p_sample)
```

```python
import functools
import math

import numpy as np
import jax
import jax.numpy as jnp
from jax import lax
from jax.experimental import pallas as pl
from jax.experimental.pallas import tpu as pltpu

F32 = jnp.float32
BF16 = jnp.bfloat16

D_MODEL = 2048
BW = 512
MIX_IN = 8 * BW
N_BRANCH = 4
S5_H = 16
S5_G = BW // S5_H
S5_P = 64
S5_CHUNK = 16
LANE = 128
S5_JB = BW // LANE
S5_GB = LANE // S5_H
S5_CW = S5_CHUNK * LANE
SCAN_ROWS = 8
SCAN_STEPS = (1, 2, 4)
SCAN_ROWPOW = len(SCAN_STEPS)
SCAN_TILEPOW = SCAN_ROWPOW + 1
N_HEADS = 8
HEAD_DIM = 64
GRID_W = 64
NA_ROWS = 8
NA_COLS = 16
FNET_GW = 128
D_FF = 5632
PLE_DIM = 256
EPS = 1e-6
NEG_INF = -1e30
VMEM_LIMIT = 56 * 1024 * 1024


def _cparams(sem):
    return pltpu.CompilerParams(dimension_semantics=sem, vmem_limit_bytes=VMEM_LIMIT)


def _rms(x, g):
    ms = jnp.mean(x * x, axis=-1, keepdims=True)
    return x * lax.rsqrt(ms + EPS) * g


def _dot(a, b):
    return jnp.dot(a, b, preferred_element_type=F32)


INPROJ_TN = 2 * BW
Z_COLS = MIX_IN - INPROJ_TN
QK_STEP = 2
V_STEP = 3


def _head_mean_sq(x, ones_ref):
    x2 = x * x
    hi = x2.astype(BF16)
    lo = (x2 - hi.astype(F32)).astype(BF16)
    return _dot(hi, ones_ref[...]) + _dot(lo, ones_ref[...])


def _inproj_body(x_ref, g_ref, w_ref, qg_ref, kg_ref, ones_ref, z_ref, q_ref, k_ref, v_ref, h_scr):
    j = pl.program_id(1)

    @pl.when(j == 0)
    def _():
        h_scr[...] = _rms(x_ref[...], g_ref[...]).astype(BF16)

    acc = _dot(h_scr[...], w_ref[...])

    @pl.when(j != QK_STEP)
    def _():
        z_ref[...] = acc

    @pl.when(j == QK_STEP)
    def _():
        q, k = acc[:, :BW], acc[:, BW:]
        qn = q * lax.rsqrt(_head_mean_sq(q, ones_ref) + EPS) * qg_ref[...]
        kn = k * lax.rsqrt(_head_mean_sq(k, ones_ref) + EPS) * kg_ref[...]
        q_ref[...] = (qn * (1.0 / math.sqrt(HEAD_DIM))).astype(BF16)
        k_ref[...] = kn.astype(BF16)

    @pl.when(j == V_STEP)
    def _():
        v_ref[...] = acc[:, :BW].astype(BF16)


def _inproj(x2d, g, w_bf, q_gain, k_gain, tm):
    t = x2d.shape[0]
    ones = np.kron(np.eye(N_HEADS, dtype=np.float32), np.full((HEAD_DIM, HEAD_DIM), 1.0 / HEAD_DIM, np.float32))
    vec = pl.BlockSpec((1, BW), lambda i, j: (0, 0))
    head = pl.BlockSpec((tm, BW), lambda i, j: (i, 0))
    return pl.pallas_call(
        _inproj_body,
        grid=(t // tm, MIX_IN // INPROJ_TN),
        in_specs=[
            pl.BlockSpec((tm, D_MODEL), lambda i, j: (i, 0)),
            pl.BlockSpec((1, D_MODEL), lambda i, j: (0, 0)),
            pl.BlockSpec((D_MODEL, INPROJ_TN), lambda i, j: (0, j)),
            vec, vec, pl.BlockSpec((BW, BW), lambda i, j: (0, 0)),
        ],
        out_specs=[pl.BlockSpec((tm, INPROJ_TN), lambda i, j: (i, jnp.where(j >= QK_STEP, j - 1, j))),
                   head, head, head],
        out_shape=[jax.ShapeDtypeStruct((t, Z_COLS), F32)] + [jax.ShapeDtypeStruct((t, BW), BF16)] * 3,
        scratch_shapes=[pltpu.VMEM((tm, D_MODEL), BF16)],
        compiler_params=_cparams(("parallel", "arbitrary")),
        name="inproj",
    )(x2d, g, w_bf, q_gain, k_gain, jnp.asarray(ones, BF16))


def _s5_tables(lam_re, lam_im, log_dt, b_re, b_im, c_re, c_im, d_skip):
    f = lambda a: a.astype(F32)
    lam_re, lam_im, b_re, b_im, c_re, c_im = map(f, (lam_re, lam_im, b_re, b_im, c_re, c_im))
    dt = jnp.exp(f(log_dt))[..., None]
    lr, li = lam_re * dt, lam_im * dt
    mag = jnp.exp(lr)
    ar, ai = mag * jnp.cos(li), mag * jnp.sin(li)
    den = lam_re * lam_re + lam_im * lam_im
    fr = ((ar - 1.0) * lam_re + ai * lam_im) / den
    fi = (ai * lam_re - (ar - 1.0) * lam_im) / den
    bbr = (fr[..., None] * b_re - fi[..., None] * b_im).transpose(0, 1, 3, 2)
    bbi = (fr[..., None] * b_im + fi[..., None] * b_re).transpose(0, 1, 3, 2)
    n = jnp.arange(S5_CHUNK + 1, dtype=F32)[:, None, None, None]
    pr = jnp.exp(n * lr) * jnp.cos(n * li)
    pi = jnp.exp(n * lr) * jnp.sin(n * li)
    abr = pr[:, :, :, None, :] * bbr - pi[:, :, :, None, :] * bbi
    abi = pr[:, :, :, None, :] * bbi + pi[:, :, :, None, :] * bbr
    kk = jnp.sum(c_re[None, :, :, :, None, :] * abr[:, :, :, None, :, :]
                 - c_im[None, :, :, :, None, :] * abi[:, :, :, None, :, :], axis=-1)
    tt = np.arange(S5_CHUNK)
    nq = S5_GB // 2
    dsk = f(d_skip).reshape(S5_G, S5_H)
    k0 = kk[0, 0] + kk[0, 1] + dsk[:, :, None] * jnp.asarray(np.eye(S5_H, dtype=np.float32))[None]
    kcomb = jnp.concatenate([kk[1:S5_CHUNK, 1][::-1], k0[None], kk[1:S5_CHUNK, 0]], axis=0)
    kcomb = kcomb.reshape(2 * S5_CHUNK - 1, S5_JB, S5_GB, S5_H, S5_H).transpose(1, 0, 4, 2, 3)
    kcomb = kcomb.reshape(S5_JB, 2 * S5_CHUNK - 1, S5_H, LANE)
    lag = tt[None, :] - tt[:, None]
    y = kcomb[:, lag + (S5_CHUNK - 1)]
    y = y.transpose(0, 1, 3, 2, 4).reshape(S5_JB, S5_CHUNK, S5_H, S5_CW)
    out_group = (np.arange(S5_CW) // S5_H) % S5_GB
    row_group = np.arange(S5_GB)
    mmask = row_group[:, None, None] == out_group[None, None, :]
    m = jnp.where(mmask[None, None], y[:, :, None], 0.0).reshape(S5_JB, S5_CW, S5_CW)

    sw = 2 * S5_GB * 2 * S5_P
    ab = jnp.stack([abr, abi], axis=0)
    spow = (S5_CHUNK - 1 - tt, tt)
    sel = jnp.stack([ab[:, spow[d], d] for d in range(2)], axis=0)
    sel = sel.reshape(2, 2, S5_CHUNK, S5_JB, nq, 2, S5_H, S5_P)
    zs = sel.transpose(3, 2, 6, 0, 4, 1, 5, 7).reshape(S5_JB, S5_CHUNK, S5_H, sw)
    scol = np.arange(sw)
    state_group = ((scol // (2 * 2 * S5_P)) % nq) * 2 + (scol // S5_P) % 2
    smask = row_group[:, None, None] == state_group[None, None, :]
    s = jnp.where(smask[None, None], zs[:, :, None], 0.0).reshape(S5_JB, S5_CW, sw)

    def lanes_gh(c):
        return c.reshape(2, S5_JB, S5_GB, S5_H, S5_P).transpose(0, 1, 4, 2, 3).reshape(2, S5_JB, S5_P, 1, LANE)

    def pow_gh(a):
        e = jnp.stack([a[tt + 1, 0], a[S5_CHUNK - tt, 1]], axis=0)
        e = e.reshape(2, S5_CHUNK, S5_JB, S5_GB, S5_P).transpose(0, 2, 4, 1, 3)
        return jnp.broadcast_to(e[..., None], e.shape + (S5_H,)).reshape(2, S5_JB, S5_P, S5_CHUNK, LANE)

    cr, ci, er, ei = lanes_gh(c_re), lanes_gh(c_im), pow_gh(pr), pow_gh(pi)
    w = jnp.stack([cr * er - ci * ei, -(cr * ei + ci * er)], axis=2)
    w = w.reshape(2, S5_JB, 2, S5_P, S5_CW)
    omask = (np.arange(nq)[:, None, None] * 2 + np.arange(2)[None, :, None]) == out_group[None, None, :]
    o = jnp.where(omask[None, None, :, None, :, None, :], w[:, :, None, :, None, :, :], 0.0)
    o = o.reshape(2, S5_JB, S5_GB * 2 * S5_P, S5_CW)

    nn = jnp.arange(SCAN_ROWS + 1, dtype=F32)[:, None, None, None] * S5_CHUNK
    cw = nq * 2 * S5_P
    pw = jnp.stack([(jnp.exp(nn * lr) * jnp.cos(nn * li)).reshape(SCAN_ROWS + 1, 2, S5_JB, cw),
                    (jnp.exp(nn * lr) * jnp.sin(nn * li)).reshape(SCAN_ROWS + 1, 2, S5_JB, cw)], axis=2)
    rows = np.arange(SCAN_ROWS)
    bcast = lambda a: jnp.broadcast_to(a[:, :, None], (2, 2, SCAN_ROWS, S5_JB, cw))
    rowpow = jnp.stack([pw[rows, 0], pw[SCAN_ROWS - 1 - rows, 1]], axis=0).transpose(0, 2, 1, 3, 4)
    tab = jnp.stack([bcast(pw[k]) for k in SCAN_STEPS] + [rowpow, bcast(pw[SCAN_ROWS])], axis=0)
    tab = tab.transpose(4, 1, 0, 2, 3, 5)
    return (m.astype(BF16), s.astype(BF16), o[0].astype(BF16), o[1].astype(BF16), tab)


def _chunk_rows(u_ref):
    rbs = u_ref.shape[0] // S5_CHUNK
    return jnp.concatenate([u_ref[pl.ds(t, rbs, stride=S5_CHUNK), :] for t in range(S5_CHUNK)], axis=-1).astype(BF16)


def _s5_state_body(u_ref, s_ref, o_ref):
    o_ref[...] = _dot(_chunk_rows(u_ref), s_ref[0])


def _s5_scan_body(sf_ref, sb_ref, tab_ref, xf_ref, xb_ref, carry):
    @pl.when(pl.program_id(1) == 0)
    def _():
        carry[...] = jnp.zeros_like(carry)

    ntile = sf_ref.shape[0] // SCAN_ROWS
    nq = sf_ref.shape[1] // (2 * LANE)
    row = lax.broadcasted_iota(jnp.int32, (SCAN_ROWS, LANE), 0)

    def shift(v, k, d):
        if d == 0:
            return jnp.where(row >= k, pltpu.roll(v, k, axis=0), 0.0)
        return jnp.where(row < SCAN_ROWS - k, pltpu.roll(v, SCAN_ROWS - k, axis=0), 0.0)

    def cmul(d, kind, q, xr, xi):
        a_r = tab_ref[0, d, kind, 0, :, q * LANE:(q + 1) * LANE]
        a_i = tab_ref[0, d, kind, 1, :, q * LANE:(q + 1) * LANE]
        return a_r * xr - a_i * xi, a_r * xi + a_i * xr

    def tile(d, s_ref, x_ref, r0, cr):
        last = SCAN_ROWS - 1 if d == 0 else 0
        new = []
        for q in range(nq):
            re = slice(q * 2 * LANE, q * 2 * LANE + LANE)
            im = slice(q * 2 * LANE + LANE, (q + 1) * 2 * LANE)
            yr, yi = s_ref[pl.ds(r0, SCAN_ROWS), re], s_ref[pl.ds(r0, SCAN_ROWS), im]
            for kind, k in enumerate(SCAN_STEPS):
                tr, ti = cmul(d, kind, q, shift(yr, k, d), shift(yi, k, d))
                yr, yi = yr + tr, yi + ti
            c_r, c_i = cr[2 * q], cr[2 * q + 1]
            er, ei = cmul(d, SCAN_ROWPOW, q, c_r, c_i)
            x_ref[pl.ds(r0, SCAN_ROWS), re] = shift(yr, 1, d) + er
            x_ref[pl.ds(r0, SCAN_ROWS), im] = shift(yi, 1, d) + ei
            nr, ni = cmul(d, SCAN_TILEPOW, q, c_r, c_i)
            new.append(jnp.broadcast_to(yr[last:last + 1], (SCAN_ROWS, LANE)) + nr)
            new.append(jnp.broadcast_to(yi[last:last + 1], (SCAN_ROWS, LANE)) + ni)
        return tuple(new)

    def step(t, cs):
        cf, cb = cs
        rf = pl.multiple_of(t * SCAN_ROWS, SCAN_ROWS)
        rb = pl.multiple_of((ntile - 1 - t) * SCAN_ROWS, SCAN_ROWS)
        return tile(0, sf_ref, xf_ref, rf, cf), tile(1, sb_ref, xb_ref, rb, cb)

    init = tuple(tuple(carry[d, :, t * LANE:(t + 1) * LANE] for t in range(2 * nq)) for d in range(2))
    cf, cb = lax.fori_loop(0, ntile, step, init)
    for d, cs in enumerate((cf, cb)):
        for t in range(2 * nq):
            carry[d, :, t * LANE:(t + 1) * LANE] = cs[t]


def _s5_out_body(u_ref, xf_ref, xb_ref, m_ref, of_ref, ob_ref, y_ref):
    y = (_dot(_chunk_rows(u_ref), m_ref[0])
         + _dot(xf_ref[...].astype(BF16), of_ref[0])
         + _dot(xb_ref[...].astype(BF16), ob_ref[0]))
    y = jax.nn.gelu(y)
    for t in range(S5_CHUNK):
        y_ref[pl.ds(t, y.shape[0], stride=S5_CHUNK), :] = y[:, t * LANE:(t + 1) * LANE]


def _s5_mixer(z, bn, seq, tabs, li):
    m, s, of, ob, scan_tab = tabs
    nc = seq // S5_CHUNK
    rbs = min(nc, 256)
    nrb = nc // rbs
    sw = s.shape[-1]
    half = sw // 2
    u_spec = pl.BlockSpec((rbs * S5_CHUNK, LANE), lambda j, b, r: (b * nrb + r, j))
    st = pl.pallas_call(
        _s5_state_body,
        grid=(S5_JB, bn, nrb),
        in_specs=[u_spec, pl.BlockSpec((None, 1, S5_CW, sw), lambda j, b, r: (li, j, 0, 0))],
        out_specs=pl.BlockSpec((rbs, sw), lambda j, b, r: (r, b * S5_JB + j)),
        out_shape=jax.ShapeDtypeStruct((nc, bn * S5_JB * sw), F32),
        compiler_params=_cparams(("parallel", "parallel", "parallel")),
        name="s5_state",
    )(z, s)

    nseq = bn * S5_JB
    sbs = min(nc, 256)
    nsb = nc // sbs
    xf, xb = pl.pallas_call(
        _s5_scan_body,
        grid=(nseq, nsb),
        in_specs=[
            pl.BlockSpec((sbs, half), lambda q, i: (i, 2 * q)),
            pl.BlockSpec((sbs, half), lambda q, i: (nsb - 1 - i, 2 * q + 1)),
            pl.BlockSpec((None, 1) + scan_tab.shape[2:], lambda q, i: (li, q % S5_JB, 0, 0, 0, 0, 0)),
        ],
        out_specs=[
            pl.BlockSpec((sbs, half), lambda q, i: (i, q)),
            pl.BlockSpec((sbs, half), lambda q, i: (nsb - 1 - i, q)),
        ],
        out_shape=[jax.ShapeDtypeStruct((nc, nseq * half), F32)] * 2,
        scratch_shapes=[pltpu.VMEM((2, SCAN_ROWS, half), F32)],
        compiler_params=_cparams(("parallel", "arbitrary")),
        name="s5_scan",
    )(st, st, scan_tab)

    x_spec = pl.BlockSpec((rbs, half), lambda j, b, r: (r, b * S5_JB + j))
    y = pl.pallas_call(
        _s5_out_body,
        grid=(S5_JB, bn, nrb),
        in_specs=[
            u_spec, x_spec, x_spec,
            pl.BlockSpec((None, 1, S5_CW, S5_CW), lambda j, b, r: (li, j, 0, 0)),
            pl.BlockSpec((None, 1, half, S5_CW), lambda j, b, r: (li, j, 0, 0)),
            pl.BlockSpec((None, 1, half, S5_CW), lambda j, b, r: (li, j, 0, 0)),
        ],
        out_specs=u_spec,
        out_shape=jax.ShapeDtypeStruct((bn * seq, BW), F32),
        compiler_params=_cparams(("parallel", "parallel", "parallel")),
        name="s5_out",
    )(z, xf, xb, m, of, ob)
    return y


def _conv_body(b_ref, c_ref, v_ref, cp_ref, vp_ref, cn_ref, vn_ref, w_ref, o_ref):
    i = pl.program_id(1)
    tc = c_ref.shape[1]
    z = c_ref[0] * v_ref[0]
    zp = cp_ref[0][7:8, :] * vp_ref[0][7:8, :]
    zn = cn_ref[0][0:1, :] * vn_ref[0][0:1, :]
    zp = jnp.where(i == 0, 0.0, zp)
    zn = jnp.where(i == pl.num_programs(1) - 1, 0.0, zn)
    row = lax.broadcasted_iota(jnp.int32, z.shape, 0)
    up = jnp.where(row == 0, zp, pltpu.roll(z, 1, axis=0))
    dn = jnp.where(row == tc - 1, zn, pltpu.roll(z, tc - 1, axis=0))
    w = w_ref[...]
    y = w[0:1, :] * up + w[1:2, :] * z + w[2:3, :] * dn
    o_ref[0] = (b_ref[0] * y).astype(o_ref.dtype)


def _conv_mixer(z, bn, seq, conv_w, tc):
    z3 = z.reshape(bn, seq, Z_COLS)
    nb8 = tc // 8
    last8 = seq // 8 - 1
    main = lambda col: pl.BlockSpec((1, tc, BW), lambda b, i: (b, i, col))
    prev = lambda col: pl.BlockSpec((1, 8, BW), lambda b, i: (b, jnp.maximum(i * nb8 - 1, 0), col))
    nxt = lambda col: pl.BlockSpec((1, 8, BW), lambda b, i: (b, jnp.minimum((i + 1) * nb8, last8), col))
    y = pl.pallas_call(
        _conv_body,
        grid=(bn, seq // tc),
        in_specs=[main(1), main(2), main(3), prev(2), prev(3), nxt(2), nxt(3),
                  pl.BlockSpec((3, BW), lambda b, i: (0, 0))],
        out_specs=pl.BlockSpec((1, tc, BW), lambda b, i: (b, i, 0)),
        out_shape=jax.ShapeDtypeStruct((bn, seq, BW), BF16),
        compiler_params=_cparams(("parallel", "parallel")),
        name="short_conv",
    )(z3, z3, z3, z3, z3, z3, z3, conv_w.astype(F32))
    return y.reshape(bn * seq, BW)


def _na_bias_table(rel_bias):
    qc = np.arange(GRID_W)[:, None]
    kc = np.arange(GRID_W)[None, :]
    cs = np.clip(qc - NA_COLS // 2, 0, GRID_W - NA_COLS)
    valid = (kc >= cs) & (kc < cs + NA_COLS)
    dc = np.clip(kc - qc + (NA_COLS - 1), 0, 2 * NA_COLS - 2)
    pick = (dc[None] == np.arange(2 * NA_COLS - 1)[:, None, None]).astype(np.float32)
    b = jnp.einsum('hrd,dqk->hrqk', rel_bias.astype(F32), jnp.asarray(pick), precision=lax.Precision.HIGHEST)
    b = jnp.where(valid[None, None], b, NEG_INF)
    bq = b.transpose(0, 2, 1, 3)
    ndr = 2 * NA_ROWS - 2
    bt = [bq[:, :, o:o + ndr].reshape(N_HEADS, GRID_W, ndr * GRID_W) for o in range(2)]
    tab = jnp.stack([bt[s % 2][:, :, (s - s % 2) * GRID_W:(s - s % 2 + NA_ROWS) * GRID_W]
                     for s in range(NA_ROWS)], axis=0)
    return tab.reshape(NA_ROWS, N_HEADS // 2, 2 * GRID_W, NA_ROWS * GRID_W)


def _na_body(q_ref, kp_ref, kc_ref, kn_ref, vp_ref, vc_ref, vn_ref, bias_ref, o_ref, k_scr, v_scr, *, rows):
    i = pl.program_id(1)
    blk = NA_ROWS * GRID_W
    for n, (kr, vr) in enumerate(((kp_ref, vp_ref), (kc_ref, vc_ref), (kn_ref, vn_ref))):
        k_scr[n * blk:(n + 1) * blk, :] = kr[0]
        v_scr[n * blk:(n + 1) * blk, :] = vr[0]
    first_head = lax.broadcasted_iota(jnp.int32, (GRID_W, 2 * HEAD_DIM), 1) < HEAD_DIM
    for ir in range(NA_ROWS):
        r = i * NA_ROWS + ir
        rs = jnp.clip(r - NA_ROWS // 2, 0, rows - NA_ROWS)
        off = pl.multiple_of((rs - (i - 1) * NA_ROWS) * GRID_W, GRID_W)
        bias_row = rs - r + (NA_ROWS - 1)
        pair_lanes = [slice(hp * 2 * HEAD_DIM, (hp + 1) * 2 * HEAD_DIM) for hp in range(N_HEADS // 2)]
        scores = []
        for hp, lanes in enumerate(pair_lanes):
            q = q_ref[0, ir * GRID_W:(ir + 1) * GRID_W, lanes]
            zero = jnp.zeros_like(q)
            q2 = jnp.concatenate([jnp.where(first_head, q, zero), jnp.where(first_head, zero, q)], axis=0)
            kw = k_scr[pl.ds(off, blk), lanes]
            s = lax.dot_general(q2, kw, (((1,), (1,)), ((), ())), preferred_element_type=F32)
            scores.append(s + bias_ref[bias_row, hp])
        probs = []
        for s in scores:
            e = jnp.exp(s - jnp.max(s, axis=-1, keepdims=True))
            probs.append((e.astype(BF16), jnp.sum(e, axis=-1, keepdims=True)))
        for lanes, (e, l) in zip(pair_lanes, probs):
            o = _dot(e, v_scr[pl.ds(off, blk), lanes]) / l
            o_ref[0, ir * GRID_W:(ir + 1) * GRID_W, lanes] = jnp.where(
                first_head, o[:GRID_W], o[GRID_W:]).astype(o_ref.dtype)


def _na_mixer(qn, kn, vb, bn, seq, bias_tab, li):
    rows = seq // GRID_W
    nblk = rows // NA_ROWS
    assert rows % NA_ROWS == 0 and nblk >= 2
    blk = NA_ROWS * GRID_W
    q3, k3, v3 = (a.reshape(bn, seq, BW) for a in (qn, kn, vb))
    cur = pl.BlockSpec((1, blk, BW), lambda b, i: (b, i, 0))
    prev = pl.BlockSpec((1, blk, BW), lambda b, i: (b, jnp.maximum(i - 1, 0), 0))
    nxt = pl.BlockSpec((1, blk, BW), lambda b, i: (b, jnp.minimum(i + 1, nblk - 1), 0))
    y = pl.pallas_call(
        functools.partial(_na_body, rows=rows),
        grid=(bn, nblk),
        in_specs=[cur, prev, cur, nxt, prev, cur, nxt,
                  pl.BlockSpec((None, NA_ROWS, N_HEADS // 2, 2 * GRID_W, blk), lambda b, i: (li, 0, 0, 0, 0))],
        out_specs=pl.BlockSpec((1, blk, BW), lambda b, i: (b, i, 0)),
        out_shape=jax.ShapeDtypeStruct((bn, seq, BW), BF16),
        scratch_shapes=[pltpu.VMEM((3 * blk, BW), BF16), pltpu.VMEM((3 * blk, BW), BF16)],
        compiler_params=_cparams(("parallel", "arbitrary")),
        name="nbr_attention",
    )(q3, k3, k3, k3, v3, v3, v3, bias_tab)
    return y.reshape(bn * seq, BW)


def _fnet_factors(seq):
    n1 = 1 << ((seq.bit_length() - 1 + 1) // 2)
    assert seq == n1 * (seq // n1) and seq & (seq - 1) == 0
    return n1, seq // n1


@functools.lru_cache(maxsize=None)
def _fnet_tables(seq):
    n1, n2 = _fnet_factors(seq)

    def cs(n, scale):
        idx = np.outer(np.arange(n), np.arange(n)) % n
        ang = 2.0 * np.pi * idx / n
        return np.cos(ang) * scale, np.sin(ang) * scale

    cc, sc = cs(FNET_GW, FNET_GW ** -0.5)
    chan = np.concatenate([cc, sc], axis=1)
    c1, s1 = cs(n1, n1 ** -0.5)
    d1 = np.block([[c1, -s1], [s1, c1]])
    c2, s2 = cs(n2, n2 ** -0.5)
    d2 = np.concatenate([c2, -s2], axis=1)
    ang = 2.0 * np.pi * (np.outer(np.arange(n2), np.arange(n1)) % seq) / seq
    twr = np.broadcast_to(np.cos(ang)[:, :, None], (n2, n1, LANE))
    twi = np.broadcast_to(np.sin(ang)[:, :, None], (n2, n1, LANE))
    return (np.asarray(chan, np.float32), np.asarray(d1, np.float32), np.asarray(d2, np.float32),
            np.ascontiguousarray(twr, np.float32), np.ascontiguousarray(twi, np.float32))


FNET_NG = BW // FNET_GW


def _fnet1_body(x0_ref, x1_ref, x2_ref, x3_ref, chan_ref, d1_ref, twr_ref, twi_ref, o_ref):
    n1, tb = x0_ref.shape[1], x0_ref.shape[2]
    xs = [r.reshape(n1 * tb, FNET_GW) for r in (x0_ref, x1_ref, x2_ref, x3_ref)]
    o2 = o_ref.reshape(2 * FNET_NG * n1 * tb, FNET_GW)
    for t in range(tb):
        pq = [_dot(x[pl.ds(t, n1, stride=tb), :].astype(BF16), chan_ref[...]) for x in xs]
        p = jnp.concatenate([a[:, :FNET_GW] for a in pq], axis=1)
        q = jnp.concatenate([a[:, FNET_GW:] for a in pq], axis=1)
        a = _dot(d1_ref[...], jnp.concatenate([p, q], axis=0).astype(BF16))
        ar, ai = a[:n1], a[n1:]
        twr = jnp.concatenate([twr_ref[t]] * FNET_NG, axis=1)
        twi = jnp.concatenate([twi_ref[t]] * FNET_NG, axis=1)
        for ri, v in enumerate((ar * twr - ai * twi, ar * twi + ai * twr)):
            for g in range(FNET_NG):
                o2[pl.ds((ri * FNET_NG + g) * n1 * tb + t, n1, stride=tb), :] = v[:, g * FNET_GW:(g + 1) * FNET_GW]


def _fnet2_body(a_ref, d2_ref, o_ref):
    kb, n2 = a_ref.shape[3], a_ref.shape[4]
    o2 = o_ref.reshape(FNET_NG * n2 * kb, FNET_GW)
    for k in range(kb):
        a = jnp.concatenate([jnp.concatenate([a_ref[0, ri, g, k] for g in range(FNET_NG)], axis=1)
                             for ri in range(2)], axis=0).astype(BF16)
        y = _dot(d2_ref[...], a)
        for g in range(FNET_NG):
            o2[pl.ds(g * n2 * kb + k, n2, stride=kb), :] = y[:, g * FNET_GW:(g + 1) * FNET_GW]


def _fnet_mixer(z, bn, seq):
    n1, n2 = _fnet_factors(seq)
    chan, d1, d2, twr, twi = _fnet_tables(seq)
    tb = 8
    zf = z.reshape(bn, n1, n2, Z_COLS)
    first_group = (Z_COLS - BW) // FNET_GW
    a = pl.pallas_call(
        _fnet1_body,
        grid=(bn, n2 // tb),
        in_specs=[pl.BlockSpec((1, n1, tb, FNET_GW), lambda b, j, g=g: (b, 0, j, first_group + g))
                  for g in range(FNET_NG)] + [
            pl.BlockSpec((FNET_GW, 2 * FNET_GW), lambda b, j: (0, 0)),
            pl.BlockSpec((2 * n1, 2 * n1), lambda b, j: (0, 0)),
            pl.BlockSpec((tb, n1, LANE), lambda b, j: (j, 0, 0)),
            pl.BlockSpec((tb, n1, LANE), lambda b, j: (j, 0, 0)),
        ],
        out_specs=pl.BlockSpec((1, 2, FNET_NG, n1, tb, FNET_GW), lambda b, j: (b, 0, 0, 0, j, 0)),
        out_shape=jax.ShapeDtypeStruct((bn, 2, FNET_NG, n1, n2, FNET_GW), F32),
        compiler_params=_cparams(("parallel", "parallel")),
        name="fnet_stage1",
    )(*([zf] * FNET_NG), jnp.asarray(chan, BF16), jnp.asarray(d1, BF16), jnp.asarray(twr), jnp.asarray(twi))

    kb = 8
    y = pl.pallas_call(
        _fnet2_body,
        grid=(bn, n1 // kb),
        in_specs=[
            pl.BlockSpec((1, 2, FNET_NG, kb, n2, FNET_GW), lambda b, i: (b, 0, 0, i, 0, 0)),
            pl.BlockSpec((n2, 2 * n2), lambda b, i: (0, 0)),
        ],
        out_specs=pl.BlockSpec((FNET_NG, 1, n2, kb, FNET_GW), lambda b, i: (0, b, 0, i, 0)),
        out_shape=jax.ShapeDtypeStruct((FNET_NG, bn, n2, n1, FNET_GW), F32),
        compiler_params=_cparams(("parallel", "parallel")),
        name="fnet_stage2",
    )(a, jnp.asarray(d2, BF16))
    return y.reshape(FNET_NG, bn * seq, FNET_GW)


def _merge_body(x_ref, g_ref, ya_ref, yb_ref, yc_ref, yd_ref, wglu_ref, wg0_ref, wg1_ref, wg2_ref, wg3_ref,
                wbr_ref, wo_ref, o_ref, h_scr, ya_scr):
    @pl.when(pl.program_id(1) == 0)
    def _():
        x = x_ref[...]
        h_scr[...] = _rms(x, g_ref[...]).astype(BF16)
        ya = ya_ref[...]
        ya_scr[...] = (ya * jax.nn.sigmoid(_dot(ya.astype(BF16), wglu_ref[...]))).astype(BF16)
        o_ref[...] = x

    h = h_scr[...]
    yd = jnp.concatenate([yd_ref[g] for g in range(FNET_NG)], axis=1).astype(BF16)
    ys = (ya_scr[...], yb_ref[...], yc_ref[...], yd)
    merged = None
    for kb, wg_ref in enumerate((wg0_ref, wg1_ref, wg2_ref, wg3_ref)):
        term = jax.nn.sigmoid(_dot(h, wg_ref[...])) * _dot(ys[kb], wbr_ref[kb])
        merged = term if merged is None else merged + term
    o_ref[...] += _dot(merged.astype(BF16), wo_ref[...])


def _merge(x2d, g, ya, yb, yc, yd, w_glu, w_in_bf, w_br, w_o, tm, tn):
    t = x2d.shape[0]
    row = lambda w: pl.BlockSpec((tm, w), lambda i, n: (i, 0))
    gate = lambda kb: pl.BlockSpec((D_MODEL, tn), lambda i, n: (0, (MIX_IN + kb * D_MODEL) // tn + n))
    return pl.pallas_call(
        _merge_body,
        grid=(t // tm, D_MODEL // tn),
        in_specs=[
            row(D_MODEL), pl.BlockSpec((1, D_MODEL), lambda i, n: (0, 0)),
            row(BW), row(BW), row(BW), pl.BlockSpec((FNET_NG, tm, FNET_GW), lambda i, n: (0, i, 0)),
            pl.BlockSpec((BW, BW), lambda i, n: (0, 0)),
            gate(0), gate(1), gate(2), gate(3),
            pl.BlockSpec((N_BRANCH, BW, tn), lambda i, n: (0, 0, n)),
            pl.BlockSpec((tn, D_MODEL), lambda i, n: (n, 0)),
        ],
        out_specs=row(D_MODEL),
        out_shape=jax.ShapeDtypeStruct((t, D_MODEL), F32),
        scratch_shapes=[pltpu.VMEM((tm, D_MODEL), BF16), pltpu.VMEM((tm, BW), BF16)],
        compiler_params=_cparams(("parallel", "arbitrary")),
        name="gated_merge",
    )(x2d, g, ya, yb, yc, yd, w_glu, w_in_bf, w_in_bf, w_in_bf, w_in_bf, w_br, w_o)


def _ffn_ple_body(x_ref, gf_ref, wa_ref, wb_ref, wo_ref, gp_ref, p_ref, wg_ref, wp_ref, o_ref, h_scr, *, nf, tn):
    f = pl.program_id(1)

    @pl.when(f == 0)
    def _():
        x = x_ref[...]
        h_scr[...] = _rms(x, gf_ref[...]).astype(BF16)
        o_ref[...] = x

    @pl.when(f < nf)
    def _():
        h = h_scr[...]
        a = _dot(h, wa_ref[...])
        b = _dot(h, wb_ref[...])
        o_ref[...] += _dot((jax.nn.silu(a) * b).astype(BF16), wo_ref[...])

    @pl.when(f == nf)
    def _():
        h_scr[...] = _rms(o_ref[...], gp_ref[...]).astype(BF16)
        p = p_ref[...].astype(BF16)
        for n in range(D_MODEL // tn):
            cols = slice(n * tn, (n + 1) * tn)
            pg = jax.nn.sigmoid(_dot(h_scr[...], wg_ref[:, cols]))
            o_ref[:, cols] += pg * _dot(p, wp_ref[:, cols])


def _ffn_ple(x2d, g_ffn, w_in, w_out, g_ple, p2d, w_gate, w_proj, tm, tf, tn):
    t = x2d.shape[0]
    nf = D_FF // tf
    ffn_step = lambda f: jnp.minimum(f, nf - 1)
    return pl.pallas_call(
        functools.partial(_ffn_ple_body, nf=nf, tn=tn),
        grid=(t // tm, nf + 1),
        in_specs=[
            pl.BlockSpec((tm, D_MODEL), lambda i, f: (i, 0)),
            pl.BlockSpec((1, D_MODEL), lambda i, f: (0, 0)),
            pl.BlockSpec((D_MODEL, tf), lambda i, f: (0, ffn_step(f))),
            pl.BlockSpec((D_MODEL, tf), lambda i, f: (0, nf + ffn_step(f))),
            pl.BlockSpec((tf, D_MODEL), lambda i, f: (ffn_step(f), 0)),
            pl.BlockSpec((1, D_MODEL), lambda i, f: (0, 0)),
            pl.BlockSpec((tm, PLE_DIM), lambda i, f: (i, 0)),
            pl.BlockSpec((D_MODEL, D_MODEL), lambda i, f: (0, 0)),
            pl.BlockSpec((PLE_DIM, D_MODEL), lambda i, f: (0, 0)),
        ],
        out_specs=pl.BlockSpec((tm, D_MODEL), lambda i, f: (i, 0)),
        out_shape=jax.ShapeDtypeStruct((t, D_MODEL), F32),
        scratch_shapes=[pltpu.VMEM((tm, D_MODEL), BF16)],
        compiler_params=_cparams(("parallel", "arbitrary")),
        name="swiglu_ffn_ple",
    )(x2d, g_ffn, w_in, w_in, w_out, g_ple, p2d, w_gate, w_proj)


def _tile(t, want):
    return min(t, want)


def _layer(x2d, p2d, bn, seq, lw):
    t = bn * seq
    z, qn, kn, vb = _inproj(x2d, lw['g_mix'], lw['w_in'], lw['q_gain'], lw['k_gain'], _tile(t, 1024))
    ya = _s5_mixer(z, bn, seq, lw['s5'], lw['index'])
    yb = _conv_mixer(z, bn, seq, lw['conv_w'], _tile(seq, 1024))
    yc = _na_mixer(qn, kn, vb, bn, seq, lw['na_bias'], lw['index'])
    yd = _fnet_mixer(z, bn, seq)
    x2d = _merge(x2d, lw['g_mix'], ya, yb, yc, yd, lw['w_glu'], lw['w_in'], lw['w_br'], lw['w_o'],
                 _tile(t, 512), 512)
    return _ffn_ple(x2d, lw['g_ffn'], lw['w_ffn_in'], lw['w_ffn_out'], lw['g_ple'], p2d, lw['w_ple_gate'],
                    lw['w_ple_proj'], _tile(t, 512), 512, 512)


def kernel(x_prompt, x_sample, p_prompt, p_sample, g_mix, w_in, s5_lam_re, s5_lam_im, s5_log_dt, s5_b_re,
           s5_b_im, s5_c_re, s5_c_im, s5_d, w_glu, conv_w, q_gain, k_gain, rel_bias, w_br, w_o, g_ffn,
           w_ffn_in, w_ffn_out, g_ple, w_ple_gate, w_ple_proj):
    depth = w_in.shape[0]
    s5_tabs = jax.vmap(_s5_tables)(s5_lam_re, s5_lam_im, s5_log_dt, s5_b_re, s5_b_im, s5_c_re, s5_c_im, s5_d)
    na_bias = jax.vmap(_na_bias_table)(rel_bias)
    q_gain_heads = jnp.tile(q_gain.astype(F32), (1, N_HEADS))
    k_gain_heads = jnp.tile(k_gain.astype(F32), (1, N_HEADS))
    layers = []
    for i in range(depth):
        layers.append(dict(
            index=i, g_mix=g_mix[i].astype(F32)[None], w_in=w_in[i].astype(BF16), s5=s5_tabs,
            w_glu=w_glu[i].astype(BF16), conv_w=conv_w[i], q_gain=q_gain_heads[i][None],
            k_gain=k_gain_heads[i][None], na_bias=na_bias, w_br=w_br[i].astype(BF16), w_o=w_o[i].astype(BF16),
            g_ffn=g_ffn[i].astype(F32)[None], w_ffn_in=w_ffn_in[i].astype(BF16),
            w_ffn_out=w_ffn_out[i].astype(BF16), g_ple=g_ple[i].astype(F32)[None],
            w_ple_gate=w_ple_gate[i].astype(BF16), w_ple_proj=w_ple_proj[i].astype(BF16)))

    def trunk(x, p):
        bn, seq, _ = x.shape
        x2d = x.reshape(bn * seq, D_MODEL)
        for i in range(depth):
            x2d = _layer(x2d, p[i].reshape(bn * seq, PLE_DIM), bn, seq, layers[i])
        return x2d.reshape(bn, seq, D_MODEL)

    return trunk(x_prompt, p_prompt), trunk(x_sample, p_sample)
```

```python
import functools
import math

import numpy as np
import jax
import jax.numpy as jnp
from jax import lax
from jax.experimental import pallas as pl
from jax.experimental.pallas import tpu as pltpu

F32 = jnp.float32
BF16 = jnp.bfloat16

D_MODEL = 2048
BW = 512
MIX_IN = 8 * BW
N_BRANCH = 4
S5_H = 16
S5_G = BW // S5_H
S5_P = 64
S5_CHUNK = 16
LANE = 128
S5_JB = BW // LANE
S5_GB = LANE // S5_H
S5_CW = S5_CHUNK * LANE
SCAN_ROWS = 8
SCAN_STEPS = (1, 2, 4)
SCAN_ROWPOW = len(SCAN_STEPS)
SCAN_TILEPOW = SCAN_ROWPOW + 1
N_HEADS = 8
HEAD_DIM = 64
GRID_W = 64
NA_ROWS = 8
NA_COLS = 16
FNET_GW = 128
D_FF = 5632
PLE_DIM = 256
EPS = 1e-6
NEG_INF = -1e30
VMEM_LIMIT = 56 * 1024 * 1024


def _cparams(sem):
    return pltpu.CompilerParams(dimension_semantics=sem, vmem_limit_bytes=VMEM_LIMIT)


def _rms(x, g):
    ms = jnp.mean(x * x, axis=-1, keepdims=True)
    return x * lax.rsqrt(ms + EPS) * g


def _dot(a, b):
    return jnp.dot(a, b, preferred_element_type=F32)


INPROJ_TN = 2 * BW
Z_COLS = MIX_IN - INPROJ_TN
QK_STEP = 2
V_STEP = 3


def _head_mean_sq(x, ones_ref):
    x2 = x * x
    hi = x2.astype(BF16)
    lo = (x2 - hi.astype(F32)).astype(BF16)
    return _dot(hi, ones_ref[...]) + _dot(lo, ones_ref[...])


def _inproj_body(x_ref, g_ref, w_ref, qg_ref, kg_ref, ones_ref, z_ref, q_ref, k_ref, v_ref, h_ref):
    j = pl.program_id(1)

    @pl.when(j == 0)
    def _():
        h_ref[...] = _rms(x_ref[...], g_ref[...]).astype(BF16)

    acc = _dot(h_ref[...], w_ref[...])

    @pl.when(j != QK_STEP)
    def _():
        z_ref[...] = acc

    @pl.when(j == QK_STEP)
    def _():
        q, k = acc[:, :BW], acc[:, BW:]
        qn = q * lax.rsqrt(_head_mean_sq(q, ones_ref) + EPS) * qg_ref[...]
        kn = k * lax.rsqrt(_head_mean_sq(k, ones_ref) + EPS) * kg_ref[...]
        q_ref[...] = (qn * (1.0 / math.sqrt(HEAD_DIM))).astype(BF16)
        k_ref[...] = kn.astype(BF16)

    @pl.when(j == V_STEP)
    def _():
        v_ref[...] = acc[:, :BW].astype(BF16)


def _inproj(x2d, g, w_bf, q_gain, k_gain, tm):
    t = x2d.shape[0]
    ones = np.kron(np.eye(N_HEADS, dtype=np.float32), np.full((HEAD_DIM, HEAD_DIM), 1.0 / HEAD_DIM, np.float32))
    vec = pl.BlockSpec((1, BW), lambda i, j: (0, 0))
    head = pl.BlockSpec((tm, BW), lambda i, j: (i, 0))
    return pl.pallas_call(
        _inproj_body,
        grid=(t // tm, MIX_IN // INPROJ_TN),
        in_specs=[
            pl.BlockSpec((tm, D_MODEL), lambda i, j: (i, 0)),
            pl.BlockSpec((1, D_MODEL), lambda i, j: (0, 0)),
            pl.BlockSpec((D_MODEL, INPROJ_TN), lambda i, j: (0, j)),
            vec, vec, pl.BlockSpec((BW, BW), lambda i, j: (0, 0)),
        ],
        out_specs=[pl.BlockSpec((tm, INPROJ_TN), lambda i, j: (i, jnp.where(j >= QK_STEP, j - 1, j))),
                   head, head, head, pl.BlockSpec((tm, D_MODEL), lambda i, j: (i, 0))],
        out_shape=[jax.ShapeDtypeStruct((t, Z_COLS), F32)] + [jax.ShapeDtypeStruct((t, BW), BF16)] * 3
                  + [jax.ShapeDtypeStruct((t, D_MODEL), BF16)],
        compiler_params=_cparams(("parallel", "arbitrary")),
        name="inproj",
    )(x2d, g, w_bf, q_gain, k_gain, jnp.asarray(ones, BF16))


def _s5_tables(lam_re, lam_im, log_dt, b_re, b_im, c_re, c_im, d_skip):
    f = lambda a: a.astype(F32)
    lam_re, lam_im, b_re, b_im, c_re, c_im = map(f, (lam_re, lam_im, b_re, b_im, c_re, c_im))
    dt = jnp.exp(f(log_dt))[..., None]
    lr, li = lam_re * dt, lam_im * dt
    mag = jnp.exp(lr)
    ar, ai = mag * jnp.cos(li), mag * jnp.sin(li)
    den = lam_re * lam_re + lam_im * lam_im
    fr = ((ar - 1.0) * lam_re + ai * lam_im) / den
    fi = (ai * lam_re - (ar - 1.0) * lam_im) / den
    bbr = (fr[..., None] * b_re - fi[..., None] * b_im).transpose(0, 1, 3, 2)
    bbi = (fr[..., None] * b_im + fi[..., None] * b_re).transpose(0, 1, 3, 2)
    n = jnp.arange(S5_CHUNK + 1, dtype=F32)[:, None, None, None]
    pr = jnp.exp(n * lr) * jnp.cos(n * li)
    pi = jnp.exp(n * lr) * jnp.sin(n * li)
    abr = pr[:, :, :, None, :] * bbr - pi[:, :, :, None, :] * bbi
    abi = pr[:, :, :, None, :] * bbi + pi[:, :, :, None, :] * bbr
    kk = jnp.sum(c_re[None, :, :, :, None, :] * abr[:, :, :, None, :, :]
                 - c_im[None, :, :, :, None, :] * abi[:, :, :, None, :, :], axis=-1)
    tt = np.arange(S5_CHUNK)
    nq = S5_GB // 2
    dsk = f(d_skip).reshape(S5_G, S5_H)
    k0 = kk[0, 0] + kk[0, 1] + dsk[:, :, None] * jnp.asarray(np.eye(S5_H, dtype=np.float32))[None]
    kcomb = jnp.concatenate([kk[1:S5_CHUNK, 1][::-1], k0[None], kk[1:S5_CHUNK, 0]], axis=0)
    kcomb = kcomb.reshape(2 * S5_CHUNK - 1, S5_JB, S5_GB, S5_H, S5_H).transpose(1, 0, 4, 2, 3)
    kcomb = kcomb.reshape(S5_JB, 2 * S5_CHUNK - 1, S5_H, LANE)
    lag = tt[None, :] - tt[:, None]
    y = kcomb[:, lag + (S5_CHUNK - 1)]
    y = y.transpose(0, 1, 3, 2, 4).reshape(S5_JB, S5_CHUNK, S5_H, S5_CW)
    out_group = (np.arange(S5_CW) // S5_H) % S5_GB
    row_group = np.arange(S5_GB)
    mmask = row_group[:, None, None] == out_group[None, None, :]
    m = jnp.where(mmask[None, None], y[:, :, None], 0.0).reshape(S5_JB, S5_CW, S5_CW)

    sw = 2 * S5_GB * 2 * S5_P
    ab = jnp.stack([abr, abi], axis=0)
    spow = (S5_CHUNK - 1 - tt, tt)
    sel = jnp.stack([ab[:, spow[d], d] for d in range(2)], axis=0)
    sel = sel.reshape(2, 2, S5_CHUNK, S5_JB, nq, 2, S5_H, S5_P)
    zs = sel.transpose(3, 2, 6, 0, 4, 1, 5, 7).reshape(S5_JB, S5_CHUNK, S5_H, sw)
    scol = np.arange(sw)
    state_group = ((scol // (2 * 2 * S5_P)) % nq) * 2 + (scol // S5_P) % 2
    smask = row_group[:, None, None] == state_group[None, None, :]
    s = jnp.where(smask[None, None], zs[:, :, None], 0.0).reshape(S5_JB, S5_CW, sw)

    def lanes_gh(c):
        return c.reshape(2, S5_JB, S5_GB, S5_H, S5_P).transpose(0, 1, 4, 2, 3).reshape(2, S5_JB, S5_P, 1, LANE)

    def pow_gh(a):
        e = jnp.stack([a[tt + 1, 0], a[S5_CHUNK - tt, 1]], axis=0)
        e = e.reshape(2, S5_CHUNK, S5_JB, S5_GB, S5_P).transpose(0, 2, 4, 1, 3)
        return jnp.broadcast_to(e[..., None], e.shape + (S5_H,)).reshape(2, S5_JB, S5_P, S5_CHUNK, LANE)

    cr, ci, er, ei = lanes_gh(c_re), lanes_gh(c_im), pow_gh(pr), pow_gh(pi)
    w = jnp.stack([cr * er - ci * ei, -(cr * ei + ci * er)], axis=2)
    w = w.reshape(2, S5_JB, 2, S5_P, S5_CW)
    omask = (np.arange(nq)[:, None, None] * 2 + np.arange(2)[None, :, None]) == out_group[None, None, :]
    o = jnp.where(omask[None, None, :, None, :, None, :], w[:, :, None, :, None, :, :], 0.0)
    o = o.reshape(2, S5_JB, S5_GB * 2 * S5_P, S5_CW)

    nn = jnp.arange(SCAN_ROWS + 1, dtype=F32)[:, None, None, None] * S5_CHUNK
    cw = nq * 2 * S5_P
    pw = jnp.stack([(jnp.exp(nn * lr) * jnp.cos(nn * li)).reshape(SCAN_ROWS + 1, 2, S5_JB, cw),
                    (jnp.exp(nn * lr) * jnp.sin(nn * li)).reshape(SCAN_ROWS + 1, 2, S5_JB, cw)], axis=2)
    rows = np.arange(SCAN_ROWS)
    bcast = lambda a: jnp.broadcast_to(a[:, :, None], (2, 2, SCAN_ROWS, S5_JB, cw))
    rowpow = jnp.stack([pw[rows, 0], pw[SCAN_ROWS - 1 - rows, 1]], axis=0).transpose(0, 2, 1, 3, 4)
    tab = jnp.stack([bcast(pw[k]) for k in SCAN_STEPS] + [rowpow, bcast(pw[SCAN_ROWS])], axis=0)
    tab = tab.transpose(4, 1, 0, 2, 3, 5)
    return (m.astype(BF16), s.astype(BF16), o[0].astype(BF16), o[1].astype(BF16), tab)


def _chunk_rows(u_ref):
    rbs = u_ref.shape[0] // S5_CHUNK
    return jnp.concatenate([u_ref[pl.ds(t, rbs, stride=S5_CHUNK), :] for t in range(S5_CHUNK)], axis=-1).astype(BF16)


def _s5_state_body(u_ref, s_ref, o_ref):
    o_ref[...] = _dot(_chunk_rows(u_ref), s_ref[0])


def _s5_scan_body(sf_ref, sb_ref, tab_ref, xf_ref, xb_ref, carry):
    @pl.when(pl.program_id(1) == 0)
    def _():
        carry[...] = jnp.zeros_like(carry)

    ntile = sf_ref.shape[0] // SCAN_ROWS
    nq = sf_ref.shape[1] // (2 * LANE)
    row = lax.broadcasted_iota(jnp.int32, (SCAN_ROWS, LANE), 0)

    def shift(v, k, d):
        if d == 0:
            return jnp.where(row >= k, pltpu.roll(v, k, axis=0), 0.0)
        return jnp.where(row < SCAN_ROWS - k, pltpu.roll(v, SCAN_ROWS - k, axis=0), 0.0)

    def cmul(d, kind, q, xr, xi):
        a_r = tab_ref[0, d, kind, 0, :, q * LANE:(q + 1) * LANE]
        a_i = tab_ref[0, d, kind, 1, :, q * LANE:(q + 1) * LANE]
        return a_r * xr - a_i * xi, a_r * xi + a_i * xr

    def tile(d, s_ref, x_ref, r0, cr):
        last = SCAN_ROWS - 1 if d == 0 else 0
        new = []
        for q in range(nq):
            re = slice(q * 2 * LANE, q * 2 * LANE + LANE)
            im = slice(q * 2 * LANE + LANE, (q + 1) * 2 * LANE)
            yr, yi = s_ref[pl.ds(r0, SCAN_ROWS), re], s_ref[pl.ds(r0, SCAN_ROWS), im]
            for kind, k in enumerate(SCAN_STEPS):
                tr, ti = cmul(d, kind, q, shift(yr, k, d), shift(yi, k, d))
                yr, yi = yr + tr, yi + ti
            c_r, c_i = cr[2 * q], cr[2 * q + 1]
            er, ei = cmul(d, SCAN_ROWPOW, q, c_r, c_i)
            x_ref[pl.ds(r0, SCAN_ROWS), re] = shift(yr, 1, d) + er
            x_ref[pl.ds(r0, SCAN_ROWS), im] = shift(yi, 1, d) + ei
            nr, ni = cmul(d, SCAN_TILEPOW, q, c_r, c_i)
            new.append(jnp.broadcast_to(yr[last:last + 1], (SCAN_ROWS, LANE)) + nr)
            new.append(jnp.broadcast_to(yi[last:last + 1], (SCAN_ROWS, LANE)) + ni)
        return tuple(new)

    def step(t, cs):
        cf, cb = cs
        rf = pl.multiple_of(t * SCAN_ROWS, SCAN_ROWS)
        rb = pl.multiple_of((ntile - 1 - t) * SCAN_ROWS, SCAN_ROWS)
        return tile(0, sf_ref, xf_ref, rf, cf), tile(1, sb_ref, xb_ref, rb, cb)

    init = tuple(tuple(carry[d, :, t * LANE:(t + 1) * LANE] for t in range(2 * nq)) for d in range(2))
    cf, cb = lax.fori_loop(0, ntile, step, init)
    for d, cs in enumerate((cf, cb)):
        for t in range(2 * nq):
            carry[d, :, t * LANE:(t + 1) * LANE] = cs[t]


def _s5_out_body(u_ref, xf_ref, xb_ref, m_ref, of_ref, ob_ref, y_ref):
    y = (_dot(_chunk_rows(u_ref), m_ref[0])
         + _dot(xf_ref[...].astype(BF16), of_ref[0])
         + _dot(xb_ref[...].astype(BF16), ob_ref[0]))
    y = jax.nn.gelu(y)
    for t in range(S5_CHUNK):
        y_ref[pl.ds(t, y.shape[0], stride=S5_CHUNK), :] = y[:, t * LANE:(t + 1) * LANE]


def _s5_mixer(z, bn, seq, tabs, li):
    m, s, of, ob, scan_tab = tabs
    nc = seq // S5_CHUNK
    rbs = min(nc, 256)
    nrb = nc // rbs
    sw = s.shape[-1]
    half = sw // 2
    u_spec = pl.BlockSpec((rbs * S5_CHUNK, LANE), lambda j, b, r: (b * nrb + r, j))
    st = pl.pallas_call(
        _s5_state_body,
        grid=(S5_JB, bn, nrb),
        in_specs=[u_spec, pl.BlockSpec((None, 1, S5_CW, sw), lambda j, b, r: (li, j, 0, 0))],
        out_specs=pl.BlockSpec((rbs, sw), lambda j, b, r: (r, b * S5_JB + j)),
        out_shape=jax.ShapeDtypeStruct((nc, bn * S5_JB * sw), F32),
        compiler_params=_cparams(("parallel", "parallel", "parallel")),
        name="s5_state",
    )(z, s)

    nseq = bn * S5_JB
    sbs = min(nc, 256)
    nsb = nc // sbs
    xf, xb = pl.pallas_call(
        _s5_scan_body,
        grid=(nseq, nsb),
        in_specs=[
            pl.BlockSpec((sbs, half), lambda q, i: (i, 2 * q)),
            pl.BlockSpec((sbs, half), lambda q, i: (nsb - 1 - i, 2 * q + 1)),
            pl.BlockSpec((None, 1) + scan_tab.shape[2:], lambda q, i: (li, q % S5_JB, 0, 0, 0, 0, 0)),
        ],
        out_specs=[
            pl.BlockSpec((sbs, half), lambda q, i: (i, q)),
            pl.BlockSpec((sbs, half), lambda q, i: (nsb - 1 - i, q)),
        ],
        out_shape=[jax.ShapeDtypeStruct((nc, nseq * half), F32)] * 2,
        scratch_shapes=[pltpu.VMEM((2, SCAN_ROWS, half), F32)],
        compiler_params=_cparams(("parallel", "arbitrary")),
        name="s5_scan",
    )(st, st, scan_tab)

    x_spec = pl.BlockSpec((rbs, half), lambda j, b, r: (r, b * S5_JB + j))
    y = pl.pallas_call(
        _s5_out_body,
        grid=(S5_JB, bn, nrb),
        in_specs=[
            u_spec, x_spec, x_spec,
            pl.BlockSpec((None, 1, S5_CW, S5_CW), lambda j, b, r: (li, j, 0, 0)),
            pl.BlockSpec((None, 1, half, S5_CW), lambda j, b, r: (li, j, 0, 0)),
            pl.BlockSpec((None, 1, half, S5_CW), lambda j, b, r: (li, j, 0, 0)),
        ],
        out_specs=u_spec,
        out_shape=jax.ShapeDtypeStruct((bn * seq, BW), F32),
        compiler_params=_cparams(("parallel", "parallel", "parallel")),
        name="s5_out",
    )(z, xf, xb, m, of, ob)
    return y


def _conv_body(b_ref, c_ref, v_ref, cp_ref, vp_ref, cn_ref, vn_ref, w_ref, o_ref):
    i = pl.program_id(1)
    tc = c_ref.shape[1]
    z = c_ref[0] * v_ref[0]
    zp = cp_ref[0][7:8, :] * vp_ref[0][7:8, :]
    zn = cn_ref[0][0:1, :] * vn_ref[0][0:1, :]
    zp = jnp.where(i == 0, 0.0, zp)
    zn = jnp.where(i == pl.num_programs(1) - 1, 0.0, zn)
    row = lax.broadcasted_iota(jnp.int32, z.shape, 0)
    up = jnp.where(row == 0, zp, pltpu.roll(z, 1, axis=0))
    dn = jnp.where(row == tc - 1, zn, pltpu.roll(z, tc - 1, axis=0))
    w = w_ref[...]
    y = w[0:1, :] * up + w[1:2, :] * z + w[2:3, :] * dn
    o_ref[0] = (b_ref[0] * y).astype(o_ref.dtype)


def _conv_mixer(z, bn, seq, conv_w, tc):
    z3 = z.reshape(bn, seq, Z_COLS)
    nb8 = tc // 8
    last8 = seq // 8 - 1
    main = lambda col: pl.BlockSpec((1, tc, BW), lambda b, i: (b, i, col))
    prev = lambda col: pl.BlockSpec((1, 8, BW), lambda b, i: (b, jnp.maximum(i * nb8 - 1, 0), col))
    nxt = lambda col: pl.BlockSpec((1, 8, BW), lambda b, i: (b, jnp.minimum((i + 1) * nb8, last8), col))
    y = pl.pallas_call(
        _conv_body,
        grid=(bn, seq // tc),
        in_specs=[main(1), main(2), main(3), prev(2), prev(3), nxt(2), nxt(3),
                  pl.BlockSpec((3, BW), lambda b, i: (0, 0))],
        out_specs=pl.BlockSpec((1, tc, BW), lambda b, i: (b, i, 0)),
        out_shape=jax.ShapeDtypeStruct((bn, seq, BW), BF16),
        compiler_params=_cparams(("parallel", "parallel")),
        name="short_conv",
    )(z3, z3, z3, z3, z3, z3, z3, conv_w.astype(F32))
    return y.reshape(bn * seq, BW)


def _na_bias_table(rel_bias):
    qc = np.arange(GRID_W)[:, None]
    kc = np.arange(GRID_W)[None, :]
    cs = np.clip(qc - NA_COLS // 2, 0, GRID_W - NA_COLS)
    valid = (kc >= cs) & (kc < cs + NA_COLS)
    dc = np.clip(kc - qc + (NA_COLS - 1), 0, 2 * NA_COLS - 2)
    pick = (dc[None] == np.arange(2 * NA_COLS - 1)[:, None, None]).astype(np.float32)
    b = jnp.einsum('hrd,dqk->hrqk', rel_bias.astype(F32), jnp.asarray(pick), precision=lax.Precision.HIGHEST)
    b = jnp.where(valid[None, None], b, NEG_INF)
    bq = b.transpose(0, 2, 1, 3)
    ndr = 2 * NA_ROWS - 2
    bt = [bq[:, :, o:o + ndr].reshape(N_HEADS, GRID_W, ndr * GRID_W) for o in range(2)]
    tab = jnp.stack([bt[s % 2][:, :, (s - s % 2) * GRID_W:(s - s % 2 + NA_ROWS) * GRID_W]
                     for s in range(NA_ROWS)], axis=0)
    return tab.reshape(NA_ROWS, N_HEADS // 2, 2 * GRID_W, NA_ROWS * GRID_W)


def _na_body(q_ref, kp_ref, kc_ref, kn_ref, vp_ref, vc_ref, vn_ref, bias_ref, o_ref, k_scr, v_scr, *, rows):
    i = pl.program_id(1)
    blk = NA_ROWS * GRID_W
    for n, (kr, vr) in enumerate(((kp_ref, vp_ref), (kc_ref, vc_ref), (kn_ref, vn_ref))):
        k_scr[n * blk:(n + 1) * blk, :] = kr[0]
        v_scr[n * blk:(n + 1) * blk, :] = vr[0]
    first_head = lax.broadcasted_iota(jnp.int32, (GRID_W, 2 * HEAD_DIM), 1) < HEAD_DIM
    for ir in range(NA_ROWS):
        r = i * NA_ROWS + ir
        rs = jnp.clip(r - NA_ROWS // 2, 0, rows - NA_ROWS)
        off = pl.multiple_of((rs - (i - 1) * NA_ROWS) * GRID_W, GRID_W)
        bias_row = rs - r + (NA_ROWS - 1)
        pair_lanes = [slice(hp * 2 * HEAD_DIM, (hp + 1) * 2 * HEAD_DIM) for hp in range(N_HEADS // 2)]
        scores = []
        for hp, lanes in enumerate(pair_lanes):
            q = q_ref[0, ir * GRID_W:(ir + 1) * GRID_W, lanes]
            zero = jnp.zeros_like(q)
            q2 = jnp.concatenate([jnp.where(first_head, q, zero), jnp.where(first_head, zero, q)], axis=0)
            kw = k_scr[pl.ds(off, blk), lanes]
            s = lax.dot_general(q2, kw, (((1,), (1,)), ((), ())), preferred_element_type=F32)
            scores.append(s + bias_ref[bias_row, hp])
        probs = []
        for s in scores:
            e = jnp.exp(s - jnp.max(s, axis=-1, keepdims=True))
            probs.append((e.astype(BF16), jnp.sum(e, axis=-1, keepdims=True)))
        for lanes, (e, l) in zip(pair_lanes, probs):
            o = _dot(e, v_scr[pl.ds(off, blk), lanes]) / l
            o_ref[0, ir * GRID_W:(ir + 1) * GRID_W, lanes] = jnp.where(
                first_head, o[:GRID_W], o[GRID_W:]).astype(o_ref.dtype)


def _na_mixer(qn, kn, vb, bn, seq, bias_tab, li):
    rows = seq // GRID_W
    nblk = rows // NA_ROWS
    assert rows % NA_ROWS == 0 and nblk >= 2
    blk = NA_ROWS * GRID_W
    q3, k3, v3 = (a.reshape(bn, seq, BW) for a in (qn, kn, vb))
    cur = pl.BlockSpec((1, blk, BW), lambda b, i: (b, i, 0))
    prev = pl.BlockSpec((1, blk, BW), lambda b, i: (b, jnp.maximum(i - 1, 0), 0))
    nxt = pl.BlockSpec((1, blk, BW), lambda b, i: (b, jnp.minimum(i + 1, nblk - 1), 0))
    y = pl.pallas_call(
        functools.partial(_na_body, rows=rows),
        grid=(bn, nblk),
        in_specs=[cur, prev, cur, nxt, prev, cur, nxt,
                  pl.BlockSpec((None, NA_ROWS, N_HEADS // 2, 2 * GRID_W, blk), lambda b, i: (li, 0, 0, 0, 0))],
        out_specs=pl.BlockSpec((1, blk, BW), lambda b, i: (b, i, 0)),
        out_shape=jax.ShapeDtypeStruct((bn, seq, BW), BF16),
        scratch_shapes=[pltpu.VMEM((3 * blk, BW), BF16), pltpu.VMEM((3 * blk, BW), BF16)],
        compiler_params=_cparams(("parallel", "arbitrary")),
        name="nbr_attention",
    )(q3, k3, k3, k3, v3, v3, v3, bias_tab)
    return y.reshape(bn * seq, BW)


def _fnet_factors(seq):
    n1 = 1 << ((seq.bit_length() - 1 + 1) // 2)
    assert seq == n1 * (seq // n1) and seq & (seq - 1) == 0
    return n1, seq // n1


@functools.lru_cache(maxsize=None)
def _fnet_tables(seq):
    n1, n2 = _fnet_factors(seq)

    def cs(n, scale):
        idx = np.outer(np.arange(n), np.arange(n)) % n
        ang = 2.0 * np.pi * idx / n
        return np.cos(ang) * scale, np.sin(ang) * scale

    cc, sc = cs(FNET_GW, FNET_GW ** -0.5)
    chan = np.concatenate([cc, sc], axis=1)
    c1, s1 = cs(n1, n1 ** -0.5)
    d1 = np.block([[c1, -s1], [s1, c1]])
    c2, s2 = cs(n2, n2 ** -0.5)
    d2 = np.concatenate([c2, -s2], axis=1)
    ang = 2.0 * np.pi * (np.outer(np.arange(n2), np.arange(n1)) % seq) / seq
    twr = np.broadcast_to(np.cos(ang)[:, :, None], (n2, n1, LANE))
    twi = np.broadcast_to(np.sin(ang)[:, :, None], (n2, n1, LANE))
    return (np.asarray(chan, np.float32), np.asarray(d1, np.float32), np.asarray(d2, np.float32),
            np.ascontiguousarray(twr, np.float32), np.ascontiguousarray(twi, np.float32))


FNET_NG = BW // FNET_GW


def _fnet1_body(x0_ref, x1_ref, x2_ref, x3_ref, chan_ref, d1_ref, twr_ref, twi_ref, o_ref):
    n1, tb = x0_ref.shape[1], x0_ref.shape[2]
    xs = [r.reshape(n1 * tb, FNET_GW) for r in (x0_ref, x1_ref, x2_ref, x3_ref)]
    o2 = o_ref.reshape(2 * FNET_NG * n1 * tb, FNET_GW)
    for t in range(tb):
        pq = [_dot(x[pl.ds(t, n1, stride=tb), :].astype(BF16), chan_ref[...]) for x in xs]
        p = jnp.concatenate([a[:, :FNET_GW] for a in pq], axis=1)
        q = jnp.concatenate([a[:, FNET_GW:] for a in pq], axis=1)
        a = _dot(d1_ref[...], jnp.concatenate([p, q], axis=0).astype(BF16))
        ar, ai = a[:n1], a[n1:]
        twr = jnp.concatenate([twr_ref[t]] * FNET_NG, axis=1)
        twi = jnp.concatenate([twi_ref[t]] * FNET_NG, axis=1)
        for ri, v in enumerate((ar * twr - ai * twi, ar * twi + ai * twr)):
            for g in range(FNET_NG):
                o2[pl.ds((ri * FNET_NG + g) * n1 * tb + t, n1, stride=tb), :] = v[:, g * FNET_GW:(g + 1) * FNET_GW]


def _fnet2_body(a_ref, d2_ref, o_ref):
    kb, n2 = a_ref.shape[3], a_ref.shape[4]
    o2 = o_ref.reshape(FNET_NG * n2 * kb, FNET_GW)
    for k in range(kb):
        a = jnp.concatenate([jnp.concatenate([a_ref[0, ri, g, k] for g in range(FNET_NG)], axis=1)
                             for ri in range(2)], axis=0).astype(BF16)
        y = _dot(d2_ref[...], a)
        for g in range(FNET_NG):
            o2[pl.ds(g * n2 * kb + k, n2, stride=kb), :] = y[:, g * FNET_GW:(g + 1) * FNET_GW]


def _fnet_mixer(z, bn, seq):
    n1, n2 = _fnet_factors(seq)
    chan, d1, d2, twr, twi = _fnet_tables(seq)
    tb = 8
    zf = z.reshape(bn, n1, n2, Z_COLS)
    first_group = (Z_COLS - BW) // FNET_GW
    a = pl.pallas_call(
        _fnet1_body,
        grid=(bn, n2 // tb),
        in_specs=[pl.BlockSpec((1, n1, tb, FNET_GW), lambda b, j, g=g: (b, 0, j, first_group + g))
                  for g in range(FNET_NG)] + [
            pl.BlockSpec((FNET_GW, 2 * FNET_GW), lambda b, j: (0, 0)),
            pl.BlockSpec((2 * n1, 2 * n1), lambda b, j: (0, 0)),
            pl.BlockSpec((tb, n1, LANE), lambda b, j: (j, 0, 0)),
            pl.BlockSpec((tb, n1, LANE), lambda b, j: (j, 0, 0)),
        ],
        out_specs=pl.BlockSpec((1, 2, FNET_NG, n1, tb, FNET_GW), lambda b, j: (b, 0, 0, 0, j, 0)),
        out_shape=jax.ShapeDtypeStruct((bn, 2, FNET_NG, n1, n2, FNET_GW), F32),
        compiler_params=_cparams(("parallel", "parallel")),
        name="fnet_stage1",
    )(*([zf] * FNET_NG), jnp.asarray(chan, BF16), jnp.asarray(d1, BF16), jnp.asarray(twr), jnp.asarray(twi))

    kb = 8
    y = pl.pallas_call(
        _fnet2_body,
        grid=(bn, n1 // kb),
        in_specs=[
            pl.BlockSpec((1, 2, FNET_NG, kb, n2, FNET_GW), lambda b, i: (b, 0, 0, i, 0, 0)),
            pl.BlockSpec((n2, 2 * n2), lambda b, i: (0, 0)),
        ],
        out_specs=pl.BlockSpec((FNET_NG, 1, n2, kb, FNET_GW), lambda b, i: (0, b, 0, i, 0)),
        out_shape=jax.ShapeDtypeStruct((FNET_NG, bn, n2, n1, FNET_GW), F32),
        compiler_params=_cparams(("parallel", "parallel")),
        name="fnet_stage2",
    )(a, jnp.asarray(d2, BF16))
    return y.reshape(FNET_NG, bn * seq, FNET_GW)


def _merge_body(x_ref, h_ref, ya_ref, yb_ref, yc_ref, yd_ref, wglu_ref, wg0_ref, wg1_ref, wg2_ref, wg3_ref,
                wbr_ref, wo_ref, o_ref, ya_scr):
    n = pl.program_id(1)

    @pl.when(n == 0)
    def _():
        ya = ya_ref[...]
        ya_scr[...] = (ya * jax.nn.sigmoid(_dot(ya.astype(BF16), wglu_ref[...]))).astype(BF16)

    h = h_ref[...]
    yd = jnp.concatenate([yd_ref[g] for g in range(FNET_NG)], axis=1).astype(BF16)
    ys = (ya_scr[...], yb_ref[...], yc_ref[...], yd)
    merged = None
    for kb, wg_ref in enumerate((wg0_ref, wg1_ref, wg2_ref, wg3_ref)):
        term = jax.nn.sigmoid(_dot(h, wg_ref[...])) * _dot(ys[kb], wbr_ref[kb])
        merged = term if merged is None else merged + term
    merged = merged.astype(BF16)

    @pl.when(n == 0)
    def _():
        o_ref[...] = x_ref[...] + _dot(merged, wo_ref[...])

    @pl.when(n > 0)
    def _():
        o_ref[...] += _dot(merged, wo_ref[...])


def _merge(x2d, h, ya, yb, yc, yd, w_glu, w_in_bf, w_br, w_o, tm, tn):
    t = x2d.shape[0]
    row = lambda w: pl.BlockSpec((tm, w), lambda i, n: (i, 0))
    gate = lambda kb: pl.BlockSpec((D_MODEL, tn), lambda i, n: (0, (MIX_IN + kb * D_MODEL) // tn + n))
    return pl.pallas_call(
        _merge_body,
        grid=(t // tm, D_MODEL // tn),
        in_specs=[
            row(D_MODEL), row(D_MODEL),
            row(BW), row(BW), row(BW), pl.BlockSpec((FNET_NG, tm, FNET_GW), lambda i, n: (0, i, 0)),
            pl.BlockSpec((BW, BW), lambda i, n: (0, 0)),
            gate(0), gate(1), gate(2), gate(3),
            pl.BlockSpec((N_BRANCH, BW, tn), lambda i, n: (0, 0, n)),
            pl.BlockSpec((tn, D_MODEL), lambda i, n: (n, 0)),
        ],
        out_specs=row(D_MODEL),
        out_shape=jax.ShapeDtypeStruct((t, D_MODEL), F32),
        scratch_shapes=[pltpu.VMEM((tm, BW), BF16)],
        compiler_params=_cparams(("parallel", "arbitrary")),
        name="gated_merge",
    )(x2d, h, ya, yb, yc, yd, w_glu, w_in_bf, w_in_bf, w_in_bf, w_in_bf, w_br, w_o)


def _ffn_ple_body(x_ref, gf_ref, wa_ref, wb_ref, wo_ref, gp_ref, p_ref, wg_ref, wp_ref, o_ref, h_scr, *, nf, tn):
    f = pl.program_id(1)

    def ffn_slice():
        h = h_scr[...]
        a = _dot(h, wa_ref[...])
        b = _dot(h, wb_ref[...])
        return _dot((jax.nn.silu(a) * b).astype(BF16), wo_ref[...])

    @pl.when(f == 0)
    def _():
        x = x_ref[...]
        h_scr[...] = _rms(x, gf_ref[...]).astype(BF16)
        o_ref[...] = x + ffn_slice()

    @pl.when((f > 0) & (f < nf))
    def _():
        o_ref[...] += ffn_slice()

    @pl.when(f == nf)
    def _():
        h_scr[...] = _rms(o_ref[...], gp_ref[...]).astype(BF16)
        p = p_ref[...].astype(BF16)
        for n in range(D_MODEL // tn):
            cols = slice(n * tn, (n + 1) * tn)
            pg = jax.nn.sigmoid(_dot(h_scr[...], wg_ref[:, cols]))
            o_ref[:, cols] += pg * _dot(p, wp_ref[:, cols])


def _ffn_ple(x2d, g_ffn, w_in, w_out, g_ple, p2d, w_gate, w_proj, tm, tf, tn):
    t = x2d.shape[0]
    nf = D_FF // tf
    ffn_step = lambda f: jnp.minimum(f, nf - 1)
    return pl.pallas_call(
        functools.partial(_ffn_ple_body, nf=nf, tn=tn),
        grid=(t // tm, nf + 1),
        in_specs=[
            pl.BlockSpec((tm, D_MODEL), lambda i, f: (i, 0)),
            pl.BlockSpec((1, D_MODEL), lambda i, f: (0, 0)),
            pl.BlockSpec((D_MODEL, tf), lambda i, f: (0, ffn_step(f))),
            pl.BlockSpec((D_MODEL, tf), lambda i, f: (0, nf + ffn_step(f))),
            pl.BlockSpec((tf, D_MODEL), lambda i, f: (ffn_step(f), 0)),
            pl.BlockSpec((1, D_MODEL), lambda i, f: (0, 0)),
            pl.BlockSpec((tm, PLE_DIM), lambda i, f: (i, 0)),
            pl.BlockSpec((D_MODEL, D_MODEL), lambda i, f: (0, 0)),
            pl.BlockSpec((PLE_DIM, D_MODEL), lambda i, f: (0, 0)),
        ],
        out_specs=pl.BlockSpec((tm, D_MODEL), lambda i, f: (i, 0)),
        out_shape=jax.ShapeDtypeStruct((t, D_MODEL), F32),
        scratch_shapes=[pltpu.VMEM((tm, D_MODEL), BF16)],
        compiler_params=_cparams(("parallel", "arbitrary")),
        name="swiglu_ffn_ple",
    )(x2d, g_ffn, w_in, w_in, w_out, g_ple, p2d, w_gate, w_proj)


def _tile(t, want):
    return min(t, want)


def _layer(x2d, p2d, bn, seq, lw):
    t = bn * seq
    z, qn, kn, vb, h = _inproj(x2d, lw['g_mix'], lw['w_in'], lw['q_gain'], lw['k_gain'], _tile(t, 1024))
    ya = _s5_mixer(z, bn, seq, lw['s5'], lw['index'])
    yb = _conv_mixer(z, bn, seq, lw['conv_w'], _tile(seq, 1024))
    yc = _na_mixer(qn, kn, vb, bn, seq, lw['na_bias'], lw['index'])
    yd = _fnet_mixer(z, bn, seq)
    x2d = _merge(x2d, h, ya, yb, yc, yd, lw['w_glu'], lw['w_in'], lw['w_br'], lw['w_o'],
                 _tile(t, 512), 512)
    return _ffn_ple(x2d, lw['g_ffn'], lw['w_ffn_in'], lw['w_ffn_out'], lw['g_ple'], p2d, lw['w_ple_gate'],
                    lw['w_ple_proj'], _tile(t, 512), 512, 512)


def kernel(x_prompt, x_sample, p_prompt, p_sample, g_mix, w_in, s5_lam_re, s5_lam_im, s5_log_dt, s5_b_re,
           s5_b_im, s5_c_re, s5_c_im, s5_d, w_glu, conv_w, q_gain, k_gain, rel_bias, w_br, w_o, g_ffn,
           w_ffn_in, w_ffn_out, g_ple, w_ple_gate, w_ple_proj):
    depth = w_in.shape[0]
    s5_tabs = jax.vmap(_s5_tables)(s5_lam_re, s5_lam_im, s5_log_dt, s5_b_re, s5_b_im, s5_c_re, s5_c_im, s5_d)
    na_bias = jax.vmap(_na_bias_table)(rel_bias)
    q_gain_heads = jnp.tile(q_gain.astype(F32), (1, N_HEADS))
    k_gain_heads = jnp.tile(k_gain.astype(F32), (1, N_HEADS))
    layers = []
    for i in range(depth):
        layers.append(dict(
            index=i, g_mix=g_mix[i].astype(F32)[None], w_in=w_in[i].astype(BF16), s5=s5_tabs,
            w_glu=w_glu[i].astype(BF16), conv_w=conv_w[i], q_gain=q_gain_heads[i][None],
            k_gain=k_gain_heads[i][None], na_bias=na_bias, w_br=w_br[i].astype(BF16), w_o=w_o[i].astype(BF16),
            g_ffn=g_ffn[i].astype(F32)[None], w_ffn_in=w_ffn_in[i].astype(BF16),
            w_ffn_out=w_ffn_out[i].astype(BF16), g_ple=g_ple[i].astype(F32)[None],
            w_ple_gate=w_ple_gate[i].astype(BF16), w_ple_proj=w_ple_proj[i].astype(BF16)))

    def trunk(x, p):
        bn, seq, _ = x.shape
        x2d = x.reshape(bn * seq, D_MODEL)
        for i in range(depth):
            x2d = _layer(x2d, p[i].reshape(bn * seq, PLE_DIM), bn, seq, layers[i])
        return x2d.reshape(bn, seq, D_MODEL)

    return trunk(x_prompt, p_prompt), trunk(x_sample, p_sample)
```

```python
import functools
import math

import numpy as np
import jax
import jax.numpy as jnp
from jax import lax
from jax.experimental import pallas as pl
from jax.experimental.pallas import tpu as pltpu

F32 = jnp.float32
BF16 = jnp.bfloat16

D_MODEL = 2048
BW = 512
MIX_IN = 8 * BW
N_BRANCH = 4
S5_H = 16
S5_G = BW // S5_H
S5_P = 64
S5_CHUNK = 16
LANE = 128
S5_JB = BW // LANE
S5_GB = LANE // S5_H
S5_CW = S5_CHUNK * LANE
SCAN_ROWS = 8
SCAN_STEPS = (1, 2, 4)
SCAN_ROWPOW = len(SCAN_STEPS)
SCAN_TILEPOW = SCAN_ROWPOW + 1
N_HEADS = 8
HEAD_DIM = 64
GRID_W = 64
NA_ROWS = 8
NA_COLS = 16
FNET_GW = 128
D_FF = 5632
PLE_DIM = 256
EPS = 1e-6
NEG_INF = -1e30
VMEM_LIMIT = 56 * 1024 * 1024


def _cparams(sem):
    return pltpu.CompilerParams(dimension_semantics=sem, vmem_limit_bytes=VMEM_LIMIT)


def _rms(x, g):
    ms = jnp.mean(x * x, axis=-1, keepdims=True)
    return x * lax.rsqrt(ms + EPS) * g


def _dot(a, b):
    return jnp.dot(a, b, preferred_element_type=F32)


INPROJ_TN = 2 * BW
Z_COLS = MIX_IN - INPROJ_TN
QK_STEP = 2
V_STEP = 3


def _head_mean_sq(x, ones_ref):
    x2 = x * x
    hi = x2.astype(BF16)
    lo = (x2 - hi.astype(F32)).astype(BF16)
    return _dot(hi, ones_ref[...]) + _dot(lo, ones_ref[...])


def _inproj_body(x_ref, g_ref, w_ref, qg_ref, kg_ref, ones_ref, z_ref, q_ref, k_ref, v_ref, h_scr):
    j = pl.program_id(1)

    @pl.when(j == 0)
    def _():
        h_scr[...] = _rms(x_ref[...], g_ref[...]).astype(BF16)

    acc = _dot(h_scr[...], w_ref[...])

    @pl.when(j != QK_STEP)
    def _():
        z_ref[...] = acc

    @pl.when(j == QK_STEP)
    def _():
        q, k = acc[:, :BW], acc[:, BW:]
        qn = q * lax.rsqrt(_head_mean_sq(q, ones_ref) + EPS) * qg_ref[...]
        kn = k * lax.rsqrt(_head_mean_sq(k, ones_ref) + EPS) * kg_ref[...]
        q_ref[...] = (qn * (1.0 / math.sqrt(HEAD_DIM))).astype(BF16)
        k_ref[...] = kn.astype(BF16)

    @pl.when(j == V_STEP)
    def _():
        v_ref[...] = acc[:, :BW].astype(BF16)


def _inproj(x2d, g, w_bf, q_gain, k_gain, tm):
    t = x2d.shape[0]
    ones = np.kron(np.eye(N_HEADS, dtype=np.float32), np.full((HEAD_DIM, HEAD_DIM), 1.0 / HEAD_DIM, np.float32))
    vec = pl.BlockSpec((1, BW), lambda i, j: (0, 0))
    head = pl.BlockSpec((tm, BW), lambda i, j: (i, 0))
    return pl.pallas_call(
        _inproj_body,
        grid=(t // tm, MIX_IN // INPROJ_TN),
        in_specs=[
            pl.BlockSpec((tm, D_MODEL), lambda i, j: (i, 0)),
            pl.BlockSpec((1, D_MODEL), lambda i, j: (0, 0)),
            pl.BlockSpec((D_MODEL, INPROJ_TN), lambda i, j: (0, j)),
            vec, vec, pl.BlockSpec((BW, BW), lambda i, j: (0, 0)),
        ],
        out_specs=[pl.BlockSpec((tm, INPROJ_TN), lambda i, j: (i, jnp.where(j >= QK_STEP, j - 1, j))),
                   head, head, head],
        out_shape=[jax.ShapeDtypeStruct((t, Z_COLS), F32)] + [jax.ShapeDtypeStruct((t, BW), BF16)] * 3,
        scratch_shapes=[pltpu.VMEM((tm, D_MODEL), BF16)],
        compiler_params=_cparams(("parallel", "arbitrary")),
        name="inproj",
    )(x2d, g, w_bf, q_gain, k_gain, jnp.asarray(ones, BF16))


def _s5_tables(lam_re, lam_im, log_dt, b_re, b_im, c_re, c_im, d_skip):
    f = lambda a: a.astype(F32)
    lam_re, lam_im, b_re, b_im, c_re, c_im = map(f, (lam_re, lam_im, b_re, b_im, c_re, c_im))
    dt = jnp.exp(f(log_dt))[..., None]
    lr, li = lam_re * dt, lam_im * dt
    mag = jnp.exp(lr)
    ar, ai = mag * jnp.cos(li), mag * jnp.sin(li)
    den = lam_re * lam_re + lam_im * lam_im
    fr = ((ar - 1.0) * lam_re + ai * lam_im) / den
    fi = (ai * lam_re - (ar - 1.0) * lam_im) / den
    bbr = (fr[..., None] * b_re - fi[..., None] * b_im).transpose(0, 1, 3, 2)
    bbi = (fr[..., None] * b_im + fi[..., None] * b_re).transpose(0, 1, 3, 2)
    n = jnp.arange(S5_CHUNK + 1, dtype=F32)[:, None, None, None]
    pr = jnp.exp(n * lr) * jnp.cos(n * li)
    pi = jnp.exp(n * lr) * jnp.sin(n * li)
    abr = pr[:, :, :, None, :] * bbr - pi[:, :, :, None, :] * bbi
    abi = pr[:, :, :, None, :] * bbi + pi[:, :, :, None, :] * bbr
    kk = jnp.sum(c_re[None, :, :, :, None, :] * abr[:, :, :, None, :, :]
                 - c_im[None, :, :, :, None, :] * abi[:, :, :, None, :, :], axis=-1)
    nq = S5_GB // 2
    dsk = f(d_skip).reshape(S5_G, S5_H)
    k0 = kk[0, 0] + kk[0, 1] + dsk[:, :, None] * jnp.asarray(np.eye(S5_H, dtype=np.float32))[None]
    kcomb = jnp.concatenate([kk[1:S5_CHUNK, 1][::-1], k0[None], kk[1:S5_CHUNK, 0]], axis=0)
    kcomb = kcomb.reshape(2 * S5_CHUNK - 1, S5_JB, S5_GB, S5_H, S5_H).transpose(1, 0, 4, 2, 3)
    kcomb = kcomb.reshape(S5_JB, 2 * S5_CHUNK - 1, S5_H, LANE)
    y = jnp.stack([kcomb[:, S5_CHUNK - 1 - sp:2 * S5_CHUNK - 1 - sp] for sp in range(S5_CHUNK)], axis=1)
    y = y.transpose(0, 1, 3, 2, 4).reshape(S5_JB, S5_CHUNK, S5_H, S5_CW)
    out_group = (np.arange(S5_CW) // S5_H) % S5_GB
    row_group = np.arange(S5_GB)
    mmask = row_group[:, None, None] == out_group[None, None, :]
    m = jnp.where(mmask[None, None], y[:, :, None], 0.0).reshape(S5_JB, S5_CW, S5_CW)

    sw = 2 * S5_GB * 2 * S5_P
    ab = jnp.stack([abr, abi], axis=0)
    sel = jnp.stack([ab[:, :S5_CHUNK, 0][:, ::-1], ab[:, :S5_CHUNK, 1]], axis=0)
    sel = sel.reshape(2, 2, S5_CHUNK, S5_JB, nq, 2, S5_H, S5_P)
    zs = sel.transpose(3, 2, 6, 0, 4, 1, 5, 7).reshape(S5_JB, S5_CHUNK, S5_H, sw)
    scol = np.arange(sw)
    state_group = ((scol // (2 * 2 * S5_P)) % nq) * 2 + (scol // S5_P) % 2
    smask = row_group[:, None, None] == state_group[None, None, :]
    s = jnp.where(smask[None, None], zs[:, :, None], 0.0).reshape(S5_JB, S5_CW, sw)

    def lanes_gh(c):
        return c.reshape(2, S5_JB, S5_GB, S5_H, S5_P).transpose(0, 1, 4, 2, 3).reshape(2, S5_JB, S5_P, 1, LANE)

    def pow_gh(a):
        e = jnp.stack([a[1:, 0], a[1:, 1][::-1]], axis=0)
        e = e.reshape(2, S5_CHUNK, S5_JB, S5_GB, S5_P).transpose(0, 2, 4, 1, 3)
        return jnp.broadcast_to(e[..., None], e.shape + (S5_H,)).reshape(2, S5_JB, S5_P, S5_CHUNK, LANE)

    cr, ci, er, ei = lanes_gh(c_re), lanes_gh(c_im), pow_gh(pr), pow_gh(pi)
    w = jnp.stack([cr * er - ci * ei, -(cr * ei + ci * er)], axis=2)
    w = w.reshape(2, S5_JB, 2, S5_P, S5_CW)
    omask = (np.arange(nq)[:, None, None] * 2 + np.arange(2)[None, :, None]) == out_group[None, None, :]
    o = jnp.where(omask[None, None, :, None, :, None, :], w[:, :, None, :, None, :, :], 0.0)
    o = o.reshape(2, S5_JB, S5_GB * 2 * S5_P, S5_CW)

    nn = jnp.arange(SCAN_ROWS + 1, dtype=F32)[:, None, None, None] * S5_CHUNK
    cw = nq * 2 * S5_P
    pw = jnp.stack([(jnp.exp(nn * lr) * jnp.cos(nn * li)).reshape(SCAN_ROWS + 1, 2, S5_JB, cw),
                    (jnp.exp(nn * lr) * jnp.sin(nn * li)).reshape(SCAN_ROWS + 1, 2, S5_JB, cw)], axis=2)
    bcast = lambda a: jnp.broadcast_to(a[:, :, None], (2, 2, SCAN_ROWS, S5_JB, cw))
    rowpow = jnp.stack([pw[:SCAN_ROWS, 0], pw[:SCAN_ROWS, 1][::-1]], axis=0).transpose(0, 2, 1, 3, 4)
    tab = jnp.stack([bcast(pw[k]) for k in SCAN_STEPS] + [rowpow, bcast(pw[SCAN_ROWS])], axis=0)
    tab = tab.transpose(4, 1, 0, 2, 3, 5)
    return (m.astype(BF16), s.astype(BF16), o[0].astype(BF16), o[1].astype(BF16), tab)


def _chunk_rows(u_ref):
    rbs = u_ref.shape[0] // S5_CHUNK
    return jnp.concatenate([u_ref[pl.ds(t, rbs, stride=S5_CHUNK), :] for t in range(S5_CHUNK)], axis=-1).astype(BF16)


def _s5_state_body(u_ref, s_ref, o_ref):
    o_ref[...] = _dot(_chunk_rows(u_ref), s_ref[0])


def _s5_scan_body(sf_ref, sb_ref, tab_ref, xf_ref, xb_ref, carry):
    @pl.when(pl.program_id(1) == 0)
    def _():
        carry[...] = jnp.zeros_like(carry)

    ntile = sf_ref.shape[0] // SCAN_ROWS
    nq = sf_ref.shape[1] // (2 * LANE)
    row = lax.broadcasted_iota(jnp.int32, (SCAN_ROWS, LANE), 0)

    def shift(v, k, d):
        if d == 0:
            return jnp.where(row >= k, pltpu.roll(v, k, axis=0), 0.0)
        return jnp.where(row < SCAN_ROWS - k, pltpu.roll(v, SCAN_ROWS - k, axis=0), 0.0)

    def cmul(d, kind, q, xr, xi):
        a_r = tab_ref[0, d, kind, 0, :, q * LANE:(q + 1) * LANE]
        a_i = tab_ref[0, d, kind, 1, :, q * LANE:(q + 1) * LANE]
        return a_r * xr - a_i * xi, a_r * xi + a_i * xr

    def tile(d, s_ref, x_ref, r0, cr):
        last = SCAN_ROWS - 1 if d == 0 else 0
        new = []
        for q in range(nq):
            re = slice(q * 2 * LANE, q * 2 * LANE + LANE)
            im = slice(q * 2 * LANE + LANE, (q + 1) * 2 * LANE)
            yr, yi = s_ref[pl.ds(r0, SCAN_ROWS), re], s_ref[pl.ds(r0, SCAN_ROWS), im]
            for kind, k in enumerate(SCAN_STEPS):
                tr, ti = cmul(d, kind, q, shift(yr, k, d), shift(yi, k, d))
                yr, yi = yr + tr, yi + ti
            c_r, c_i = cr[2 * q], cr[2 * q + 1]
            er, ei = cmul(d, SCAN_ROWPOW, q, c_r, c_i)
            x_ref[pl.ds(r0, SCAN_ROWS), re] = shift(yr, 1, d) + er
            x_ref[pl.ds(r0, SCAN_ROWS), im] = shift(yi, 1, d) + ei
            nr, ni = cmul(d, SCAN_TILEPOW, q, c_r, c_i)
            new.append(jnp.broadcast_to(yr[last:last + 1], (SCAN_ROWS, LANE)) + nr)
            new.append(jnp.broadcast_to(yi[last:last + 1], (SCAN_ROWS, LANE)) + ni)
        return tuple(new)

    def step(t, cs):
        cf, cb = cs
        rf = pl.multiple_of(t * SCAN_ROWS, SCAN_ROWS)
        rb = pl.multiple_of((ntile - 1 - t) * SCAN_ROWS, SCAN_ROWS)
        return tile(0, sf_ref, xf_ref, rf, cf), tile(1, sb_ref, xb_ref, rb, cb)

    init = tuple(tuple(carry[d, :, t * LANE:(t + 1) * LANE] for t in range(2 * nq)) for d in range(2))
    cf, cb = lax.fori_loop(0, ntile, step, init)
    for d, cs in enumerate((cf, cb)):
        for t in range(2 * nq):
            carry[d, :, t * LANE:(t + 1) * LANE] = cs[t]


def _s5_out_body(u_ref, xf_ref, xb_ref, m_ref, of_ref, ob_ref, y_ref):
    y = (_dot(_chunk_rows(u_ref), m_ref[0])
         + _dot(xf_ref[...].astype(BF16), of_ref[0])
         + _dot(xb_ref[...].astype(BF16), ob_ref[0]))
    y = jax.nn.gelu(y)
    for t in range(S5_CHUNK):
        y_ref[pl.ds(t, y.shape[0], stride=S5_CHUNK), :] = y[:, t * LANE:(t + 1) * LANE]


def _s5_mixer(z, bn, seq, tabs, li):
    m, s, of, ob, scan_tab = tabs
    nc = seq // S5_CHUNK
    rbs = min(nc, 256)
    nrb = nc // rbs
    sw = s.shape[-1]
    half = sw // 2
    u_spec = pl.BlockSpec((rbs * S5_CHUNK, LANE), lambda j, b, r: (b * nrb + r, j))
    st = pl.pallas_call(
        _s5_state_body,
        grid=(S5_JB, bn, nrb),
        in_specs=[u_spec, pl.BlockSpec((None, 1, S5_CW, sw), lambda j, b, r: (li, j, 0, 0))],
        out_specs=pl.BlockSpec((rbs, sw), lambda j, b, r: (r, b * S5_JB + j)),
        out_shape=jax.ShapeDtypeStruct((nc, bn * S5_JB * sw), F32),
        compiler_params=_cparams(("parallel", "parallel", "parallel")),
        name="s5_state",
    )(z, s)

    nseq = bn * S5_JB
    sbs = min(nc, 256)
    nsb = nc // sbs
    xf, xb = pl.pallas_call(
        _s5_scan_body,
        grid=(nseq, nsb),
        in_specs=[
            pl.BlockSpec((sbs, half), lambda q, i: (i, 2 * q)),
            pl.BlockSpec((sbs, half), lambda q, i: (nsb - 1 - i, 2 * q + 1)),
            pl.BlockSpec((None, 1) + scan_tab.shape[2:], lambda q, i: (li, q % S5_JB, 0, 0, 0, 0, 0)),
        ],
        out_specs=[
            pl.BlockSpec((sbs, half), lambda q, i: (i, q)),
            pl.BlockSpec((sbs, half), lambda q, i: (nsb - 1 - i, q)),
        ],
        out_shape=[jax.ShapeDtypeStruct((nc, nseq * half), F32)] * 2,
        scratch_shapes=[pltpu.VMEM((2, SCAN_ROWS, half), F32)],
        compiler_params=_cparams(("parallel", "arbitrary")),
        name="s5_scan",
    )(st, st, scan_tab)

    x_spec = pl.BlockSpec((rbs, half), lambda j, b, r: (r, b * S5_JB + j))
    y = pl.pallas_call(
        _s5_out_body,
        grid=(S5_JB, bn, nrb),
        in_specs=[
            u_spec, x_spec, x_spec,
            pl.BlockSpec((None, 1, S5_CW, S5_CW), lambda j, b, r: (li, j, 0, 0)),
            pl.BlockSpec((None, 1, half, S5_CW), lambda j, b, r: (li, j, 0, 0)),
            pl.BlockSpec((None, 1, half, S5_CW), lambda j, b, r: (li, j, 0, 0)),
        ],
        out_specs=u_spec,
        out_shape=jax.ShapeDtypeStruct((bn * seq, BW), F32),
        compiler_params=_cparams(("parallel", "parallel", "parallel")),
        name="s5_out",
    )(z, xf, xb, m, of, ob)
    return y


def _conv_body(b_ref, c_ref, v_ref, cp_ref, vp_ref, cn_ref, vn_ref, w_ref, o_ref):
    i = pl.program_id(1)
    tc = c_ref.shape[1]
    z = c_ref[0] * v_ref[0]
    zp = cp_ref[0][7:8, :] * vp_ref[0][7:8, :]
    zn = cn_ref[0][0:1, :] * vn_ref[0][0:1, :]
    zp = jnp.where(i == 0, 0.0, zp)
    zn = jnp.where(i == pl.num_programs(1) - 1, 0.0, zn)
    row = lax.broadcasted_iota(jnp.int32, z.shape, 0)
    up = jnp.where(row == 0, zp, pltpu.roll(z, 1, axis=0))
    dn = jnp.where(row == tc - 1, zn, pltpu.roll(z, tc - 1, axis=0))
    w = w_ref[...]
    y = w[0:1, :] * up + w[1:2, :] * z + w[2:3, :] * dn
    o_ref[0] = (b_ref[0] * y).astype(o_ref.dtype)


def _conv_mixer(z, bn, seq, conv_w, tc):
    z3 = z.reshape(bn, seq, Z_COLS)
    nb8 = tc // 8
    last8 = seq // 8 - 1
    main = lambda col: pl.BlockSpec((1, tc, BW), lambda b, i: (b, i, col))
    prev = lambda col: pl.BlockSpec((1, 8, BW), lambda b, i: (b, jnp.maximum(i * nb8 - 1, 0), col))
    nxt = lambda col: pl.BlockSpec((1, 8, BW), lambda b, i: (b, jnp.minimum((i + 1) * nb8, last8), col))
    y = pl.pallas_call(
        _conv_body,
        grid=(bn, seq // tc),
        in_specs=[main(1), main(2), main(3), prev(2), prev(3), nxt(2), nxt(3),
                  pl.BlockSpec((3, BW), lambda b, i: (0, 0))],
        out_specs=pl.BlockSpec((1, tc, BW), lambda b, i: (b, i, 0)),
        out_shape=jax.ShapeDtypeStruct((bn, seq, BW), BF16),
        compiler_params=_cparams(("parallel", "parallel")),
        name="short_conv",
    )(z3, z3, z3, z3, z3, z3, z3, conv_w.astype(F32))
    return y.reshape(bn * seq, BW)


def _na_bias_table(rel_bias):
    qc = np.arange(GRID_W)[:, None]
    kc = np.arange(GRID_W)[None, :]
    cs = np.clip(qc - NA_COLS // 2, 0, GRID_W - NA_COLS)
    valid = (kc >= cs) & (kc < cs + NA_COLS)
    dc = np.clip(kc - qc + (NA_COLS - 1), 0, 2 * NA_COLS - 2)
    pick = (dc[None] == np.arange(2 * NA_COLS - 1)[:, None, None]).astype(np.float32)
    b = jnp.einsum('hrd,dqk->hrqk', rel_bias.astype(F32), jnp.asarray(pick), precision=lax.Precision.HIGHEST)
    b = jnp.where(valid[None, None], b, NEG_INF)
    bq = b.transpose(0, 2, 1, 3)
    ndr = 2 * NA_ROWS - 2
    bt = [bq[:, :, o:o + ndr].reshape(N_HEADS, GRID_W, ndr * GRID_W) for o in range(2)]
    tab = jnp.stack([bt[s % 2][:, :, (s - s % 2) * GRID_W:(s - s % 2 + NA_ROWS) * GRID_W]
                     for s in range(NA_ROWS)], axis=0)
    return tab.reshape(NA_ROWS, N_HEADS // 2, 2 * GRID_W, NA_ROWS * GRID_W)


def _na_body(q_ref, kp_ref, kc_ref, kn_ref, vp_ref, vc_ref, vn_ref, bias_ref, o_ref, k_scr, v_scr, *, rows):
    i = pl.program_id(1)
    blk = NA_ROWS * GRID_W
    for n, (kr, vr) in enumerate(((kp_ref, vp_ref), (kc_ref, vc_ref), (kn_ref, vn_ref))):
        k_scr[n * blk:(n + 1) * blk, :] = kr[0]
        v_scr[n * blk:(n + 1) * blk, :] = vr[0]
    first_head = lax.broadcasted_iota(jnp.int32, (GRID_W, 2 * HEAD_DIM), 1) < HEAD_DIM
    for ir in range(NA_ROWS):
        r = i * NA_ROWS + ir
        rs = jnp.clip(r - NA_ROWS // 2, 0, rows - NA_ROWS)
        off = pl.multiple_of((rs - (i - 1) * NA_ROWS) * GRID_W, GRID_W)
        bias_row = rs - r + (NA_ROWS - 1)
        pair_lanes = [slice(hp * 2 * HEAD_DIM, (hp + 1) * 2 * HEAD_DIM) for hp in range(N_HEADS // 2)]
        scores = []
        for hp, lanes in enumerate(pair_lanes):
            q = q_ref[0, ir * GRID_W:(ir + 1) * GRID_W, lanes]
            zero = jnp.zeros_like(q)
            q2 = jnp.concatenate([jnp.where(first_head, q, zero), jnp.where(first_head, zero, q)], axis=0)
            kw = k_scr[pl.ds(off, blk), lanes]
            s = lax.dot_general(q2, kw, (((1,), (1,)), ((), ())), preferred_element_type=F32)
            scores.append(s + bias_ref[bias_row, hp])
        probs = []
        for s in scores:
            e = jnp.exp(s - jnp.max(s, axis=-1, keepdims=True))
            probs.append((e.astype(BF16), jnp.sum(e, axis=-1, keepdims=True)))
        for lanes, (e, l) in zip(pair_lanes, probs):
            o = _dot(e, v_scr[pl.ds(off, blk), lanes]) / l
            o_ref[0, ir * GRID_W:(ir + 1) * GRID_W, lanes] = jnp.where(
                first_head, o[:GRID_W], o[GRID_W:]).astype(o_ref.dtype)


def _na_mixer(qn, kn, vb, bn, seq, bias_tab, li):
    rows = seq // GRID_W
    nblk = rows // NA_ROWS
    assert rows % NA_ROWS == 0 and nblk >= 2
    blk = NA_ROWS * GRID_W
    q3, k3, v3 = (a.reshape(bn, seq, BW) for a in (qn, kn, vb))
    cur = pl.BlockSpec((1, blk, BW), lambda b, i: (b, i, 0))
    prev = pl.BlockSpec((1, blk, BW), lambda b, i: (b, jnp.maximum(i - 1, 0), 0))
    nxt = pl.BlockSpec((1, blk, BW), lambda b, i: (b, jnp.minimum(i + 1, nblk - 1), 0))
    y = pl.pallas_call(
        functools.partial(_na_body, rows=rows),
        grid=(bn, nblk),
        in_specs=[cur, prev, cur, nxt, prev, cur, nxt,
                  pl.BlockSpec((None, NA_ROWS, N_HEADS // 2, 2 * GRID_W, blk), lambda b, i: (li, 0, 0, 0, 0))],
        out_specs=pl.BlockSpec((1, blk, BW), lambda b, i: (b, i, 0)),
        out_shape=jax.ShapeDtypeStruct((bn, seq, BW), BF16),
        scratch_shapes=[pltpu.VMEM((3 * blk, BW), BF16), pltpu.VMEM((3 * blk, BW), BF16)],
        compiler_params=_cparams(("parallel", "arbitrary")),
        name="nbr_attention",
    )(q3, k3, k3, k3, v3, v3, v3, bias_tab)
    return y.reshape(bn * seq, BW)


def _fnet_factors(seq):
    n1 = 1 << ((seq.bit_length() - 1 + 1) // 2)
    assert seq == n1 * (seq // n1) and seq & (seq - 1) == 0
    return n1, seq // n1


@functools.lru_cache(maxsize=None)
def _fnet_tables(seq):
    n1, n2 = _fnet_factors(seq)

    def cs(n, scale):
        idx = np.outer(np.arange(n), np.arange(n)) % n
        ang = 2.0 * np.pi * idx / n
        return np.cos(ang) * scale, np.sin(ang) * scale

    cc, sc = cs(FNET_GW, FNET_GW ** -0.5)
    chan = np.concatenate([cc, sc], axis=1)
    c1, s1 = cs(n1, n1 ** -0.5)
    d1 = np.block([[c1, -s1], [s1, c1]])
    c2, s2 = cs(n2, n2 ** -0.5)
    d2 = np.concatenate([c2, -s2], axis=1)
    ang = 2.0 * np.pi * (np.outer(np.arange(n2), np.arange(n1)) % seq) / seq
    twr = np.broadcast_to(np.cos(ang)[:, :, None], (n2, n1, LANE))
    twi = np.broadcast_to(np.sin(ang)[:, :, None], (n2, n1, LANE))
    return (np.asarray(chan, np.float32), np.asarray(d1, np.float32), np.asarray(d2, np.float32),
            np.ascontiguousarray(twr, np.float32), np.ascontiguousarray(twi, np.float32))


FNET_NG = BW // FNET_GW


def _fnet1_body(x0_ref, x1_ref, x2_ref, x3_ref, chan_ref, d1_ref, twr_ref, twi_ref, o_ref):
    n1, tb = x0_ref.shape[1], x0_ref.shape[2]
    xs = [r.reshape(n1 * tb, FNET_GW) for r in (x0_ref, x1_ref, x2_ref, x3_ref)]
    o2 = o_ref.reshape(2 * FNET_NG * n1 * tb, FNET_GW)
    for t in range(tb):
        pq = [_dot(x[pl.ds(t, n1, stride=tb), :].astype(BF16), chan_ref[...]) for x in xs]
        p = jnp.concatenate([a[:, :FNET_GW] for a in pq], axis=1)
        q = jnp.concatenate([a[:, FNET_GW:] for a in pq], axis=1)
        a = _dot(d1_ref[...], jnp.concatenate([p, q], axis=0).astype(BF16))
        ar, ai = a[:n1], a[n1:]
        twr = jnp.concatenate([twr_ref[t]] * FNET_NG, axis=1)
        twi = jnp.concatenate([twi_ref[t]] * FNET_NG, axis=1)
        for ri, v in enumerate((ar * twr - ai * twi, ar * twi + ai * twr)):
            for g in range(FNET_NG):
                o2[pl.ds((ri * FNET_NG + g) * n1 * tb + t, n1, stride=tb), :] = v[:, g * FNET_GW:(g + 1) * FNET_GW]


def _fnet2_body(a_ref, d2_ref, o_ref):
    kb, n2 = a_ref.shape[3], a_ref.shape[4]
    o2 = o_ref.reshape(FNET_NG * n2 * kb, FNET_GW)
    for k in range(kb):
        a = jnp.concatenate([jnp.concatenate([a_ref[0, ri, g, k] for g in range(FNET_NG)], axis=1)
                             for ri in range(2)], axis=0).astype(BF16)
        y = _dot(d2_ref[...], a)
        for g in range(FNET_NG):
            o2[pl.ds(g * n2 * kb + k, n2, stride=kb), :] = y[:, g * FNET_GW:(g + 1) * FNET_GW]


def _fnet_mixer(z, bn, seq):
    n1, n2 = _fnet_factors(seq)
    chan, d1, d2, twr, twi = _fnet_tables(seq)
    tb = 8
    zf = z.reshape(bn, n1, n2, Z_COLS)
    first_group = (Z_COLS - BW) // FNET_GW
    a = pl.pallas_call(
        _fnet1_body,
        grid=(bn, n2 // tb),
        in_specs=[pl.BlockSpec((1, n1, tb, FNET_GW), lambda b, j, g=g: (b, 0, j, first_group + g))
                  for g in range(FNET_NG)] + [
            pl.BlockSpec((FNET_GW, 2 * FNET_GW), lambda b, j: (0, 0)),
            pl.BlockSpec((2 * n1, 2 * n1), lambda b, j: (0, 0)),
            pl.BlockSpec((tb, n1, LANE), lambda b, j: (j, 0, 0)),
            pl.BlockSpec((tb, n1, LANE), lambda b, j: (j, 0, 0)),
        ],
        out_specs=pl.BlockSpec((1, 2, FNET_NG, n1, tb, FNET_GW), lambda b, j: (b, 0, 0, 0, j, 0)),
        out_shape=jax.ShapeDtypeStruct((bn, 2, FNET_NG, n1, n2, FNET_GW), F32),
        compiler_params=_cparams(("parallel", "parallel")),
        name="fnet_stage1",
    )(*([zf] * FNET_NG), jnp.asarray(chan, BF16), jnp.asarray(d1, BF16), jnp.asarray(twr), jnp.asarray(twi))

    kb = 8
    y = pl.pallas_call(
        _fnet2_body,
        grid=(bn, n1 // kb),
        in_specs=[
            pl.BlockSpec((1, 2, FNET_NG, kb, n2, FNET_GW), lambda b, i: (b, 0, 0, i, 0, 0)),
            pl.BlockSpec((n2, 2 * n2), lambda b, i: (0, 0)),
        ],
        out_specs=pl.BlockSpec((FNET_NG, 1, n2, kb, FNET_GW), lambda b, i: (0, b, 0, i, 0)),
        out_shape=jax.ShapeDtypeStruct((FNET_NG, bn, n2, n1, FNET_GW), F32),
        compiler_params=_cparams(("parallel", "parallel")),
        name="fnet_stage2",
    )(a, jnp.asarray(d2, BF16))
    return y.reshape(FNET_NG, bn * seq, FNET_GW)


def _merge_body(x_ref, g_ref, ya_ref, yb_ref, yc_ref, yd_ref, wglu_ref, wg0_ref, wg1_ref, wg2_ref, wg3_ref,
                wbr_ref, wo_ref, o_ref, h_scr, ya_scr):
    @pl.when(pl.program_id(1) == 0)
    def _():
        x = x_ref[...]
        h_scr[...] = _rms(x, g_ref[...]).astype(BF16)
        ya = ya_ref[...]
        ya_scr[...] = (ya * jax.nn.sigmoid(_dot(ya.astype(BF16), wglu_ref[...]))).astype(BF16)
        o_ref[...] = x

    h = h_scr[...]
    yd = jnp.concatenate([yd_ref[g] for g in range(FNET_NG)], axis=1).astype(BF16)
    ys = (ya_scr[...], yb_ref[...], yc_ref[...], yd)
    merged = None
    for kb, wg_ref in enumerate((wg0_ref, wg1_ref, wg2_ref, wg3_ref)):
        term = jax.nn.sigmoid(_dot(h, wg_ref[...])) * _dot(ys[kb], wbr_ref[kb])
        merged = term if merged is None else merged + term
    o_ref[...] += _dot(merged.astype(BF16), wo_ref[...])


def _merge(x2d, g, ya, yb, yc, yd, w_glu, w_in_bf, w_br, w_o, tm, tn):
    t = x2d.shape[0]
    row = lambda w: pl.BlockSpec((tm, w), lambda i, n: (i, 0))
    gate = lambda kb: pl.BlockSpec((D_MODEL, tn), lambda i, n: (0, (MIX_IN + kb * D_MODEL) // tn + n))
    return pl.pallas_call(
        _merge_body,
        grid=(t // tm, D_MODEL // tn),
        in_specs=[
            row(D_MODEL), pl.BlockSpec((1, D_MODEL), lambda i, n: (0, 0)),
            row(BW), row(BW), row(BW), pl.BlockSpec((FNET_NG, tm, FNET_GW), lambda i, n: (0, i, 0)),
            pl.BlockSpec((BW, BW), lambda i, n: (0, 0)),
            gate(0), gate(1), gate(2), gate(3),
            pl.BlockSpec((N_BRANCH, BW, tn), lambda i, n: (0, 0, n)),
            pl.BlockSpec((tn, D_MODEL), lambda i, n: (n, 0)),
        ],
        out_specs=row(D_MODEL),
        out_shape=jax.ShapeDtypeStruct((t, D_MODEL), F32),
        scratch_shapes=[pltpu.VMEM((tm, D_MODEL), BF16), pltpu.VMEM((tm, BW), BF16)],
        compiler_params=_cparams(("parallel", "arbitrary")),
        name="gated_merge",
    )(x2d, g, ya, yb, yc, yd, w_glu, w_in_bf, w_in_bf, w_in_bf, w_in_bf, w_br, w_o)


def _ffn_ple_body(x_ref, gf_ref, wa_ref, wb_ref, wo_ref, gp_ref, p_ref, wg_ref, wp_ref, o_ref, h_scr, *, nf, tn):
    f = pl.program_id(1)

    @pl.when(f == 0)
    def _():
        x = x_ref[...]
        h_scr[...] = _rms(x, gf_ref[...]).astype(BF16)
        o_ref[...] = x

    @pl.when(f < nf)
    def _():
        h = h_scr[...]
        a = _dot(h, wa_ref[...])
        b = _dot(h, wb_ref[...])
        o_ref[...] += _dot((jax.nn.silu(a) * b).astype(BF16), wo_ref[...])

    @pl.when(f == nf)
    def _():
        h_scr[...] = _rms(o_ref[...], gp_ref[...]).astype(BF16)
        p = p_ref[...].astype(BF16)
        for n in range(D_MODEL // tn):
            cols = slice(n * tn, (n + 1) * tn)
            pg = jax.nn.sigmoid(_dot(h_scr[...], wg_ref[:, cols]))
            o_ref[:, cols] += pg * _dot(p, wp_ref[:, cols])


def _ffn_ple(x2d, g_ffn, w_in, w_out, g_ple, p2d, w_gate, w_proj, tm, tf, tn):
    t = x2d.shape[0]
    nf = D_FF // tf
    ffn_step = lambda f: jnp.minimum(f, nf - 1)
    return pl.pallas_call(
        functools.partial(_ffn_ple_body, nf=nf, tn=tn),
        grid=(t // tm, nf + 1),
        in_specs=[
            pl.BlockSpec((tm, D_MODEL), lambda i, f: (i, 0)),
            pl.BlockSpec((1, D_MODEL), lambda i, f: (0, 0)),
            pl.BlockSpec((D_MODEL, tf), lambda i, f: (0, ffn_step(f))),
            pl.BlockSpec((D_MODEL, tf), lambda i, f: (0, nf + ffn_step(f))),
            pl.BlockSpec((tf, D_MODEL), lambda i, f: (ffn_step(f), 0)),
            pl.BlockSpec((1, D_MODEL), lambda i, f: (0, 0)),
            pl.BlockSpec((tm, PLE_DIM), lambda i, f: (i, 0)),
            pl.BlockSpec((D_MODEL, D_MODEL), lambda i, f: (0, 0)),
            pl.BlockSpec((PLE_DIM, D_MODEL), lambda i, f: (0, 0)),
        ],
        out_specs=pl.BlockSpec((tm, D_MODEL), lambda i, f: (i, 0)),
        out_shape=jax.ShapeDtypeStruct((t, D_MODEL), F32),
        scratch_shapes=[pltpu.VMEM((tm, D_MODEL), BF16)],
        compiler_params=_cparams(("parallel", "arbitrary")),
        name="swiglu_ffn_ple",
    )(x2d, g_ffn, w_in, w_in, w_out, g_ple, p2d, w_gate, w_proj)


def _tile(t, want):
    return min(t, want)


def _layer(x2d, p2d, bn, seq, lw):
    t = bn * seq
    z, qn, kn, vb = _inproj(x2d, lw['g_mix'], lw['w_in'], lw['q_gain'], lw['k_gain'], _tile(t, 1024))
    ya = _s5_mixer(z, bn, seq, lw['s5'], lw['index'])
    yb = _conv_mixer(z, bn, seq, lw['conv_w'], _tile(seq, 1024))
    yc = _na_mixer(qn, kn, vb, bn, seq, lw['na_bias'], lw['index'])
    yd = _fnet_mixer(z, bn, seq)
    x2d = _merge(x2d, lw['g_mix'], ya, yb, yc, yd, lw['w_glu'], lw['w_in'], lw['w_br'], lw['w_o'],
                 _tile(t, 512), 512)
    return _ffn_ple(x2d, lw['g_ffn'], lw['w_ffn_in'], lw['w_ffn_out'], lw['g_ple'], p2d, lw['w_ple_gate'],
                    lw['w_ple_proj'], _tile(t, 512), 512, 512)


def kernel(x_prompt, x_sample, p_prompt, p_sample, g_mix, w_in, s5_lam_re, s5_lam_im, s5_log_dt, s5_b_re,
           s5_b_im, s5_c_re, s5_c_im, s5_d, w_glu, conv_w, q_gain, k_gain, rel_bias, w_br, w_o, g_ffn,
           w_ffn_in, w_ffn_out, g_ple, w_ple_gate, w_ple_proj):
    depth = w_in.shape[0]
    s5_tabs = jax.vmap(_s5_tables)(s5_lam_re, s5_lam_im, s5_log_dt, s5_b_re, s5_b_im, s5_c_re, s5_c_im, s5_d)
    na_bias = jax.vmap(_na_bias_table)(rel_bias)
    q_gain_heads = jnp.tile(q_gain.astype(F32), (1, N_HEADS))
    k_gain_heads = jnp.tile(k_gain.astype(F32), (1, N_HEADS))
    layers = []
    for i in range(depth):
        layers.append(dict(
            index=i, g_mix=g_mix[i].astype(F32)[None], w_in=w_in[i].astype(BF16), s5=s5_tabs,
            w_glu=w_glu[i].astype(BF16), conv_w=conv_w[i], q_gain=q_gain_heads[i][None],
            k_gain=k_gain_heads[i][None], na_bias=na_bias, w_br=w_br[i].astype(BF16), w_o=w_o[i].astype(BF16),
            g_ffn=g_ffn[i].astype(F32)[None], w_ffn_in=w_ffn_in[i].astype(BF16),
            w_ffn_out=w_ffn_out[i].astype(BF16), g_ple=g_ple[i].astype(F32)[None],
            w_ple_gate=w_ple_gate[i].astype(BF16), w_ple_proj=w_ple_proj[i].astype(BF16)))

    def trunk(x, p):
        bn, seq, _ = x.shape
        x2d = x.reshape(bn * seq, D_MODEL)
        for i in range(depth):
            x2d = _layer(x2d, p[i].reshape(bn * seq, PLE_DIM), bn, seq, layers[i])
        return x2d.reshape(bn, seq, D_MODEL)

    return trunk(x_prompt, p_prompt), trunk(x_sample, p_sample)
```

```python
import functools
import math

import numpy as np
import jax
import jax.numpy as jnp
from jax import lax
from jax.experimental import pallas as pl
from jax.experimental.pallas import tpu as pltpu

F32 = jnp.float32
BF16 = jnp.bfloat16

D_MODEL = 2048
BW = 512
MIX_IN = 8 * BW
N_BRANCH = 4
S5_H = 16
S5_G = BW // S5_H
S5_P = 64
S5_CHUNK = 16
LANE = 128
S5_JB = BW // LANE
S5_GB = LANE // S5_H
S5_CW = S5_CHUNK * LANE
SCAN_ROWS = 8
SCAN_STEPS = (1, 2, 4)
SCAN_ROWPOW = len(SCAN_STEPS)
SCAN_TILEPOW = SCAN_ROWPOW + 1
N_HEADS = 8
HEAD_DIM = 64
GRID_W = 64
NA_ROWS = 8
NA_COLS = 16
FNET_GW = 128
D_FF = 5632
PLE_DIM = 256
EPS = 1e-6
NEG_INF = -1e30
VMEM_LIMIT = 56 * 1024 * 1024


def _cparams(sem):
    return pltpu.CompilerParams(dimension_semantics=sem, vmem_limit_bytes=VMEM_LIMIT)


def _rms(x, g):
    ms = jnp.mean(x * x, axis=-1, keepdims=True)
    return x * lax.rsqrt(ms + EPS) * g


def _dot(a, b):
    return jnp.dot(a, b, preferred_element_type=F32)


INPROJ_TN = 2 * BW
Z_COLS = MIX_IN - INPROJ_TN
QK_STEP = 2
V_STEP = 3


def _head_mean_sq(x, ones_ref):
    x2 = x * x
    hi = x2.astype(BF16)
    lo = (x2 - hi.astype(F32)).astype(BF16)
    return _dot(hi, ones_ref[...]) + _dot(lo, ones_ref[...])


def _inproj_body(x_ref, g_ref, w_ref, qg_ref, kg_ref, ones_ref, z_ref, q_ref, k_ref, v_ref, h_scr):
    j = pl.program_id(1)

    @pl.when(j == 0)
    def _():
        h_scr[...] = _rms(x_ref[...], g_ref[...]).astype(BF16)

    acc = _dot(h_scr[...], w_ref[...])

    @pl.when(j != QK_STEP)
    def _():
        z_ref[...] = acc

    @pl.when(j == QK_STEP)
    def _():
        q, k = acc[:, :BW], acc[:, BW:]
        qn = q * lax.rsqrt(_head_mean_sq(q, ones_ref) + EPS) * qg_ref[...]
        kn = k * lax.rsqrt(_head_mean_sq(k, ones_ref) + EPS) * kg_ref[...]
        q_ref[...] = (qn * (1.0 / math.sqrt(HEAD_DIM))).astype(BF16)
        k_ref[...] = kn.astype(BF16)

    @pl.when(j == V_STEP)
    def _():
        v_ref[...] = acc[:, :BW].astype(BF16)


def _inproj(x2d, g, w_bf, q_gain, k_gain, tm):
    t = x2d.shape[0]
    ones = np.kron(np.eye(N_HEADS, dtype=np.float32), np.full((HEAD_DIM, HEAD_DIM), 1.0 / HEAD_DIM, np.float32))
    vec = pl.BlockSpec((1, BW), lambda i, j: (0, 0))
    head = pl.BlockSpec((tm, BW), lambda i, j: (i, 0))
    return pl.pallas_call(
        _inproj_body,
        grid=(t // tm, MIX_IN // INPROJ_TN),
        in_specs=[
            pl.BlockSpec((tm, D_MODEL), lambda i, j: (i, 0)),
            pl.BlockSpec((1, D_MODEL), lambda i, j: (0, 0)),
            pl.BlockSpec((D_MODEL, INPROJ_TN), lambda i, j: (0, j)),
            vec, vec, pl.BlockSpec((BW, BW), lambda i, j: (0, 0)),
        ],
        out_specs=[pl.BlockSpec((tm, INPROJ_TN), lambda i, j: (i, jnp.where(j >= QK_STEP, j - 1, j))),
                   head, head, head],
        out_shape=[jax.ShapeDtypeStruct((t, Z_COLS), F32)] + [jax.ShapeDtypeStruct((t, BW), BF16)] * 3,
        scratch_shapes=[pltpu.VMEM((tm, D_MODEL), BF16)],
        compiler_params=_cparams(("parallel", "arbitrary")),
        name="inproj",
    )(x2d, g, w_bf, q_gain, k_gain, jnp.asarray(ones, BF16))


def _s5_tables(lam_re, lam_im, log_dt, b_re, b_im, c_re, c_im, d_skip):
    f = lambda a: a.astype(F32)
    lam_re, lam_im, b_re, b_im, c_re, c_im = map(f, (lam_re, lam_im, b_re, b_im, c_re, c_im))
    dt = jnp.exp(f(log_dt))[..., None]
    lr, li = lam_re * dt, lam_im * dt
    mag = jnp.exp(lr)
    ar, ai = mag * jnp.cos(li), mag * jnp.sin(li)
    den = lam_re * lam_re + lam_im * lam_im
    fr = ((ar - 1.0) * lam_re + ai * lam_im) / den
    fi = (ai * lam_re - (ar - 1.0) * lam_im) / den
    bbr = (fr[..., None] * b_re - fi[..., None] * b_im).transpose(0, 1, 3, 2)
    bbi = (fr[..., None] * b_im + fi[..., None] * b_re).transpose(0, 1, 3, 2)
    n = jnp.arange(S5_CHUNK + 1, dtype=F32)[:, None, None, None]
    pr = jnp.exp(n * lr) * jnp.cos(n * li)
    pi = jnp.exp(n * lr) * jnp.sin(n * li)
    abr = pr[:, :, :, None, :] * bbr - pi[:, :, :, None, :] * bbi
    abi = pr[:, :, :, None, :] * bbi + pi[:, :, :, None, :] * bbr
    kk = jnp.sum(c_re[None, :, :, :, None, :] * abr[:, :, :, None, :, :]
                 - c_im[None, :, :, :, None, :] * abi[:, :, :, None, :, :], axis=-1)
    tt = np.arange(S5_CHUNK)
    nq = S5_GB // 2
    dsk = f(d_skip).reshape(S5_G, S5_H)
    k0 = kk[0, 0] + kk[0, 1] + dsk[:, :, None] * jnp.asarray(np.eye(S5_H, dtype=np.float32))[None]
    kcomb = jnp.concatenate([kk[1:S5_CHUNK, 1][::-1], k0[None], kk[1:S5_CHUNK, 0]], axis=0)
    kcomb = kcomb.reshape(2 * S5_CHUNK - 1, S5_JB, S5_GB, S5_H, S5_H).transpose(1, 0, 4, 2, 3)
    kcomb = kcomb.reshape(S5_JB, 2 * S5_CHUNK - 1, S5_H, LANE)
    lag = tt[None, :] - tt[:, None]
    y = kcomb[:, lag + (S5_CHUNK - 1)]
    y = y.transpose(0, 1, 3, 2, 4).reshape(S5_JB, S5_CHUNK, S5_H, S5_CW)
    out_group = (np.arange(S5_CW) // S5_H) % S5_GB
    row_group = np.arange(S5_GB)
    mmask = row_group[:, None, None] == out_group[None, None, :]
    m = jnp.where(mmask[None, None], y[:, :, None], 0.0).reshape(S5_JB, S5_CW, S5_CW)

    sw = 2 * S5_GB * 2 * S5_P
    ab = jnp.stack([abr, abi], axis=0)
    spow = (S5_CHUNK - 1 - tt, tt)
    sel = jnp.stack([ab[:, spow[d], d] for d in range(2)], axis=0)
    sel = sel.reshape(2, 2, S5_CHUNK, S5_JB, nq, 2, S5_H, S5_P)
    zs = sel.transpose(3, 2, 6, 0, 4, 1, 5, 7).reshape(S5_JB, S5_CHUNK, S5_H, sw)
    scol = np.arange(sw)
    state_group = ((scol // (2 * 2 * S5_P)) % nq) * 2 + (scol // S5_P) % 2
    smask = row_group[:, None, None] == state_group[None, None, :]
    s = jnp.where(smask[None, None], zs[:, :, None], 0.0).reshape(S5_JB, S5_CW, sw)

    def lanes_gh(c):
        return c.reshape(2, S5_JB, S5_GB, S5_H, S5_P).transpose(0, 1, 4, 2, 3).reshape(2, S5_JB, S5_P, 1, LANE)

    def pow_gh(a):
        e = jnp.stack([a[tt + 1, 0], a[S5_CHUNK - tt, 1]], axis=0)
        e = e.reshape(2, S5_CHUNK, S5_JB, S5_GB, S5_P).transpose(0, 2, 4, 1, 3)
        return jnp.broadcast_to(e[..., None], e.shape + (S5_H,)).reshape(2, S5_JB, S5_P, S5_CHUNK, LANE)

    cr, ci, er, ei = lanes_gh(c_re), lanes_gh(c_im), pow_gh(pr), pow_gh(pi)
    w = jnp.stack([cr * er - ci * ei, -(cr * ei + ci * er)], axis=2)
    w = w.reshape(2, S5_JB, 2, S5_P, S5_CW)
    omask = (np.arange(nq)[:, None, None] * 2 + np.arange(2)[None, :, None]) == out_group[None, None, :]
    o = jnp.where(omask[None, None, :, None, :, None, :], w[:, :, None, :, None, :, :], 0.0)
    o = o.reshape(2, S5_JB, S5_GB * 2 * S5_P, S5_CW)

    nn = jnp.arange(SCAN_ROWS + 1, dtype=F32)[:, None, None, None] * S5_CHUNK
    cw = nq * 2 * S5_P
    pw = jnp.stack([(jnp.exp(nn * lr) * jnp.cos(nn * li)).reshape(SCAN_ROWS + 1, 2, S5_JB, cw),
                    (jnp.exp(nn * lr) * jnp.sin(nn * li)).reshape(SCAN_ROWS + 1, 2, S5_JB, cw)], axis=2)
    rows = np.arange(SCAN_ROWS)
    bcast = lambda a: jnp.broadcast_to(a[:, :, None], (2, 2, SCAN_ROWS, S5_JB, cw))
    rowpow = jnp.stack([pw[rows, 0], pw[SCAN_ROWS - 1 - rows, 1]], axis=0).transpose(0, 2, 1, 3, 4)
    tab = jnp.stack([bcast(pw[k]) for k in SCAN_STEPS] + [rowpow, bcast(pw[SCAN_ROWS])], axis=0)
    tab = tab.transpose(4, 1, 0, 2, 3, 5)
    return (m.astype(BF16), s.astype(BF16), o[0].astype(BF16), o[1].astype(BF16), tab)


def _chunk_rows(u_ref):
    rbs = u_ref.shape[0] // S5_CHUNK
    return jnp.concatenate([u_ref[pl.ds(t, rbs, stride=S5_CHUNK), :] for t in range(S5_CHUNK)], axis=-1).astype(BF16)


def _s5_state_body(u_ref, s_ref, o_ref):
    o_ref[...] = _dot(_chunk_rows(u_ref), s_ref[0])


def _s5_scan_body(sf_ref, sb_ref, tab_ref, xf_ref, xb_ref, carry):
    @pl.when(pl.program_id(1) == 0)
    def _():
        carry[...] = jnp.zeros_like(carry)

    ntile = sf_ref.shape[0] // SCAN_ROWS
    nq = sf_ref.shape[1] // (2 * LANE)
    row = lax.broadcasted_iota(jnp.int32, (SCAN_ROWS, LANE), 0)

    def shift(v, k, d):
        if d == 0:
            return jnp.where(row >= k, pltpu.roll(v, k, axis=0), 0.0)
        return jnp.where(row < SCAN_ROWS - k, pltpu.roll(v, SCAN_ROWS - k, axis=0), 0.0)

    def cmul(d, kind, q, xr, xi):
        a_r = tab_ref[0, d, kind, 0, :, q * LANE:(q + 1) * LANE]
        a_i = tab_ref[0, d, kind, 1, :, q * LANE:(q + 1) * LANE]
        return a_r * xr - a_i * xi, a_r * xi + a_i * xr

    def tile(d, s_ref, x_ref, r0, cr):
        last = SCAN_ROWS - 1 if d == 0 else 0
        new = []
        for q in range(nq):
            re = slice(q * 2 * LANE, q * 2 * LANE + LANE)
            im = slice(q * 2 * LANE + LANE, (q + 1) * 2 * LANE)
            yr, yi = s_ref[pl.ds(r0, SCAN_ROWS), re], s_ref[pl.ds(r0, SCAN_ROWS), im]
            for kind, k in enumerate(SCAN_STEPS):
                tr, ti = cmul(d, kind, q, shift(yr, k, d), shift(yi, k, d))
                yr, yi = yr + tr, yi + ti
            c_r, c_i = cr[2 * q], cr[2 * q + 1]
            er, ei = cmul(d, SCAN_ROWPOW, q, c_r, c_i)
            x_ref[pl.ds(r0, SCAN_ROWS), re] = shift(yr, 1, d) + er
            x_ref[pl.ds(r0, SCAN_ROWS), im] = shift(yi, 1, d) + ei
            nr, ni = cmul(d, SCAN_TILEPOW, q, c_r, c_i)
            new.append(jnp.broadcast_to(yr[last:last + 1], (SCAN_ROWS, LANE)) + nr)
            new.append(jnp.broadcast_to(yi[last:last + 1], (SCAN_ROWS, LANE)) + ni)
        return tuple(new)

    def step(t, cs):
        cf, cb = cs
        rf = pl.multiple_of(t * SCAN_ROWS, SCAN_ROWS)
        rb = pl.multiple_of((ntile - 1 - t) * SCAN_ROWS, SCAN_ROWS)
        return tile(0, sf_ref, xf_ref, rf, cf), tile(1, sb_ref, xb_ref, rb, cb)

    init = tuple(tuple(carry[d, :, t * LANE:(t + 1) * LANE] for t in range(2 * nq)) for d in range(2))
    cf, cb = lax.fori_loop(0, ntile, step, init)
    for d, cs in enumerate((cf, cb)):
        for t in range(2 * nq):
            carry[d, :, t * LANE:(t + 1) * LANE] = cs[t]


def _s5_out_body(u_ref, xf_ref, xb_ref, m_ref, of_ref, ob_ref, y_ref):
    y = (_dot(_chunk_rows(u_ref), m_ref[0])
         + _dot(xf_ref[...].astype(BF16), of_ref[0])
         + _dot(xb_ref[...].astype(BF16), ob_ref[0]))
    y = jax.nn.gelu(y)
    for t in range(S5_CHUNK):
        y_ref[pl.ds(t, y.shape[0], stride=S5_CHUNK), :] = y[:, t * LANE:(t + 1) * LANE]


def _s5_mixer(z, bn, seq, tabs, li):
    m, s, of, ob, scan_tab = tabs
    nc = seq // S5_CHUNK
    rbs = min(nc, 256)
    nrb = nc // rbs
    sw = s.shape[-1]
    half = sw // 2
    u_spec = pl.BlockSpec((rbs * S5_CHUNK, LANE), lambda j, b, r: (b * nrb + r, j))
    st = pl.pallas_call(
        _s5_state_body,
        grid=(S5_JB, bn, nrb),
        in_specs=[u_spec, pl.BlockSpec((None, 1, S5_CW, sw), lambda j, b, r: (li, j, 0, 0))],
        out_specs=pl.BlockSpec((rbs, sw), lambda j, b, r: (r, b * S5_JB + j)),
        out_shape=jax.ShapeDtypeStruct((nc, bn * S5_JB * sw), F32),
        compiler_params=_cparams(("parallel", "parallel", "parallel")),
        name="s5_state",
    )(z, s)

    nseq = bn * S5_JB
    sbs = min(nc, 256)
    nsb = nc // sbs
    xf, xb = pl.pallas_call(
        _s5_scan_body,
        grid=(nseq, nsb),
        in_specs=[
            pl.BlockSpec((sbs, half), lambda q, i: (i, 2 * q)),
            pl.BlockSpec((sbs, half), lambda q, i: (nsb - 1 - i, 2 * q + 1)),
            pl.BlockSpec((None, 1) + scan_tab.shape[2:], lambda q, i: (li, q % S5_JB, 0, 0, 0, 0, 0)),
        ],
        out_specs=[
            pl.BlockSpec((sbs, half), lambda q, i: (i, q)),
            pl.BlockSpec((sbs, half), lambda q, i: (nsb - 1 - i, q)),
        ],
        out_shape=[jax.ShapeDtypeStruct((nc, nseq * half), F32)] * 2,
        scratch_shapes=[pltpu.VMEM((2, SCAN_ROWS, half), F32)],
        compiler_params=_cparams(("parallel", "arbitrary")),
        name="s5_scan",
    )(st, st, scan_tab)

    x_spec = pl.BlockSpec((rbs, half), lambda j, b, r: (r, b * S5_JB + j))
    y = pl.pallas_call(
        _s5_out_body,
        grid=(S5_JB, bn, nrb),
        in_specs=[
            u_spec, x_spec, x_spec,
            pl.BlockSpec((None, 1, S5_CW, S5_CW), lambda j, b, r: (li, j, 0, 0)),
            pl.BlockSpec((None, 1, half, S5_CW), lambda j, b, r: (li, j, 0, 0)),
            pl.BlockSpec((None, 1, half, S5_CW), lambda j, b, r: (li, j, 0, 0)),
        ],
        out_specs=u_spec,
        out_shape=jax.ShapeDtypeStruct((bn * seq, BW), F32),
        compiler_params=_cparams(("parallel", "parallel", "parallel")),
        name="s5_out",
    )(z, xf, xb, m, of, ob)
    return y


def _conv_body(b_ref, c_ref, v_ref, cp_ref, vp_ref, cn_ref, vn_ref, w_ref, o_ref):
    i = pl.program_id(1)
    tc = c_ref.shape[1]
    z = c_ref[0] * v_ref[0]
    zp = cp_ref[0][7:8, :] * vp_ref[0][7:8, :]
    zn = cn_ref[0][0:1, :] * vn_ref[0][0:1, :]
    zp = jnp.where(i == 0, 0.0, zp)
    zn = jnp.where(i == pl.num_programs(1) - 1, 0.0, zn)
    row = lax.broadcasted_iota(jnp.int32, z.shape, 0)
    up = jnp.where(row == 0, zp, pltpu.roll(z, 1, axis=0))
    dn = jnp.where(row == tc - 1, zn, pltpu.roll(z, tc - 1, axis=0))
    w = w_ref[...]
    y = w[0:1, :] * up + w[1:2, :] * z + w[2:3, :] * dn
    o_ref[0] = (b_ref[0] * y).astype(o_ref.dtype)


def _conv_mixer(z, bn, seq, conv_w, tc):
    z3 = z.reshape(bn, seq, Z_COLS)
    nb8 = tc // 8
    last8 = seq // 8 - 1
    main = lambda col: pl.BlockSpec((1, tc, BW), lambda b, i: (b, i, col))
    prev = lambda col: pl.BlockSpec((1, 8, BW), lambda b, i: (b, jnp.maximum(i * nb8 - 1, 0), col))
    nxt = lambda col: pl.BlockSpec((1, 8, BW), lambda b, i: (b, jnp.minimum((i + 1) * nb8, last8), col))
    y = pl.pallas_call(
        _conv_body,
        grid=(bn, seq // tc),
        in_specs=[main(1), main(2), main(3), prev(2), prev(3), nxt(2), nxt(3),
                  pl.BlockSpec((3, BW), lambda b, i: (0, 0))],
        out_specs=pl.BlockSpec((1, tc, BW), lambda b, i: (b, i, 0)),
        out_shape=jax.ShapeDtypeStruct((bn, seq, BW), BF16),
        compiler_params=_cparams(("parallel", "parallel")),
        name="short_conv",
    )(z3, z3, z3, z3, z3, z3, z3, conv_w.astype(F32))
    return y.reshape(bn * seq, BW)


def _na_bias_table(rel_bias):
    qc = np.arange(GRID_W)[:, None]
    kc = np.arange(GRID_W)[None, :]
    cs = np.clip(qc - NA_COLS // 2, 0, GRID_W - NA_COLS)
    valid = (kc >= cs) & (kc < cs + NA_COLS)
    dc = np.clip(kc - qc + (NA_COLS - 1), 0, 2 * NA_COLS - 2)
    pick = (dc[None] == np.arange(2 * NA_COLS - 1)[:, None, None]).astype(np.float32)
    b = jnp.einsum('hrd,dqk->hrqk', rel_bias.astype(F32), jnp.asarray(pick), precision=lax.Precision.HIGHEST)
    b = jnp.where(valid[None, None], b, NEG_INF)
    bq = b.transpose(0, 2, 1, 3)
    ndr = 2 * NA_ROWS - 2
    bt = [bq[:, :, o:o + ndr].reshape(N_HEADS, GRID_W, ndr * GRID_W) for o in range(2)]
    tab = jnp.stack([bt[s % 2][:, :, (s - s % 2) * GRID_W:(s - s % 2 + NA_ROWS) * GRID_W]
                     for s in range(NA_ROWS)], axis=0)
    return tab.reshape(NA_ROWS, N_HEADS // 2, 2 * GRID_W, NA_ROWS * GRID_W)


def _na_body(q_ref, kp_ref, kc_ref, kn_ref, vp_ref, vc_ref, vn_ref, bias_ref, o_ref, k_scr, v_scr, *, rows):
    i = pl.program_id(1)
    blk = NA_ROWS * GRID_W
    for n, (kr, vr) in enumerate(((kp_ref, vp_ref), (kc_ref, vc_ref), (kn_ref, vn_ref))):
        k_scr[n * blk:(n + 1) * blk, :] = kr[0]
        v_scr[n * blk:(n + 1) * blk, :] = vr[0]
    first_head = lax.broadcasted_iota(jnp.int32, (GRID_W, 2 * HEAD_DIM), 1) < HEAD_DIM
    for ir in range(NA_ROWS):
        r = i * NA_ROWS + ir
        rs = jnp.clip(r - NA_ROWS // 2, 0, rows - NA_ROWS)
        off = pl.multiple_of((rs - (i - 1) * NA_ROWS) * GRID_W, GRID_W)
        bias_row = rs - r + (NA_ROWS - 1)
        pair_lanes = [slice(hp * 2 * HEAD_DIM, (hp + 1) * 2 * HEAD_DIM) for hp in range(N_HEADS // 2)]
        scores = []
        for hp, lanes in enumerate(pair_lanes):
            q = q_ref[0, ir * GRID_W:(ir + 1) * GRID_W, lanes]
            zero = jnp.zeros_like(q)
            q2 = jnp.concatenate([jnp.where(first_head, q, zero), jnp.where(first_head, zero, q)], axis=0)
            kw = k_scr[pl.ds(off, blk), lanes]
            s = lax.dot_general(q2, kw, (((1,), (1,)), ((), ())), preferred_element_type=F32)
            scores.append(s + bias_ref[bias_row, hp])
        probs = []
        for s in scores:
            e = jnp.exp(s - jnp.max(s, axis=-1, keepdims=True))
            probs.append((e.astype(BF16), jnp.sum(e, axis=-1, keepdims=True)))
        for lanes, (e, l) in zip(pair_lanes, probs):
            o = _dot(e, v_scr[pl.ds(off, blk), lanes]) / l
            o_ref[0, ir * GRID_W:(ir + 1) * GRID_W, lanes] = jnp.where(
                first_head, o[:GRID_W], o[GRID_W:]).astype(o_ref.dtype)


def _na_mixer(qn, kn, vb, bn, seq, bias_tab, li):
    rows = seq // GRID_W
    nblk = rows // NA_ROWS
    assert rows % NA_ROWS == 0 and nblk >= 2
    blk = NA_ROWS * GRID_W
    q3, k3, v3 = (a.reshape(bn, seq, BW) for a in (qn, kn, vb))
    cur = pl.BlockSpec((1, blk, BW), lambda b, i: (b, i, 0))
    prev = pl.BlockSpec((1, blk, BW), lambda b, i: (b, jnp.maximum(i - 1, 0), 0))
    nxt = pl.BlockSpec((1, blk, BW), lambda b, i: (b, jnp.minimum(i + 1, nblk - 1), 0))
    y = pl.pallas_call(
        functools.partial(_na_body, rows=rows),
        grid=(bn, nblk),
        in_specs=[cur, prev, cur, nxt, prev, cur, nxt,
                  pl.BlockSpec((None, NA_ROWS, N_HEADS // 2, 2 * GRID_W, blk), lambda b, i: (li, 0, 0, 0, 0))],
        out_specs=pl.BlockSpec((1, blk, BW), lambda b, i: (b, i, 0)),
        out_shape=jax.ShapeDtypeStruct((bn, seq, BW), BF16),
        scratch_shapes=[pltpu.VMEM((3 * blk, BW), BF16), pltpu.VMEM((3 * blk, BW), BF16)],
        compiler_params=_cparams(("parallel", "arbitrary")),
        name="nbr_attention",
    )(q3, k3, k3, k3, v3, v3, v3, bias_tab)
    return y.reshape(bn * seq, BW)


def _fnet_factors(seq):
    n1 = 1 << ((seq.bit_length() - 1 + 1) // 2)
    assert seq == n1 * (seq // n1) and seq & (seq - 1) == 0
    return n1, seq // n1


@functools.lru_cache(maxsize=None)
def _fnet_tables(seq):
    n1, n2 = _fnet_factors(seq)

    def cs(n, scale):
        idx = np.outer(np.arange(n), np.arange(n)) % n
        ang = 2.0 * np.pi * idx / n
        return np.cos(ang) * scale, np.sin(ang) * scale

    cc, sc = cs(FNET_GW, FNET_GW ** -0.5)
    chan = np.concatenate([cc, sc], axis=1)
    c1, s1 = cs(n1, n1 ** -0.5)
    d1 = np.block([[c1, -s1], [s1, c1]])
    c2, s2 = cs(n2, n2 ** -0.5)
    d2 = np.concatenate([c2, -s2], axis=1)
    ang = 2.0 * np.pi * (np.outer(np.arange(n2), np.arange(n1)) % seq) / seq
    twr = np.broadcast_to(np.cos(ang)[:, :, None], (n2, n1, LANE))
    twi = np.broadcast_to(np.sin(ang)[:, :, None], (n2, n1, LANE))
    return (np.asarray(chan, np.float32), np.asarray(d1, np.float32), np.asarray(d2, np.float32),
            np.ascontiguousarray(twr, np.float32), np.ascontiguousarray(twi, np.float32))


FNET_NG = BW // FNET_GW


def _fnet1_body(x0_ref, x1_ref, x2_ref, x3_ref, chan_ref, d1_ref, twr_ref, twi_ref, o_ref):
    n1, tb = x0_ref.shape[1], x0_ref.shape[2]
    xs = [r.reshape(n1 * tb, FNET_GW) for r in (x0_ref, x1_ref, x2_ref, x3_ref)]
    o2 = o_ref.reshape(2 * FNET_NG * n1 * tb, FNET_GW)
    for t in range(tb):
        pq = [_dot(x[pl.ds(t, n1, stride=tb), :].astype(BF16), chan_ref[...]) for x in xs]
        p = jnp.concatenate([a[:, :FNET_GW] for a in pq], axis=1)
        q = jnp.concatenate([a[:, FNET_GW:] for a in pq], axis=1)
        a = _dot(d1_ref[...], jnp.concatenate([p, q], axis=0).astype(BF16))
        ar, ai = a[:n1], a[n1:]
        twr = jnp.concatenate([twr_ref[t]] * FNET_NG, axis=1)
        twi = jnp.concatenate([twi_ref[t]] * FNET_NG, axis=1)
        for ri, v in enumerate((ar * twr - ai * twi, ar * twi + ai * twr)):
            for g in range(FNET_NG):
                o2[pl.ds((ri * FNET_NG + g) * n1 * tb + t, n1, stride=tb), :] = v[:, g * FNET_GW:(g + 1) * FNET_GW]


def _fnet2_body(a_ref, d2_ref, o_ref):
    kb, n2 = a_ref.shape[3], a_ref.shape[4]
    o2 = o_ref.reshape(FNET_NG * n2 * kb, FNET_GW)
    for k in range(kb):
        a = jnp.concatenate([jnp.concatenate([a_ref[0, ri, g, k] for g in range(FNET_NG)], axis=1)
                             for ri in range(2)], axis=0).astype(BF16)
        y = _dot(d2_ref[...], a)
        for g in range(FNET_NG):
            o2[pl.ds(g * n2 * kb + k, n2, stride=kb), :] = y[:, g * FNET_GW:(g + 1) * FNET_GW]


def _fnet_mixer(z, bn, seq):
    n1, n2 = _fnet_factors(seq)
    chan, d1, d2, twr, twi = _fnet_tables(seq)
    tb = 8
    zf = z.reshape(bn, n1, n2, Z_COLS)
    first_group = (Z_COLS - BW) // FNET_GW
    a = pl.pallas_call(
        _fnet1_body,
        grid=(bn, n2 // tb),
        in_specs=[pl.BlockSpec((1, n1, tb, FNET_GW), lambda b, j, g=g: (b, 0, j, first_group + g))
                  for g in range(FNET_NG)] + [
            pl.BlockSpec((FNET_GW, 2 * FNET_GW), lambda b, j: (0, 0)),
            pl.BlockSpec((2 * n1, 2 * n1), lambda b, j: (0, 0)),
            pl.BlockSpec((tb, n1, LANE), lambda b, j: (j, 0, 0)),
            pl.BlockSpec((tb, n1, LANE), lambda b, j: (j, 0, 0)),
        ],
        out_specs=pl.BlockSpec((1, 2, FNET_NG, n1, tb, FNET_GW), lambda b, j: (b, 0, 0, 0, j, 0)),
        out_shape=jax.ShapeDtypeStruct((bn, 2, FNET_NG, n1, n2, FNET_GW), F32),
        compiler_params=_cparams(("parallel", "parallel")),
        name="fnet_stage1",
    )(*([zf] * FNET_NG), jnp.asarray(chan, BF16), jnp.asarray(d1, BF16), jnp.asarray(twr), jnp.asarray(twi))

    kb = 8
    y = pl.pallas_call(
        _fnet2_body,
        grid=(bn, n1 // kb),
        in_specs=[
            pl.BlockSpec((1, 2, FNET_NG, kb, n2, FNET_GW), lambda b, i: (b, 0, 0, i, 0, 0)),
            pl.BlockSpec((n2, 2 * n2), lambda b, i: (0, 0)),
        ],
        out_specs=pl.BlockSpec((FNET_NG, 1, n2, kb, FNET_GW), lambda b, i: (0, b, 0, i, 0)),
        out_shape=jax.ShapeDtypeStruct((FNET_NG, bn, n2, n1, FNET_GW), F32),
        compiler_params=_cparams(("parallel", "parallel")),
        name="fnet_stage2",
    )(a, jnp.asarray(d2, BF16))
    return y.reshape(FNET_NG, bn * seq, FNET_GW)


def _merge_body(x_ref, g_ref, ya_ref, yb_ref, yc_ref, yd_ref, wglu_ref, wg0_ref, wg1_ref, wg2_ref, wg3_ref,
                wbr_ref, wo_ref, o_ref, h_scr, ya_scr):
    @pl.when(pl.program_id(1) == 0)
    def _():
        x = x_ref[...]
        h_scr[...] = _rms(x, g_ref[...]).astype(BF16)
        ya = ya_ref[...]
        ya_scr[...] = (ya * jax.nn.sigmoid(_dot(ya.astype(BF16), wglu_ref[...]))).astype(BF16)
        o_ref[...] = x

    h = h_scr[...]
    yd = jnp.concatenate([yd_ref[g] for g in range(FNET_NG)], axis=1).astype(BF16)
    ys = (ya_scr[...], yb_ref[...], yc_ref[...], yd)
    merged = None
    for kb, wg_ref in enumerate((wg0_ref, wg1_ref, wg2_ref, wg3_ref)):
        term = jax.nn.sigmoid(_dot(h, wg_ref[...])) * _dot(ys[kb], wbr_ref[kb])
        merged = term if merged is None else merged + term
    o_ref[...] += _dot(merged.astype(BF16), wo_ref[...])


def _merge(x2d, g, ya, yb, yc, yd, w_glu, w_in_bf, w_br, w_o, tm, tn):
    t = x2d.shape[0]
    row = lambda w: pl.BlockSpec((tm, w), lambda i, n: (i, 0))
    gate = lambda kb: pl.BlockSpec((D_MODEL, tn), lambda i, n: (0, (MIX_IN + kb * D_MODEL) // tn + n))
    return pl.pallas_call(
        _merge_body,
        grid=(t // tm, D_MODEL // tn),
        in_specs=[
            row(D_MODEL), pl.BlockSpec((1, D_MODEL), lambda i, n: (0, 0)),
            row(BW), row(BW), row(BW), pl.BlockSpec((FNET_NG, tm, FNET_GW), lambda i, n: (0, i, 0)),
            pl.BlockSpec((BW, BW), lambda i, n: (0, 0)),
            gate(0), gate(1), gate(2), gate(3),
            pl.BlockSpec((N_BRANCH, BW, tn), lambda i, n: (0, 0, n)),
            pl.BlockSpec((tn, D_MODEL), lambda i, n: (n, 0)),
        ],
        out_specs=row(D_MODEL),
        out_shape=jax.ShapeDtypeStruct((t, D_MODEL), F32),
        scratch_shapes=[pltpu.VMEM((tm, D_MODEL), BF16), pltpu.VMEM((tm, BW), BF16)],
        compiler_params=_cparams(("parallel", "arbitrary")),
        name="gated_merge",
    )(x2d, g, ya, yb, yc, yd, w_glu, w_in_bf, w_in_bf, w_in_bf, w_in_bf, w_br, w_o)


FFN_ROWS = 512


def _ffn_ple_body(x_ref, gf_ref, wa_ref, wb_ref, wo_ref, gp_ref, p_ref, wg_ref, wp_ref, o_ref, h_scr, *, nf, tn):
    f = pl.program_id(1)
    tm = x_ref.shape[0]
    row_blocks = [slice(r, r + FFN_ROWS) for r in range(0, tm, FFN_ROWS)]

    @pl.when(f == 0)
    def _():
        for rows in row_blocks:
            x = x_ref[rows, :]
            h_scr[rows, :] = _rms(x, gf_ref[...]).astype(BF16)
            o_ref[rows, :] = x

    @pl.when(f < nf)
    def _():
        for rows in row_blocks:
            h = h_scr[rows, :]
            a = _dot(h, wa_ref[...])
            b = _dot(h, wb_ref[...])
            o_ref[rows, :] += _dot((jax.nn.silu(a) * b).astype(BF16), wo_ref[...])

    @pl.when(f == nf)
    def _():
        for rows in row_blocks:
            h_scr[rows, :] = _rms(o_ref[rows, :], gp_ref[...]).astype(BF16)
            p = p_ref[rows, :].astype(BF16)
            for n in range(D_MODEL // tn):
                cols = slice(n * tn, (n + 1) * tn)
                pg = jax.nn.sigmoid(_dot(h_scr[rows, :], wg_ref[:, cols]))
                o_ref[rows, cols] += pg * _dot(p, wp_ref[:, cols])


def _ffn_ple(x2d, g_ffn, w_in, w_out, g_ple, p2d, w_gate, w_proj, tm, tf, tn):
    t = x2d.shape[0]
    nf = D_FF // tf
    ffn_step = lambda f: jnp.minimum(f, nf - 1)
    return pl.pallas_call(
        functools.partial(_ffn_ple_body, nf=nf, tn=tn),
        grid=(t // tm, nf + 1),
        in_specs=[
            pl.BlockSpec((tm, D_MODEL), lambda i, f: (i, 0), pipeline_mode=pl.Buffered(1)),
            pl.BlockSpec((1, D_MODEL), lambda i, f: (0, 0)),
            pl.BlockSpec((D_MODEL, tf), lambda i, f: (0, ffn_step(f))),
            pl.BlockSpec((D_MODEL, tf), lambda i, f: (0, nf + ffn_step(f))),
            pl.BlockSpec((tf, D_MODEL), lambda i, f: (ffn_step(f), 0)),
            pl.BlockSpec((1, D_MODEL), lambda i, f: (0, 0)),
            pl.BlockSpec((tm, PLE_DIM), lambda i, f: (i, 0)),
            pl.BlockSpec((D_MODEL, D_MODEL), lambda i, f: (0, 0), pipeline_mode=pl.Buffered(1)),
            pl.BlockSpec((PLE_DIM, D_MODEL), lambda i, f: (0, 0), pipeline_mode=pl.Buffered(1)),
        ],
        out_specs=pl.BlockSpec((tm, D_MODEL), lambda i, f: (i, 0)),
        out_shape=jax.ShapeDtypeStruct((t, D_MODEL), F32),
        scratch_shapes=[pltpu.VMEM((tm, D_MODEL), BF16)],
        compiler_params=_cparams(("parallel", "arbitrary")),
        name="swiglu_ffn_ple",
    )(x2d, g_ffn, w_in, w_in, w_out, g_ple, p2d, w_gate, w_proj)


def _tile(t, want):
    return min(t, want)


def _layer(x2d, p2d, bn, seq, lw):
    t = bn * seq
    z, qn, kn, vb = _inproj(x2d, lw['g_mix'], lw['w_in'], lw['q_gain'], lw['k_gain'], _tile(t, 1024))
    ya = _s5_mixer(z, bn, seq, lw['s5'], lw['index'])
    yb = _conv_mixer(z, bn, seq, lw['conv_w'], _tile(seq, 1024))
    yc = _na_mixer(qn, kn, vb, bn, seq, lw['na_bias'], lw['index'])
    yd = _fnet_mixer(z, bn, seq)
    x2d = _merge(x2d, lw['g_mix'], ya, yb, yc, yd, lw['w_glu'], lw['w_in'], lw['w_br'], lw['w_o'],
                 _tile(t, 512), 512)
    return _ffn_ple(x2d, lw['g_ffn'], lw['w_ffn_in'], lw['w_ffn_out'], lw['g_ple'], p2d, lw['w_ple_gate'],
                    lw['w_ple_proj'], _tile(t, 1024), 512, 512)


def kernel(x_prompt, x_sample, p_prompt, p_sample, g_mix, w_in, s5_lam_re, s5_lam_im, s5_log_dt, s5_b_re,
           s5_b_im, s5_c_re, s5_c_im, s5_d, w_glu, conv_w, q_gain, k_gain, rel_bias, w_br, w_o, g_ffn,
           w_ffn_in, w_ffn_out, g_ple, w_ple_gate, w_ple_proj):
    depth = w_in.shape[0]
    s5_tabs = jax.vmap(_s5_tables)(s5_lam_re, s5_lam_im, s5_log_dt, s5_b_re, s5_b_im, s5_c_re, s5_c_im, s5_d)
    na_bias = jax.vmap(_na_bias_table)(rel_bias)
    q_gain_heads = jnp.tile(q_gain.astype(F32), (1, N_HEADS))
    k_gain_heads = jnp.tile(k_gain.astype(F32), (1, N_HEADS))
    layers = []
    for i in range(depth):
        layers.append(dict(
            index=i, g_mix=g_mix[i].astype(F32)[None], w_in=w_in[i].astype(BF16), s5=s5_tabs,
            w_glu=w_glu[i].astype(BF16), conv_w=conv_w[i], q_gain=q_gain_heads[i][None],
            k_gain=k_gain_heads[i][None], na_bias=na_bias, w_br=w_br[i].astype(BF16), w_o=w_o[i].astype(BF16),
            g_ffn=g_ffn[i].astype(F32)[None], w_ffn_in=w_ffn_in[i].astype(BF16),
            w_ffn_out=w_ffn_out[i].astype(BF16), g_ple=g_ple[i].astype(F32)[None],
            w_ple_gate=w_ple_gate[i].astype(BF16), w_ple_proj=w_ple_proj[i].astype(BF16)))

    def trunk(x, p):
        bn, seq, _ = x.shape
        x2d = x.reshape(bn * seq, D_MODEL)
        for i in range(depth):
            x2d = _layer(x2d, p[i].reshape(bn * seq, PLE_DIM), bn, seq, layers[i])
        return x2d.reshape(bn, seq, D_MODEL)

    return trunk(x_prompt, p_prompt), trunk(x_sample, p_sample)
```
